```python
import jax, jax.numpy as jnp
from jax import lax
import numpy as np

D_MODEL = 2048
BATCH = 4
SEQ = 2048
DEPTH = 1
DEC_BATCH = 32
DEC_SEQ = 1
PAST_LEN = 16384
PAGE_SIZE = 128

MIX_WIDTH = D_MODEL
ATTN_WIDTH = MIX_WIDTH // 2
CONV_CH = MIX_WIDTH - ATTN_WIDTH
HEAD_DIM = 64
N_HEADS = ATTN_WIDTH // HEAD_DIM
N_KV_HEADS = 4
GROUP = N_HEADS // N_KV_HEADS
KV_DIM = N_KV_HEADS * HEAD_DIM
WINDOW = 128
CONV_WIDTH = 31
D_FF = 4 * D_MODEL
ROPE_THETA = 10000.0
EPS = 1e-6
IN_COLS = ATTN_WIDTH + 2 * KV_DIM + 2 * CONV_CH
NEG = -1e30

kernel_name = "hymba_conformer_swa_sink_decoder_step"


def rms_norm(x, g):
    xf = x.astype(jnp.float32)
    y = xf * lax.rsqrt(jnp.mean(xf * xf, axis=-1, keepdims=True) + EPS)
    return (y * g.astype(jnp.float32)).astype(x.dtype)


def layer_norm(x, g, b):
    xf = x.astype(jnp.float32)
    mu = jnp.mean(xf, axis=-1, keepdims=True)
    xc = xf - mu
    y = xc * lax.rsqrt(jnp.mean(xc * xc, axis=-1, keepdims=True) + EPS)
    return (y * g.astype(jnp.float32) + b.astype(jnp.float32)).astype(x.dtype)


def rope(x, pos):
    half = HEAD_DIM // 2
    inv = ROPE_THETA ** (-jnp.arange(half, dtype=jnp.float32) / half)
    ang = pos.astype(jnp.float32)[:, None] * inv[None, :]
    cos = jnp.cos(ang)[:, None, :]
    sin = jnp.sin(ang)[:, None, :]
    xf = x.astype(jnp.float32)
    x1, x2 = xf[..., :half], xf[..., half:]
    return jnp.concatenate([x1 * cos - x2 * sin, x2 * cos + x1 * sin], axis=-1).astype(x.dtype)


def softmax_with_sink(s, sinks, mask):
    sink = sinks.astype(jnp.float32).reshape(N_KV_HEADS, GROUP, 1, 1)
    s = jnp.where(mask, s, NEG)
    m = jnp.maximum(jnp.max(s, axis=-1, keepdims=True), sink)
    p = jnp.exp(s - m)
    return p / (jnp.sum(p, axis=-1, keepdims=True) + jnp.exp(sink - m))


def attn_prompt(q, k, v, sinks):
    B, S = q.shape[0], q.shape[1]
    NB = S // WINDOW
    qb = q.reshape(B, NB, WINDOW, N_KV_HEADS, GROUP, HEAD_DIM)
    kb = k.reshape(B, NB, WINDOW, N_KV_HEADS, HEAD_DIM)
    vb = v.reshape(B, NB, WINDOW, N_KV_HEADS, HEAD_DIM)
    pad = ((0, 0), (1, 0), (0, 0), (0, 0), (0, 0))
    kk = jnp.concatenate([jnp.pad(kb, pad)[:, :-1], kb], axis=2)
    vv = jnp.concatenate([jnp.pad(vb, pad)[:, :-1], vb], axis=2)
    s = jnp.einsum('bnqkgd,bnskd->bnkgqs', qb, kk,
                   preferred_element_type=jnp.float32) * (HEAD_DIM ** -0.5)
    qi = jnp.arange(WINDOW)[:, None] + WINDOW
    kj = jnp.arange(2 * WINDOW)[None, :]
    diff = qi - kj
    band = (diff >= 0) & (diff < WINDOW)
    blk = jnp.arange(NB)[:, None, None]
    valid = (blk > 0) | (kj[None] >= WINDOW)
    mask = (band[None] & valid)[None, :, None, None]
    p = softmax_with_sink(s, sinks, mask)
    o = jnp.einsum('bnkgqs,bnskd->bnqkgd', p.astype(v.dtype), vv)
    return o.reshape(B, S, N_HEADS * HEAD_DIM)


def attn_sample(q, k, v, k_buf, v_buf, sinks):
    B, DS = q.shape[0], q.shape[1]
    WB = k_buf.shape[1]
    kk = jnp.concatenate([k_buf, k], axis=1)
    vv = jnp.concatenate([v_buf, v], axis=1)
    qg = q.reshape(B, DS, N_KV_HEADS, GROUP, HEAD_DIM)
    s = jnp.einsum('bqkgd,bskd->bkgqs', qg, kk,
                   preferred_element_type=jnp.float32) * (HEAD_DIM ** -0.5)
    q_pos = PAST_LEN + jnp.arange(DS)
    k_pos = PAST_LEN - WB + jnp.arange(WB + DS)
    diff = q_pos[:, None] - k_pos[None, :]
    mask = (diff >= 0) & (diff < WINDOW)
    p = softmax_with_sink(s, sinks, mask)
    o = jnp.einsum('bkgqs,bskd->bqkgd', p.astype(v.dtype), vv)
    return o.reshape(B, DS, N_HEADS * HEAD_DIM), kk[:, -WB:], vv[:, -WB:]


def causal_dwconv(u, buf, w, b):
    full = jnp.concatenate([buf, u], axis=1)
    y = lax.conv_general_dilated(full, w[:, None, :], window_strides=(1,), padding='VALID',
                                 dimension_numbers=('NWC', 'WIO', 'NWC'),
                                 feature_group_count=CONV_CH)
    return y + b, full[:, -(CONV_WIDTH - 1):]


def layer(x, pos, k_buf, v_buf, c_buf, norm_mix_g, w_in, q_norm_g, k_norm_g, sinks,
          conv_w, conv_b, conv_ln_g, conv_ln_b, w_out, norm_mlp_g, w_up, w_down, is_prompt):
    B, S = x.shape[0], x.shape[1]
    h = rms_norm(x, norm_mix_g)
    z = h @ w_in
    q = z[..., :ATTN_WIDTH].reshape(B, S, N_HEADS, HEAD_DIM)
    k = z[..., ATTN_WIDTH:ATTN_WIDTH + KV_DIM].reshape(B, S, N_KV_HEADS, HEAD_DIM)
    v = z[..., ATTN_WIDTH + KV_DIM:ATTN_WIDTH + 2 * KV_DIM].reshape(B, S, N_KV_HEADS, HEAD_DIM)
    u_val = z[..., ATTN_WIDTH + 2 * KV_DIM:ATTN_WIDTH + 2 * KV_DIM + CONV_CH]
    u_gate = z[..., ATTN_WIDTH + 2 * KV_DIM + CONV_CH:]
    q = rope(rms_norm(q, q_norm_g), pos)
    k = rope(rms_norm(k, k_norm_g), pos)
    if is_prompt:
        a = attn_prompt(q, k, v, sinks)
        wb = min(WINDOW, S)
        new_k, new_v = k[:, -wb:], v[:, -wb:]
    else:
        a, new_k, new_v = attn_sample(q, k, v, k_buf, v_buf, sinks)
    u = u_val * jax.nn.sigmoid(u_gate)
    c, new_c = causal_dwconv(u, c_buf, conv_w, conv_b)
    c = jax.nn.silu(layer_norm(c, conv_ln_g, conv_ln_b))
    x = x + jnp.concatenate([a, c], axis=-1) @ w_out
    hm = rms_norm(x, norm_mlp_g)
    x = x + jnp.square(jax.nn.relu(hm @ w_up)) @ w_down
    return x, new_k, new_v, new_c


def setup_inputs(seed: int = 0) -> dict:
    key = jax.random.key(seed)
    ks = jax.random.split(key, 20)
    f32 = jnp.float32
    wb = min(WINDOW, PAST_LEN)
    nrm = lambda k, shp, s: jax.random.normal(k, shp, f32) * s
    return {
        "x_prompt": nrm(ks[0], (BATCH, SEQ, D_MODEL), 1.0),
        "x_sample": nrm(ks[1], (DEC_BATCH, DEC_SEQ, D_MODEL), 1.0),
        "cache_k": nrm(ks[2], (DEPTH, DEC_BATCH, wb, N_KV_HEADS, HEAD_DIM), 1.0),
        "cache_v": nrm(ks[3], (DEPTH, DEC_BATCH, wb, N_KV_HEADS, HEAD_DIM), 1.0),
        "state_conv": nrm(ks[4], (DEPTH, DEC_BATCH, CONV_WIDTH - 1, CONV_CH), 0.5),
        "norm_mix_g": 1.0 + nrm(ks[5], (DEPTH, D_MODEL), 0.02),
        "w_in": nrm(ks[6], (DEPTH, D_MODEL, IN_COLS), D_MODEL ** -0.5),
        "q_norm_g": 1.0 + nrm(ks[7], (DEPTH, HEAD_DIM), 0.02),
        "k_norm_g": 1.0 + nrm(ks[8], (DEPTH, HEAD_DIM), 0.02),
        "sinks": nrm(ks[9], (DEPTH, N_HEADS), 0.5),
        "conv_w": nrm(ks[10], (DEPTH, CONV_WIDTH, CONV_CH), CONV_WIDTH ** -0.5),
        "conv_b": nrm(ks[11], (DEPTH, CONV_CH), 0.02),
        "conv_ln_g": 1.0 + nrm(ks[12], (DEPTH, CONV_CH), 0.02),
        "conv_ln_b": nrm(ks[13], (DEPTH, CONV_CH), 0.02),
        "w_out": nrm(ks[14], (DEPTH, MIX_WIDTH, D_MODEL), MIX_WIDTH ** -0.5),
        "norm_mlp_g": 1.0 + nrm(ks[15], (DEPTH, D_MODEL), 0.02),
        "w_up": nrm(ks[16], (DEPTH, D_MODEL, D_FF), D_MODEL ** -0.5),
        "w_down": nrm(ks[17], (DEPTH, D_FF, D_MODEL), D_FF ** -0.5),
    }


def reference(x_prompt, x_sample, cache_k, cache_v, state_conv, norm_mix_g, w_in, q_norm_g,
              k_norm_g, sinks, conv_w, conv_b, conv_ln_g, conv_ln_b, w_out, norm_mlp_g,
              w_up, w_down):
    pos_p = jnp.arange(x_prompt.shape[1])
    pos_s = PAST_LEN + jnp.arange(x_sample.shape[1])
    zero_conv = jnp.zeros((x_prompt.shape[0], CONV_WIDTH - 1, CONV_CH), x_prompt.dtype)
    xp, xs = x_prompt, x_sample
    kp, vp, cp, ksm, vsm, csm = [], [], [], [], [], []
    for l in range(DEPTH):
        w = (norm_mix_g[l], w_in[l], q_norm_g[l], k_norm_g[l], sinks[l], conv_w[l], conv_b[l],
             conv_ln_g[l], conv_ln_b[l], w_out[l], norm_mlp_g[l], w_up[l], w_down[l])
        xp, nk, nv, nc = layer(xp, pos_p, None, None, zero_conv, *w, is_prompt=True)
        kp.append(nk); vp.append(nv); cp.append(nc)
        xs, nk, nv, nc = layer(xs, pos_s, cache_k[l], cache_v[l], state_conv[l], *w,
                               is_prompt=False)
        ksm.append(nk); vsm.append(nv); csm.append(nc)
    new_k_prompt = jnp.stack(kp)
    new_v_prompt = jnp.stack(vp)
    new_conv_prompt = jnp.stack(cp)
    new_k_sample = jnp.stack(ksm)
    new_v_sample = jnp.stack(vsm)
    new_conv_sample = jnp.stack(csm)
    return (xp, xs, new_k_prompt, new_v_prompt, new_conv_prompt, new_k_sample, new_v_sample, new_conv_sample)
```

```python
import functools

import jax
import jax.numpy as jnp
from jax import lax
from jax.experimental import pallas as pl
from jax.experimental.pallas import tpu as pltpu

D_MODEL = 2048
ATTN_WIDTH = 1024
CONV_CH = 1024
HEAD_DIM = 64
HALF = HEAD_DIM // 2
N_HEADS = 16
N_KV_HEADS = 4
GROUP = N_HEADS // N_KV_HEADS
KV_DIM = N_KV_HEADS * HEAD_DIM
WINDOW = 128
CONV_WIDTH = 31
CONV_STATE = CONV_WIDTH - 1
D_FF = 4 * D_MODEL
ROPE_THETA = 10000.0
EPS = 1e-6
IN_COLS = ATTN_WIDTH + 2 * KV_DIM + 2 * CONV_CH
NEG = -1e30
PAST_LEN = 16384

K_OFF = ATTN_WIDTH
V_OFF = ATTN_WIDTH + KV_DIM
VAL_OFF = ATTN_WIDTH + 2 * KV_DIM
GATE_OFF = VAL_OFF + CONV_CH

VMEM_LIMIT_BYTES = 56 * 1024 * 1024
SUBLANES = 8
CONV_PAD = 32

F32 = jnp.float32
BF16 = jnp.bfloat16


def _rms_rows(x, g):
    ms = jnp.mean(x * x, axis=-1, keepdims=True)
    return x * lax.rsqrt(ms + EPS) * g


def _norm_matmul_kernel(x_ref, g_ref, w_ref, o_ref, xn_ref):
    @pl.when(pl.program_id(1) == 0)
    def _():
        xn_ref[...] = _rms_rows(x_ref[...], g_ref[...]).astype(BF16)

    o_ref[...] = jnp.dot(xn_ref[...], w_ref[...].astype(BF16), preferred_element_type=F32)


def _norm_matmul(x, g, w, *, tm, tn):
    m, k = x.shape
    n = w.shape[1]
    return pl.pallas_call(
        _norm_matmul_kernel,
        grid=(m // tm, n // tn),
        in_specs=[
            pl.BlockSpec((tm, k), lambda i, j: (i, 0)),
            pl.BlockSpec((1, k), lambda i, j: (0, 0)),
            pl.BlockSpec((k, tn), lambda i, j: (0, j)),
        ],
        out_specs=pl.BlockSpec((tm, tn), lambda i, j: (i, j)),
        out_shape=jax.ShapeDtypeStruct((m, n), F32),
        scratch_shapes=[pltpu.VMEM((tm, k), BF16)],
        compiler_params=pltpu.CompilerParams(
            dimension_semantics=("arbitrary", "arbitrary"), vmem_limit_bytes=VMEM_LIMIT_BYTES),
        name="norm_matmul",
    )(x, g.reshape(1, k), w)


def _residual_matmul_kernel(a_ref, w_ref, x_ref, o_ref):
    o_ref[...] = x_ref[...] + jnp.dot(a_ref[...], w_ref[...].astype(BF16), preferred_element_type=F32)


def _residual_matmul(a, w, x, *, tm, tn):
    m, k = a.shape
    n = w.shape[1]
    return pl.pallas_call(
        _residual_matmul_kernel,
        grid=(m // tm, n // tn),
        in_specs=[
            pl.BlockSpec((tm, k), lambda i, j: (i, 0)),
            pl.BlockSpec((k, tn), lambda i, j: (0, j)),
            pl.BlockSpec((tm, tn), lambda i, j: (i, j)),
        ],
        out_specs=pl.BlockSpec((tm, tn), lambda i, j: (i, j)),
        out_shape=jax.ShapeDtypeStruct((m, n), F32),
        compiler_params=pltpu.CompilerParams(
            dimension_semantics=("arbitrary", "arbitrary"), vmem_limit_bytes=VMEM_LIMIT_BYTES),
        name="residual_matmul",
    )(a, w, x)


def _mlp_kernel(x_ref, g_ref, wu_ref, wd_ref, o_ref, hm_ref):
    @pl.when(pl.program_id(1) == 0)
    def _():
        x = x_ref[...]
        hm_ref[...] = _rms_rows(x, g_ref[...]).astype(BF16)
        o_ref[...] = x

    h = jnp.dot(hm_ref[...], wu_ref[...].astype(BF16), preferred_element_type=F32)
    h = jnp.square(jnp.maximum(h, 0.0)).astype(BF16)
    o_ref[...] += jnp.dot(h, wd_ref[...].astype(BF16), preferred_element_type=F32)


def _mlp(x, g, wu, wd, *, tm, tf):
    m, d = x.shape
    f = wu.shape[1]
    return pl.pallas_call(
        _mlp_kernel,
        grid=(m // tm, f // tf),
        in_specs=[
            pl.BlockSpec((tm, d), lambda i, j: (i, 0), pipeline_mode=pl.Buffered(1)),
            pl.BlockSpec((1, d), lambda i, j: (0, 0)),
            pl.BlockSpec((d, tf), lambda i, j: (0, j)),
            pl.BlockSpec((tf, d), lambda i, j: (j, 0)),
        ],
        out_specs=pl.BlockSpec((tm, d), lambda i, j: (i, 0)),
        out_shape=jax.ShapeDtypeStruct((m, d), F32),
        scratch_shapes=[pltpu.VMEM((tm, d), BF16)],
        compiler_params=pltpu.CompilerParams(
            dimension_semantics=("arbitrary", "arbitrary"), vmem_limit_bytes=VMEM_LIMIT_BYTES),
        name="mlp",
    )(x, g.reshape(1, d), wu, wd)


def _rotate_half(x):
    return jnp.concatenate([x[:, HALF:], x[:, :HALF]], axis=-1)


def _norm_rope(x, g, cos2, sin2):
    xn = _rms_rows(x, g)
    return xn * cos2 + _rotate_half(xn) * sin2


def _layernorm_swish(c, g, b):
    mu = jnp.mean(c, axis=-1, keepdims=True)
    xc = c - mu
    y = xc * lax.rsqrt(jnp.mean(xc * xc, axis=-1, keepdims=True) + EPS)
    y = y * g + b
    return y * jax.nn.sigmoid(y)


def _prompt_mixer_kernel(sinks_ref, z_ref, cos_ref, sin_ref, qg_ref, kg_ref, cw_ref, cb_ref,
                         lg_ref, lb_ref, mix_ref, nk_ref, nv_ref, nc_ref,
                         kk_ref, vv_ref, ubuf_ref, a_ref, y_ref):
    n = pl.program_id(1)
    nb = pl.num_programs(1)

    @pl.when(n == 0)
    def _():
        kk_ref[0:WINDOW, :] = jnp.zeros((WINDOW, KV_DIM), BF16)
        vv_ref[0:WINDOW, :] = jnp.zeros((WINDOW, KV_DIM), BF16)
        ubuf_ref[0:CONV_PAD, :] = jnp.zeros((CONV_PAD, CONV_CH), F32)

    @pl.when(n > 0)
    def _():
        kk_ref[0:WINDOW, :] = kk_ref[WINDOW:2 * WINDOW, :]
        vv_ref[0:WINDOW, :] = vv_ref[WINDOW:2 * WINDOW, :]
        ubuf_ref[0:CONV_PAD, :] = ubuf_ref[WINDOW:WINDOW + CONV_PAD, :]

    cos2 = cos_ref[...]
    sin2 = sin_ref[...]
    qg = qg_ref[...]
    kg = kg_ref[...]

    for h in range(N_KV_HEADS):
        lo = K_OFF + h * HEAD_DIM
        kh = _norm_rope(z_ref[:, lo:lo + HEAD_DIM], kg, cos2, sin2)
        nk_ref[0, :, h * HEAD_DIM:(h + 1) * HEAD_DIM] = kh
        kk_ref[WINDOW:2 * WINDOW, h * HEAD_DIM:(h + 1) * HEAD_DIM] = kh.astype(BF16)
    v_cur = z_ref[:, V_OFF:V_OFF + KV_DIM]
    nv_ref[0] = v_cur
    vv_ref[WINDOW:2 * WINDOW, :] = v_cur.astype(BF16)

    rows = lax.broadcasted_iota(jnp.int32, (GROUP * WINDOW, 2 * WINDOW), 0) & (WINDOW - 1)
    cols = lax.broadcasted_iota(jnp.int32, (GROUP * WINDOW, 2 * WINDOW), 1)
    diff = rows + WINDOW - cols
    mask = (diff >= 0) & (diff <= jnp.where(n > 0, WINDOW - 1, rows))

    for h in range(N_KV_HEADS):
        qs = []
        sink_rows = []
        for g in range(GROUP):
            hh = h * GROUP + g
            qh = _norm_rope(z_ref[:, hh * HEAD_DIM:(hh + 1) * HEAD_DIM], qg, cos2, sin2)
            qs.append(qh.astype(BF16))
            sink_rows.append(jnp.full((WINDOW, 1), sinks_ref[hh], F32))
        q = jnp.concatenate(qs, axis=0)
        sink = jnp.concatenate(sink_rows, axis=0)
        k = kk_ref[:, h * HEAD_DIM:(h + 1) * HEAD_DIM]
        v = vv_ref[:, h * HEAD_DIM:(h + 1) * HEAD_DIM]
        s = lax.dot_general(q, k, (((1,), (1,)), ((), ())), preferred_element_type=F32)
        s = jnp.where(mask, s * (HEAD_DIM ** -0.5), NEG)
        m = jnp.maximum(jnp.max(s, axis=-1, keepdims=True), sink)
        p = jnp.exp(s - m)
        den = jnp.sum(p, axis=-1, keepdims=True) + jnp.exp(sink - m)
        p = p / den
        o = jnp.dot(p.astype(BF16), v, preferred_element_type=F32)
        for g in range(GROUP):
            hh = h * GROUP + g
            a_ref[:, hh * HEAD_DIM:(hh + 1) * HEAD_DIM] = o[g * WINDOW:(g + 1) * WINDOW]
    mix_ref[:, 0:ATTN_WIDTH] = a_ref[...].astype(BF16)

    u = z_ref[:, VAL_OFF:VAL_OFF + CONV_CH] * jax.nn.sigmoid(z_ref[:, GATE_OFF:GATE_OFF + CONV_CH])
    ubuf_ref[CONV_PAD:CONV_PAD + WINDOW, :] = u

    @pl.when(n == nb - 1)
    def _():
        nc_ref[0] = ubuf_ref[CONV_PAD + WINDOW - CONV_STATE:CONV_PAD + WINDOW, :]

    lanes = 128

    def conv_chunk(c, carry):
        c0 = pl.multiple_of(c * lanes, lanes)
        acc = jnp.zeros((WINDOW, lanes), F32)
        for j in range(CONV_WIDTH):
            r0 = CONV_PAD - CONV_STATE + j
            acc = acc + cw_ref[j:j + 1, pl.ds(c0, lanes)] * ubuf_ref[r0:r0 + WINDOW, pl.ds(c0, lanes)]
        y_ref[:, pl.ds(c0, lanes)] = acc + cb_ref[:, pl.ds(c0, lanes)]
        return carry

    lax.fori_loop(0, CONV_CH // lanes, conv_chunk, 0)
    mix_ref[:, ATTN_WIDTH:] = _layernorm_swish(y_ref[...], lg_ref[...], lb_ref[...]).astype(BF16)


def _prompt_mixer(z, batch, seq, cos2, sin2, qg, kg, sinks, cw, cb, lg, lb):
    nb = seq // WINDOW
    row = lambda b, n: (b * nb + n, 0)
    const = lambda b, n: (0, 0)
    per_batch = lambda b, n: (b, 0, 0)
    return pl.pallas_call(
        _prompt_mixer_kernel,
        grid=(batch, nb),
        in_specs=[
            pl.BlockSpec(memory_space=pltpu.SMEM),
            pl.BlockSpec((WINDOW, IN_COLS), row),
            pl.BlockSpec((WINDOW, HEAD_DIM), lambda b, n: (n, 0)),
            pl.BlockSpec((WINDOW, HEAD_DIM), lambda b, n: (n, 0)),
            pl.BlockSpec((1, HEAD_DIM), const),
            pl.BlockSpec((1, HEAD_DIM), const),
            pl.BlockSpec((CONV_WIDTH, CONV_CH), const),
            pl.BlockSpec((1, CONV_CH), const),
            pl.BlockSpec((1, CONV_CH), const),
            pl.BlockSpec((1, CONV_CH), const),
        ],
        out_specs=[
            pl.BlockSpec((WINDOW, ATTN_WIDTH + CONV_CH), row),
            pl.BlockSpec((1, WINDOW, KV_DIM), per_batch),
            pl.BlockSpec((1, WINDOW, KV_DIM), per_batch),
            pl.BlockSpec((1, CONV_STATE, CONV_CH), per_batch),
        ],
        out_shape=[
            jax.ShapeDtypeStruct((batch * seq, ATTN_WIDTH + CONV_CH), BF16),
            jax.ShapeDtypeStruct((batch, WINDOW, KV_DIM), F32),
            jax.ShapeDtypeStruct((batch, WINDOW, KV_DIM), F32),
            jax.ShapeDtypeStruct((batch, CONV_STATE, CONV_CH), F32),
        ],
        scratch_shapes=[
            pltpu.VMEM((2 * WINDOW, KV_DIM), BF16),
            pltpu.VMEM((2 * WINDOW, KV_DIM), BF16),
            pltpu.VMEM((CONV_PAD + WINDOW, CONV_CH), F32),
            pltpu.VMEM((WINDOW, ATTN_WIDTH), F32),
            pltpu.VMEM((WINDOW, CONV_CH), F32),
        ],
        compiler_params=pltpu.CompilerParams(
            dimension_semantics=("arbitrary", "arbitrary"), vmem_limit_bytes=VMEM_LIMIT_BYTES),
        name="prompt_mixer",
    )(sinks, z, cos2, sin2, qg, kg, cw, cb, lg, lb)


def _sample_mixer_kernel(sinks_ref, q_ref, k_ref, z_ref, ck_ref, cv_ref, st_ref, cos_ref, sin_ref,
                         qg_ref, kg_ref, cw_ref, cb_ref, lg_ref, lb_ref,
                         a_ref, c_ref, nk_ref, nv_ref, nc_ref, knew_ref):
    bs = z_ref.shape[0]
    cos2 = cos_ref[...]
    sin2 = sin_ref[...]

    qn = _norm_rope(q_ref[...], qg_ref[...], cos2, sin2).astype(BF16)
    tile_r = lax.broadcasted_iota(jnp.int32, (HEAD_DIM, KV_DIM), 0)
    tile_c = lax.broadcasted_iota(jnp.int32, (HEAD_DIM, KV_DIM), 1)
    tile = jnp.where((tile_c & (HEAD_DIM - 1)) == tile_r, 1.0, 0.0).astype(BF16)
    qt = jnp.dot(qn, tile, preferred_element_type=F32)
    q_row = lax.broadcasted_iota(jnp.int32, (bs * N_HEADS, KV_DIM), 0)
    q_col = lax.broadcasted_iota(jnp.int32, (bs * N_HEADS, KV_DIM), 1)
    own = ((q_row & (N_HEADS - 1)) // GROUP) == (q_col // HEAD_DIM)
    qexp = jnp.where(own, qt, 0.0)

    kn = _norm_rope(k_ref[...], kg_ref[...], cos2, sin2)
    for b in range(bs):
        for h in range(N_KV_HEADS):
            r = b * N_KV_HEADS + h
            knew_ref[b:b + 1, h * HEAD_DIM:(h + 1) * HEAD_DIM] = kn[r:r + 1, :]
    knew = knew_ref[...]
    vnew = z_ref[:, V_OFF:V_OFF + KV_DIM]
    knew_r = knew.astype(BF16).astype(F32)
    vnew_r = vnew.astype(BF16).astype(F32)

    sink_col = jnp.concatenate([jnp.full((1, 1), sinks_ref[hh], F32) for hh in range(N_HEADS)], axis=0)
    key_idx = lax.broadcasted_iota(jnp.int32, (N_HEADS, WINDOW), 1)
    key_ok = key_idx >= 1
    o_row = lax.broadcasted_iota(jnp.int32, (N_HEADS, KV_DIM), 0)
    o_col = lax.broadcasted_iota(jnp.int32, (N_HEADS, KV_DIM), 1)
    o_own = (o_row // GROUP) == (o_col // HEAD_DIM)
    fold_r = lax.broadcasted_iota(jnp.int32, (KV_DIM, HEAD_DIM), 0)
    fold_c = lax.broadcasted_iota(jnp.int32, (KV_DIM, HEAD_DIM), 1)
    fold = jnp.where((fold_r & (HEAD_DIM - 1)) == fold_c, 1.0, 0.0).astype(BF16)

    for b in range(bs):
        qb = qexp[b * N_HEADS:(b + 1) * N_HEADS]
        kc = ck_ref[b]
        vc = cv_ref[b]
        s = lax.dot_general(qb.astype(BF16), kc.astype(BF16), (((1,), (1,)), ((), ())),
                            preferred_element_type=F32)
        s = jnp.where(key_ok, s * (HEAD_DIM ** -0.5), NEG)
        s_new = jnp.sum(qb * knew_r[b:b + 1, :], axis=-1, keepdims=True) * (HEAD_DIM ** -0.5)
        m = jnp.maximum(jnp.maximum(jnp.max(s, axis=-1, keepdims=True), s_new), sink_col)
        p = jnp.exp(s - m)
        p_new = jnp.exp(s_new - m)
        den = jnp.sum(p, axis=-1, keepdims=True) + p_new + jnp.exp(sink_col - m)
        p = p / den
        p_new = (p_new / den).astype(BF16).astype(F32)
        o = jnp.dot(p.astype(BF16), vc.astype(BF16), preferred_element_type=F32)
        o = o + p_new * vnew_r[b:b + 1, :]
        o = jnp.where(o_own, o, 0.0).astype(BF16)
        a_ref[b * N_HEADS:(b + 1) * N_HEADS, :] = jnp.dot(o, fold, preferred_element_type=F32)

        nk_ref[b, 0:WINDOW - 1, :] = ck_ref[b, 1:WINDOW, :]
        nk_ref[b, WINDOW - 1:WINDOW, :] = knew[b:b + 1, :]
        nv_ref[b, 0:WINDOW - 1, :] = cv_ref[b, 1:WINDOW, :]
        nv_ref[b, WINDOW - 1:WINDOW, :] = vnew[b:b + 1, :]

    u = z_ref[:, VAL_OFF:VAL_OFF + CONV_CH] * jax.nn.sigmoid(z_ref[:, GATE_OFF:GATE_OFF + CONV_CH])
    w_hist = cw_ref[0:CONV_STATE, :]
    w_last = cw_ref[CONV_STATE:CONV_WIDTH, :]
    for b in range(bs):
        st = st_ref[b]
        yb = jnp.sum(st * w_hist, axis=0, keepdims=True) + u[b:b + 1, :] * w_last + cb_ref[...]
        c_ref[b:b + 1, :] = _layernorm_swish(yb, lg_ref[...], lb_ref[...])
        nc_ref[b, 0:CONV_STATE - 1, :] = st_ref[b, 1:CONV_STATE, :]
        nc_ref[b, CONV_STATE - 1:CONV_STATE, :] = u[b:b + 1, :]


def _sample_mixer(zs, cache_k, cache_v, state, cos2, sin2, qg, kg, sinks, cw, cb, lg, lb, *, bs):
    nsamp = zs.shape[0]
    q_rows = zs[:, :ATTN_WIDTH].reshape(nsamp * N_HEADS, HEAD_DIM)
    k_rows = zs[:, K_OFF:K_OFF + KV_DIM].reshape(nsamp * N_KV_HEADS, HEAD_DIM)
    const = lambda i: (0, 0)
    step = lambda i: (i, 0)
    step3 = lambda i: (i, 0, 0)
    return pl.pallas_call(
        _sample_mixer_kernel,
        grid=(nsamp // bs,),
        in_specs=[
            pl.BlockSpec(memory_space=pltpu.SMEM),
            pl.BlockSpec((bs * N_HEADS, HEAD_DIM), step),
            pl.BlockSpec((bs * N_KV_HEADS, HEAD_DIM), step),
            pl.BlockSpec((bs, IN_COLS), step),
            pl.BlockSpec((bs, WINDOW, KV_DIM), step3),
            pl.BlockSpec((bs, WINDOW, KV_DIM), step3),
            pl.BlockSpec((bs, CONV_STATE, CONV_CH), step3),
            pl.BlockSpec((1, HEAD_DIM), const),
            pl.BlockSpec((1, HEAD_DIM), const),
            pl.BlockSpec((1, HEAD_DIM), const),
            pl.BlockSpec((1, HEAD_DIM), const),
            pl.BlockSpec((CONV_WIDTH, CONV_CH), const),
            pl.BlockSpec((1, CONV_CH), const),
            pl.BlockSpec((1, CONV_CH), const),
            pl.BlockSpec((1, CONV_CH), const),
        ],
        out_specs=[
            pl.BlockSpec((bs * N_HEADS, HEAD_DIM), step),
            pl.BlockSpec((bs, CONV_CH), step),
            pl.BlockSpec((bs, WINDOW, KV_DIM), step3),
            pl.BlockSpec((bs, WINDOW, KV_DIM), step3),
            pl.BlockSpec((bs, CONV_STATE, CONV_CH), step3),
        ],
        out_shape=[
            jax.ShapeDtypeStruct((nsamp * N_HEADS, HEAD_DIM), F32),
            jax.ShapeDtypeStruct((nsamp, CONV_CH), F32),
            jax.ShapeDtypeStruct((nsamp, WINDOW, KV_DIM), F32),
            jax.ShapeDtypeStruct((nsamp, WINDOW, KV_DIM), F32),
            jax.ShapeDtypeStruct((nsamp, CONV_STATE, CONV_CH), F32),
        ],
        scratch_shapes=[pltpu.VMEM((bs, KV_DIM), F32)],
        compiler_params=pltpu.CompilerParams(
            dimension_semantics=("arbitrary",), vmem_limit_bytes=VMEM_LIMIT_BYTES),
        name="sample_mixer",
    )(sinks, q_rows, k_rows, zs, cache_k, cache_v, state, cos2, sin2, qg, kg, cw, cb, lg, lb)


def _rope_tables(pos):
    inv = ROPE_THETA ** (-jnp.arange(HALF, dtype=F32) / HALF)
    ang = pos.astype(F32)[:, None] * inv[None, :]
    cos = jnp.cos(ang)
    sin = jnp.sin(ang)
    return jnp.concatenate([cos, cos], axis=-1), jnp.concatenate([-sin, sin], axis=-1)


def kernel(x_prompt, x_sample, cache_k, cache_v, state_conv, norm_mix_g, w_in, q_norm_g, k_norm_g,
           sinks, conv_w, conv_b, conv_ln_g, conv_ln_b, w_out, norm_mlp_g, w_up, w_down):
    batch, seq, d = x_prompt.shape
    nsamp, dec_seq, _ = x_sample.shape
    depth = w_in.shape[0]
    assert dec_seq == 1 and seq % WINDOW == 0 and cache_k.shape[2] == WINDOW

    cos_p, sin_p = _rope_tables(jnp.arange(seq))
    cos_s, sin_s = _rope_tables(PAST_LEN + jnp.arange(dec_seq))

    xp = x_prompt.reshape(batch * seq, d)
    xs = x_sample.reshape(nsamp * dec_seq, d)
    kp, vp, cp, ksm, vsm, csm = [], [], [], [], [], []
    for l in range(depth):
        qg = q_norm_g[l].reshape(1, HEAD_DIM)
        kg = k_norm_g[l].reshape(1, HEAD_DIM)
        cb = conv_b[l].reshape(1, CONV_CH)
        lg = conv_ln_g[l].reshape(1, CONV_CH)
        lb = conv_ln_b[l].reshape(1, CONV_CH)

        zp = _norm_matmul(xp, norm_mix_g[l], w_in[l], tm=1024, tn=512)
        mixp, nk, nv, nc = _prompt_mixer(zp, batch, seq, cos_p, sin_p, qg, kg, sinks[l],
                                         conv_w[l], cb, lg, lb)
        x1 = _residual_matmul(mixp, w_out[l], xp, tm=1024, tn=512)
        xp = _mlp(x1, norm_mlp_g[l], w_up[l], w_down[l], tm=1024, tf=256)
        kp.append(nk.reshape(batch, WINDOW, N_KV_HEADS, HEAD_DIM))
        vp.append(nv.reshape(batch, WINDOW, N_KV_HEADS, HEAD_DIM))
        cp.append(nc)

        zs = _norm_matmul(xs, norm_mix_g[l], w_in[l], tm=nsamp, tn=512)
        a_s, c_s, nk, nv, nc = _sample_mixer(
            zs, cache_k[l].reshape(nsamp, WINDOW, KV_DIM), cache_v[l].reshape(nsamp, WINDOW, KV_DIM),
            state_conv[l], cos_s, sin_s, qg, kg, sinks[l], conv_w[l], cb, lg, lb, bs=16)
        mixs = jnp.concatenate([a_s.reshape(nsamp, ATTN_WIDTH), c_s], axis=-1).astype(BF16)
        x1s = _residual_matmul(mixs, w_out[l], xs, tm=nsamp, tn=512)
        xs = _mlp(x1s, norm_mlp_g[l], w_up[l], w_down[l], tm=nsamp, tf=512)
        ksm.append(nk.reshape(nsamp, WINDOW, N_KV_HEADS, HEAD_DIM))
        vsm.append(nv.reshape(nsamp, WINDOW, N_KV_HEADS, HEAD_DIM))
        csm.append(nc)

    return (xp.reshape(batch, seq, d), xs.reshape(nsamp, dec_seq, d),
            jnp.stack(kp), jnp.stack(vp), jnp.stack(cp),
            jnp.stack(ksm), jnp.stack(vsm), jnp.stack(csm))
```

```python
import functools

import jax
import jax.numpy as jnp
from jax import lax
from jax.experimental import pallas as pl
from jax.experimental.pallas import tpu as pltpu

D_MODEL = 2048
ATTN_WIDTH = 1024
CONV_CH = 1024
HEAD_DIM = 64
HALF = HEAD_DIM // 2
N_HEADS = 16
N_KV_HEADS = 4
GROUP = N_HEADS // N_KV_HEADS
KV_DIM = N_KV_HEADS * HEAD_DIM
WINDOW = 128
CONV_WIDTH = 31
CONV_STATE = CONV_WIDTH - 1
D_FF = 4 * D_MODEL
ROPE_THETA = 10000.0
EPS = 1e-6
IN_COLS = ATTN_WIDTH + 2 * KV_DIM + 2 * CONV_CH
NEG = -1e30
PAST_LEN = 16384

K_OFF = ATTN_WIDTH
V_OFF = ATTN_WIDTH + KV_DIM
VAL_OFF = ATTN_WIDTH + 2 * KV_DIM
GATE_OFF = VAL_OFF + CONV_CH

VMEM_LIMIT_BYTES = 56 * 1024 * 1024
SUBLANES = 8
CONV_PAD = 32

F32 = jnp.float32
BF16 = jnp.bfloat16


def _rms_rows(x, g):
    ms = jnp.mean(x * x, axis=-1, keepdims=True)
    return x * lax.rsqrt(ms + EPS) * g


def _norm_matmul_kernel(x_ref, g_ref, w_ref, o_ref, xn_ref):
    @pl.when(pl.program_id(1) == 0)
    def _():
        xn_ref[...] = _rms_rows(x_ref[...], g_ref[...]).astype(BF16)

    o_ref[...] = jnp.dot(xn_ref[...], w_ref[...].astype(BF16), preferred_element_type=F32)


def _norm_matmul(x, g, w, *, tm, tn):
    m, k = x.shape
    n = w.shape[1]
    return pl.pallas_call(
        _norm_matmul_kernel,
        grid=(m // tm, n // tn),
        in_specs=[
            pl.BlockSpec((tm, k), lambda i, j: (i, 0)),
            pl.BlockSpec((1, k), lambda i, j: (0, 0)),
            pl.BlockSpec((k, tn), lambda i, j: (0, j)),
        ],
        out_specs=pl.BlockSpec((tm, tn), lambda i, j: (i, j)),
        out_shape=jax.ShapeDtypeStruct((m, n), F32),
        scratch_shapes=[pltpu.VMEM((tm, k), BF16)],
        compiler_params=pltpu.CompilerParams(
            dimension_semantics=("arbitrary", "arbitrary"), vmem_limit_bytes=VMEM_LIMIT_BYTES),
        name="norm_matmul",
    )(x, g.reshape(1, k), w)


def _residual_matmul_kernel(a_ref, w_ref, x_ref, o_ref):
    o_ref[...] = x_ref[...] + jnp.dot(a_ref[...], w_ref[...].astype(BF16), preferred_element_type=F32)


def _residual_matmul(a, w, x, *, tm, tn):
    m, k = a.shape
    n = w.shape[1]
    return pl.pallas_call(
        _residual_matmul_kernel,
        grid=(m // tm, n // tn),
        in_specs=[
            pl.BlockSpec((tm, k), lambda i, j: (i, 0)),
            pl.BlockSpec((k, tn), lambda i, j: (0, j)),
            pl.BlockSpec((tm, tn), lambda i, j: (i, j)),
        ],
        out_specs=pl.BlockSpec((tm, tn), lambda i, j: (i, j)),
        out_shape=jax.ShapeDtypeStruct((m, n), F32),
        compiler_params=pltpu.CompilerParams(
            dimension_semantics=("arbitrary", "arbitrary"), vmem_limit_bytes=VMEM_LIMIT_BYTES),
        name="residual_matmul",
    )(a, w, x)


def _mlp_kernel(x_ref, g_ref, wu_ref, wd_ref, o_ref, hm_ref):
    @pl.when(pl.program_id(1) == 0)
    def _():
        x = x_ref[...]
        hm_ref[...] = _rms_rows(x, g_ref[...]).astype(BF16)
        o_ref[...] = x

    h = jnp.dot(hm_ref[...], wu_ref[...].astype(BF16), preferred_element_type=F32)
    h = jnp.square(jnp.maximum(h, 0.0)).astype(BF16)
    o_ref[...] += jnp.dot(h, wd_ref[...].astype(BF16), preferred_element_type=F32)


def _mlp(x, g, wu, wd, *, tm, tf):
    m, d = x.shape
    f = wu.shape[1]
    return pl.pallas_call(
        _mlp_kernel,
        grid=(m // tm, f // tf),
        in_specs=[
            pl.BlockSpec((tm, d), lambda i, j: (i, 0), pipeline_mode=pl.Buffered(1)),
            pl.BlockSpec((1, d), lambda i, j: (0, 0)),
            pl.BlockSpec((d, tf), lambda i, j: (0, j)),
            pl.BlockSpec((tf, d), lambda i, j: (j, 0)),
        ],
        out_specs=pl.BlockSpec((tm, d), lambda i, j: (i, 0)),
        out_shape=jax.ShapeDtypeStruct((m, d), F32),
        scratch_shapes=[pltpu.VMEM((tm, d), BF16)],
        compiler_params=pltpu.CompilerParams(
            dimension_semantics=("arbitrary", "arbitrary"), vmem_limit_bytes=VMEM_LIMIT_BYTES),
        name="mlp",
    )(x, g.reshape(1, d), wu, wd)


def _rotate_half(x):
    return jnp.concatenate([x[:, HALF:], x[:, :HALF]], axis=-1)


def _norm_rope(x, g, cos2, sin2):
    xn = _rms_rows(x, g)
    return xn * cos2 + _rotate_half(xn) * sin2


def _layernorm_swish(c, g, b):
    mu = jnp.mean(c, axis=-1, keepdims=True)
    xc = c - mu
    y = xc * lax.rsqrt(jnp.mean(xc * xc, axis=-1, keepdims=True) + EPS)
    y = y * g + b
    return y * jax.nn.sigmoid(y)


def _prompt_mixer_kernel(sinks_ref, z_ref, cos_ref, sin_ref, qg_ref, kg_ref, cw_ref, cb_ref,
                         lg_ref, lb_ref, mix_ref, nk_ref, nv_ref, nc_ref,
                         kk_ref, vv_ref, ubuf_ref, a_ref, y_ref):
    n = pl.program_id(1)
    nb = pl.num_programs(1)

    @pl.when(n == 0)
    def _():
        kk_ref[0:WINDOW, :] = jnp.zeros((WINDOW, KV_DIM), BF16)
        vv_ref[0:WINDOW, :] = jnp.zeros((WINDOW, KV_DIM), BF16)
        ubuf_ref[0:CONV_PAD, :] = jnp.zeros((CONV_PAD, CONV_CH), F32)

    @pl.when(n > 0)
    def _():
        kk_ref[0:WINDOW, :] = kk_ref[WINDOW:2 * WINDOW, :]
        vv_ref[0:WINDOW, :] = vv_ref[WINDOW:2 * WINDOW, :]
        ubuf_ref[0:CONV_PAD, :] = ubuf_ref[WINDOW:WINDOW + CONV_PAD, :]

    cos2 = cos_ref[...]
    sin2 = sin_ref[...]
    qg = qg_ref[...]
    kg = kg_ref[...]

    for h in range(N_KV_HEADS):
        lo = K_OFF + h * HEAD_DIM
        kh = _norm_rope(z_ref[:, lo:lo + HEAD_DIM], kg, cos2, sin2)
        nk_ref[0, :, h * HEAD_DIM:(h + 1) * HEAD_DIM] = kh
        kk_ref[WINDOW:2 * WINDOW, h * HEAD_DIM:(h + 1) * HEAD_DIM] = kh.astype(BF16)
    v_cur = z_ref[:, V_OFF:V_OFF + KV_DIM]
    nv_ref[0] = v_cur
    vv_ref[WINDOW:2 * WINDOW, :] = v_cur.astype(BF16)

    rows = lax.broadcasted_iota(jnp.int32, (GROUP * WINDOW, 2 * WINDOW), 0) & (WINDOW - 1)
    cols = lax.broadcasted_iota(jnp.int32, (GROUP * WINDOW, 2 * WINDOW), 1)
    diff = rows + WINDOW - cols
    mask = (diff >= 0) & (diff <= jnp.where(n > 0, WINDOW - 1, rows))

    for h in range(N_KV_HEADS):
        qs = []
        sink_rows = []
        for g in range(GROUP):
            hh = h * GROUP + g
            qh = _norm_rope(z_ref[:, hh * HEAD_DIM:(hh + 1) * HEAD_DIM], qg, cos2, sin2)
            qs.append(qh.astype(BF16))
            sink_rows.append(jnp.full((WINDOW, 1), sinks_ref[hh], F32))
        q = jnp.concatenate(qs, axis=0)
        sink = jnp.concatenate(sink_rows, axis=0)
        k = kk_ref[:, h * HEAD_DIM:(h + 1) * HEAD_DIM]
        v = vv_ref[:, h * HEAD_DIM:(h + 1) * HEAD_DIM]
        s = lax.dot_general(q, k, (((1,), (1,)), ((), ())), preferred_element_type=F32)
        s = jnp.where(mask, s * (HEAD_DIM ** -0.5), NEG)
        m = jnp.maximum(jnp.max(s, axis=-1, keepdims=True), sink)
        p = jnp.exp(s - m)
        den = jnp.sum(p, axis=-1, keepdims=True) + jnp.exp(sink - m)
        p = p * (1.0 / den)
        o = jnp.dot(p.astype(BF16), v, preferred_element_type=F32)
        for g in range(GROUP):
            hh = h * GROUP + g
            a_ref[:, hh * HEAD_DIM:(hh + 1) * HEAD_DIM] = o[g * WINDOW:(g + 1) * WINDOW]
    mix_ref[:, 0:ATTN_WIDTH] = a_ref[...].astype(BF16)

    u = z_ref[:, VAL_OFF:VAL_OFF + CONV_CH] * jax.nn.sigmoid(z_ref[:, GATE_OFF:GATE_OFF + CONV_CH])
    ubuf_ref[CONV_PAD:CONV_PAD + WINDOW, :] = u

    @pl.when(n == nb - 1)
    def _():
        nc_ref[0] = ubuf_ref[CONV_PAD + WINDOW - CONV_STATE:CONV_PAD + WINDOW, :]

    lanes = 128

    def conv_chunk(c, carry):
        c0 = pl.multiple_of(c * lanes, lanes)
        acc = jnp.zeros((WINDOW, lanes), F32)
        for j in range(CONV_WIDTH):
            r0 = CONV_PAD - CONV_STATE + j
            acc = acc + cw_ref[j:j + 1, pl.ds(c0, lanes)] * ubuf_ref[r0:r0 + WINDOW, pl.ds(c0, lanes)]
        y_ref[:, pl.ds(c0, lanes)] = acc + cb_ref[:, pl.ds(c0, lanes)]
        return carry

    lax.fori_loop(0, CONV_CH // lanes, conv_chunk, 0)
    mix_ref[:, ATTN_WIDTH:] = _layernorm_swish(y_ref[...], lg_ref[...], lb_ref[...]).astype(BF16)


def _prompt_mixer(z, batch, seq, cos2, sin2, qg, kg, sinks, cw, cb, lg, lb):
    nb = seq // WINDOW
    row = lambda b, n: (b * nb + n, 0)
    const = lambda b, n: (0, 0)
    per_batch = lambda b, n: (b, 0, 0)
    return pl.pallas_call(
        _prompt_mixer_kernel,
        grid=(batch, nb),
        in_specs=[
            pl.BlockSpec(memory_space=pltpu.SMEM),
            pl.BlockSpec((WINDOW, IN_COLS), row),
            pl.BlockSpec((WINDOW, HEAD_DIM), lambda b, n: (n, 0)),
            pl.BlockSpec((WINDOW, HEAD_DIM), lambda b, n: (n, 0)),
            pl.BlockSpec((1, HEAD_DIM), const),
            pl.BlockSpec((1, HEAD_DIM), const),
            pl.BlockSpec((CONV_WIDTH, CONV_CH), const),
            pl.BlockSpec((1, CONV_CH), const),
            pl.BlockSpec((1, CONV_CH), const),
            pl.BlockSpec((1, CONV_CH), const),
        ],
        out_specs=[
            pl.BlockSpec((WINDOW, ATTN_WIDTH + CONV_CH), row),
            pl.BlockSpec((1, WINDOW, KV_DIM), per_batch),
            pl.BlockSpec((1, WINDOW, KV_DIM), per_batch),
            pl.BlockSpec((1, CONV_STATE, CONV_CH), per_batch),
        ],
        out_shape=[
            jax.ShapeDtypeStruct((batch * seq, ATTN_WIDTH + CONV_CH), BF16),
            jax.ShapeDtypeStruct((batch, WINDOW, KV_DIM), F32),
            jax.ShapeDtypeStruct((batch, WINDOW, KV_DIM), F32),
            jax.ShapeDtypeStruct((batch, CONV_STATE, CONV_CH), F32),
        ],
        scratch_shapes=[
            pltpu.VMEM((2 * WINDOW, KV_DIM), BF16),
            pltpu.VMEM((2 * WINDOW, KV_DIM), BF16),
            pltpu.VMEM((CONV_PAD + WINDOW, CONV_CH), F32),
            pltpu.VMEM((WINDOW, ATTN_WIDTH), F32),
            pltpu.VMEM((WINDOW, CONV_CH), F32),
        ],
        compiler_params=pltpu.CompilerParams(
            dimension_semantics=("arbitrary", "arbitrary"), vmem_limit_bytes=VMEM_LIMIT_BYTES),
        name="prompt_mixer",
    )(sinks, z, cos2, sin2, qg, kg, cw, cb, lg, lb)


def _sample_mixer_kernel(sinks_ref, q_ref, k_ref, z_ref, ck_ref, cv_ref, st_ref, cos_ref, sin_ref,
                         qg_ref, kg_ref, cw_ref, cb_ref, lg_ref, lb_ref,
                         a_ref, c_ref, nk_ref, nv_ref, nc_ref, knew_ref):
    bs = z_ref.shape[0]
    cos2 = cos_ref[...]
    sin2 = sin_ref[...]

    qn = _norm_rope(q_ref[...], qg_ref[...], cos2, sin2).astype(BF16)
    tile_r = lax.broadcasted_iota(jnp.int32, (HEAD_DIM, KV_DIM), 0)
    tile_c = lax.broadcasted_iota(jnp.int32, (HEAD_DIM, KV_DIM), 1)
    tile = jnp.where((tile_c & (HEAD_DIM - 1)) == tile_r, 1.0, 0.0).astype(BF16)
    qt = jnp.dot(qn, tile, preferred_element_type=F32)
    q_row = lax.broadcasted_iota(jnp.int32, (bs * N_HEADS, KV_DIM), 0)
    q_col = lax.broadcasted_iota(jnp.int32, (bs * N_HEADS, KV_DIM), 1)
    own = ((q_row & (N_HEADS - 1)) // GROUP) == (q_col // HEAD_DIM)
    qexp = jnp.where(own, qt, 0.0)

    kn = _norm_rope(k_ref[...], kg_ref[...], cos2, sin2)
    for b in range(bs):
        for h in range(N_KV_HEADS):
            r = b * N_KV_HEADS + h
            knew_ref[b:b + 1, h * HEAD_DIM:(h + 1) * HEAD_DIM] = kn[r:r + 1, :]
    knew = knew_ref[...]
    vnew = z_ref[:, V_OFF:V_OFF + KV_DIM]
    knew_r = knew.astype(BF16).astype(F32)
    vnew_r = vnew.astype(BF16).astype(F32)

    sink_col = jnp.concatenate([jnp.full((1, 1), sinks_ref[hh], F32) for hh in range(N_HEADS)], axis=0)
    key_idx = lax.broadcasted_iota(jnp.int32, (N_HEADS, WINDOW), 1)
    key_ok = key_idx >= 1
    o_row = lax.broadcasted_iota(jnp.int32, (N_HEADS, KV_DIM), 0)
    o_col = lax.broadcasted_iota(jnp.int32, (N_HEADS, KV_DIM), 1)
    o_own = (o_row // GROUP) == (o_col // HEAD_DIM)
    fold_r = lax.broadcasted_iota(jnp.int32, (KV_DIM, HEAD_DIM), 0)
    fold_c = lax.broadcasted_iota(jnp.int32, (KV_DIM, HEAD_DIM), 1)
    fold = jnp.where((fold_r & (HEAD_DIM - 1)) == fold_c, 1.0, 0.0).astype(BF16)

    for b in range(bs):
        qb = qexp[b * N_HEADS:(b + 1) * N_HEADS]
        kc = ck_ref[b]
        vc = cv_ref[b]
        s = lax.dot_general(qb.astype(BF16), kc.astype(BF16), (((1,), (1,)), ((), ())),
                            preferred_element_type=F32)
        s = jnp.where(key_ok, s * (HEAD_DIM ** -0.5), NEG)
        s_new = jnp.sum(qb * knew_r[b:b + 1, :], axis=-1, keepdims=True) * (HEAD_DIM ** -0.5)
        m = jnp.maximum(jnp.maximum(jnp.max(s, axis=-1, keepdims=True), s_new), sink_col)
        p = jnp.exp(s - m)
        p_new = jnp.exp(s_new - m)
        den = jnp.sum(p, axis=-1, keepdims=True) + p_new + jnp.exp(sink_col - m)
        p = p / den
        p_new = (p_new / den).astype(BF16).astype(F32)
        o = jnp.dot(p.astype(BF16), vc.astype(BF16), preferred_element_type=F32)
        o = o + p_new * vnew_r[b:b + 1, :]
        o = jnp.where(o_own, o, 0.0).astype(BF16)
        a_ref[b * N_HEADS:(b + 1) * N_HEADS, :] = jnp.dot(o, fold, preferred_element_type=F32)

        nk_ref[b, 0:WINDOW - 1, :] = ck_ref[b, 1:WINDOW, :]
        nk_ref[b, WINDOW - 1:WINDOW, :] = knew[b:b + 1, :]
        nv_ref[b, 0:WINDOW - 1, :] = cv_ref[b, 1:WINDOW, :]
        nv_ref[b, WINDOW - 1:WINDOW, :] = vnew[b:b + 1, :]

    u = z_ref[:, VAL_OFF:VAL_OFF + CONV_CH] * jax.nn.sigmoid(z_ref[:, GATE_OFF:GATE_OFF + CONV_CH])
    w_hist = cw_ref[0:CONV_STATE, :]
    w_last = cw_ref[CONV_STATE:CONV_WIDTH, :]
    for b in range(bs):
        st = st_ref[b]
        yb = jnp.sum(st * w_hist, axis=0, keepdims=True) + u[b:b + 1, :] * w_last + cb_ref[...]
        c_ref[b:b + 1, :] = _layernorm_swish(yb, lg_ref[...], lb_ref[...])
        nc_ref[b, 0:CONV_STATE - 1, :] = st_ref[b, 1:CONV_STATE, :]
        nc_ref[b, CONV_STATE - 1:CONV_STATE, :] = u[b:b + 1, :]


def _sample_mixer(zs, cache_k, cache_v, state, cos2, sin2, qg, kg, sinks, cw, cb, lg, lb, *, bs):
    nsamp = zs.shape[0]
    q_rows = zs[:, :ATTN_WIDTH].reshape(nsamp * N_HEADS, HEAD_DIM)
    k_rows = zs[:, K_OFF:K_OFF + KV_DIM].reshape(nsamp * N_KV_HEADS, HEAD_DIM)
    const = lambda i: (0, 0)
    step = lambda i: (i, 0)
    step3 = lambda i: (i, 0, 0)
    return pl.pallas_call(
        _sample_mixer_kernel,
        grid=(nsamp // bs,),
        in_specs=[
            pl.BlockSpec(memory_space=pltpu.SMEM),
            pl.BlockSpec((bs * N_HEADS, HEAD_DIM), step),
            pl.BlockSpec((bs * N_KV_HEADS, HEAD_DIM), step),
            pl.BlockSpec((bs, IN_COLS), step),
            pl.BlockSpec((bs, WINDOW, KV_DIM), step3),
            pl.BlockSpec((bs, WINDOW, KV_DIM), step3),
            pl.BlockSpec((bs, CONV_STATE, CONV_CH), step3),
            pl.BlockSpec((1, HEAD_DIM), const),
            pl.BlockSpec((1, HEAD_DIM), const),
            pl.BlockSpec((1, HEAD_DIM), const),
            pl.BlockSpec((1, HEAD_DIM), const),
            pl.BlockSpec((CONV_WIDTH, CONV_CH), const),
            pl.BlockSpec((1, CONV_CH), const),
            pl.BlockSpec((1, CONV_CH), const),
            pl.BlockSpec((1, CONV_CH), const),
        ],
        out_specs=[
            pl.BlockSpec((bs * N_HEADS, HEAD_DIM), step),
            pl.BlockSpec((bs, CONV_CH), step),
            pl.BlockSpec((bs, WINDOW, KV_DIM), step3),
            pl.BlockSpec((bs, WINDOW, KV_DIM), step3),
            pl.BlockSpec((bs, CONV_STATE, CONV_CH), step3),
        ],
        out_shape=[
            jax.ShapeDtypeStruct((nsamp * N_HEADS, HEAD_DIM), F32),
            jax.ShapeDtypeStruct((nsamp, CONV_CH), F32),
            jax.ShapeDtypeStruct((nsamp, WINDOW, KV_DIM), F32),
            jax.ShapeDtypeStruct((nsamp, WINDOW, KV_DIM), F32),
            jax.ShapeDtypeStruct((nsamp, CONV_STATE, CONV_CH), F32),
        ],
        scratch_shapes=[pltpu.VMEM((bs, KV_DIM), F32)],
        compiler_params=pltpu.CompilerParams(
            dimension_semantics=("arbitrary",), vmem_limit_bytes=VMEM_LIMIT_BYTES),
        name="sample_mixer",
    )(sinks, q_rows, k_rows, zs, cache_k, cache_v, state, cos2, sin2, qg, kg, cw, cb, lg, lb)


def _rope_tables(pos):
    inv = ROPE_THETA ** (-jnp.arange(HALF, dtype=F32) / HALF)
    ang = pos.astype(F32)[:, None] * inv[None, :]
    cos = jnp.cos(ang)
    sin = jnp.sin(ang)
    return jnp.concatenate([cos, cos], axis=-1), jnp.concatenate([-sin, sin], axis=-1)


def kernel(x_prompt, x_sample, cache_k, cache_v, state_conv, norm_mix_g, w_in, q_norm_g, k_norm_g,
           sinks, conv_w, conv_b, conv_ln_g, conv_ln_b, w_out, norm_mlp_g, w_up, w_down):
    batch, seq, d = x_prompt.shape
    nsamp, dec_seq, _ = x_sample.shape
    depth = w_in.shape[0]
    assert dec_seq == 1 and seq % WINDOW == 0 and cache_k.shape[2] == WINDOW

    cos_p, sin_p = _rope_tables(jnp.arange(seq))
    cos_s, sin_s = _rope_tables(PAST_LEN + jnp.arange(dec_seq))

    xp = x_prompt.reshape(batch * seq, d)
    xs = x_sample.reshape(nsamp * dec_seq, d)
    kp, vp, cp, ksm, vsm, csm = [], [], [], [], [], []
    for l in range(depth):
        qg = q_norm_g[l].reshape(1, HEAD_DIM)
        kg = k_norm_g[l].reshape(1, HEAD_DIM)
        cb = conv_b[l].reshape(1, CONV_CH)
        lg = conv_ln_g[l].reshape(1, CONV_CH)
        lb = conv_ln_b[l].reshape(1, CONV_CH)

        zp = _norm_matmul(xp, norm_mix_g[l], w_in[l], tm=1024, tn=512)
        mixp, nk, nv, nc = _prompt_mixer(zp, batch, seq, cos_p, sin_p, qg, kg, sinks[l],
                                         conv_w[l], cb, lg, lb)
        x1 = _residual_matmul(mixp, w_out[l], xp, tm=1024, tn=512)
        xp = _mlp(x1, norm_mlp_g[l], w_up[l], w_down[l], tm=1024, tf=512)
        kp.append(nk.reshape(batch, WINDOW, N_KV_HEADS, HEAD_DIM))
        vp.append(nv.reshape(batch, WINDOW, N_KV_HEADS, HEAD_DIM))
        cp.append(nc)

        zs = _norm_matmul(xs, norm_mix_g[l], w_in[l], tm=nsamp, tn=512)
        a_s, c_s, nk, nv, nc = _sample_mixer(
            zs, cache_k[l].reshape(nsamp, WINDOW, KV_DIM), cache_v[l].reshape(nsamp, WINDOW, KV_DIM),
            state_conv[l], cos_s, sin_s, qg, kg, sinks[l], conv_w[l], cb, lg, lb, bs=16)
        mixs = jnp.concatenate([a_s.reshape(nsamp, ATTN_WIDTH), c_s], axis=-1).astype(BF16)
        x1s = _residual_matmul(mixs, w_out[l], xs, tm=nsamp, tn=512)
        xs = _mlp(x1s, norm_mlp_g[l], w_up[l], w_down[l], tm=nsamp, tf=512)
        ksm.append(nk.reshape(nsamp, WINDOW, N_KV_HEADS, HEAD_DIM))
        vsm.append(nv.reshape(nsamp, WINDOW, N_KV_HEADS, HEAD_DIM))
        csm.append(nc)

    return (xp.reshape(batch, seq, d), xs.reshape(nsamp, dec_seq, d),
            jnp.stack(kp), jnp.stack(vp), jnp.stack(cp),
            jnp.stack(ksm), jnp.stack(vsm), jnp.stack(csm))
```

```python
import functools

import jax
import jax.numpy as jnp
from jax import lax
from jax.experimental import pallas as pl
from jax.experimental.pallas import tpu as pltpu

D_MODEL = 2048
ATTN_WIDTH = 1024
CONV_CH = 1024
HEAD_DIM = 64
HALF = HEAD_DIM // 2
N_HEADS = 16
N_KV_HEADS = 4
GROUP = N_HEADS // N_KV_HEADS
KV_DIM = N_KV_HEADS * HEAD_DIM
WINDOW = 128
CONV_WIDTH = 31
CONV_STATE = CONV_WIDTH - 1
D_FF = 4 * D_MODEL
ROPE_THETA = 10000.0
EPS = 1e-6
IN_COLS = ATTN_WIDTH + 2 * KV_DIM + 2 * CONV_CH
NEG = -1e30
PAST_LEN = 16384

K_OFF = ATTN_WIDTH
V_OFF = ATTN_WIDTH + KV_DIM
VAL_OFF = ATTN_WIDTH + 2 * KV_DIM
GATE_OFF = VAL_OFF + CONV_CH

VMEM_LIMIT_BYTES = 56 * 1024 * 1024
SUBLANES = 8
LANES = 128
BF16_ROWS = 16
VT_ROWS = HEAD_DIM + BF16_ROWS
CONV_PAD = 32

F32 = jnp.float32
BF16 = jnp.bfloat16


def _rms_rows(x, g):
    ms = jnp.mean(x * x, axis=-1, keepdims=True)
    return x * lax.rsqrt(ms + EPS) * g


def _norm_matmul_kernel(x_ref, g_ref, w_ref, o_ref, xn_ref):
    @pl.when(pl.program_id(1) == 0)
    def _():
        xn_ref[...] = _rms_rows(x_ref[...], g_ref[...]).astype(BF16)

    o_ref[...] = jnp.dot(xn_ref[...], w_ref[...].astype(BF16), preferred_element_type=F32)


def _norm_matmul(x, g, w, *, tm, tn):
    m, k = x.shape
    n = w.shape[1]
    return pl.pallas_call(
        _norm_matmul_kernel,
        grid=(m // tm, n // tn),
        in_specs=[
            pl.BlockSpec((tm, k), lambda i, j: (i, 0)),
            pl.BlockSpec((1, k), lambda i, j: (0, 0)),
            pl.BlockSpec((k, tn), lambda i, j: (0, j)),
        ],
        out_specs=pl.BlockSpec((tm, tn), lambda i, j: (i, j)),
        out_shape=jax.ShapeDtypeStruct((m, n), F32),
        scratch_shapes=[pltpu.VMEM((tm, k), BF16)],
        compiler_params=pltpu.CompilerParams(
            dimension_semantics=("arbitrary", "arbitrary"), vmem_limit_bytes=VMEM_LIMIT_BYTES),
        name="norm_matmul",
    )(x, g.reshape(1, k), w)


def _residual_matmul_kernel(a_ref, w_ref, x_ref, o_ref):
    o_ref[...] = x_ref[...] + jnp.dot(a_ref[...], w_ref[...].astype(BF16), preferred_element_type=F32)


def _residual_matmul(a, w, x, *, tm, tn):
    m, k = a.shape
    n = w.shape[1]
    return pl.pallas_call(
        _residual_matmul_kernel,
        grid=(m // tm, n // tn),
        in_specs=[
            pl.BlockSpec((tm, k), lambda i, j: (i, 0)),
            pl.BlockSpec((k, tn), lambda i, j: (0, j)),
            pl.BlockSpec((tm, tn), lambda i, j: (i, j)),
        ],
        out_specs=pl.BlockSpec((tm, tn), lambda i, j: (i, j)),
        out_shape=jax.ShapeDtypeStruct((m, n), F32),
        compiler_params=pltpu.CompilerParams(
            dimension_semantics=("arbitrary", "arbitrary"), vmem_limit_bytes=VMEM_LIMIT_BYTES),
        name="residual_matmul",
    )(a, w, x)


def _mlp_kernel(x_ref, g_ref, wu_ref, wd_ref, o_ref, hm_ref):
    @pl.when(pl.program_id(1) == 0)
    def _():
        x = x_ref[...]
        hm_ref[...] = _rms_rows(x, g_ref[...]).astype(BF16)
        o_ref[...] = x

    h = jnp.dot(hm_ref[...], wu_ref[...].astype(BF16), preferred_element_type=F32)
    h = jnp.square(jnp.maximum(h, 0.0)).astype(BF16)
    o_ref[...] += jnp.dot(h, wd_ref[...].astype(BF16), preferred_element_type=F32)


def _mlp(x, g, wu, wd, *, tm, tf):
    m, d = x.shape
    f = wu.shape[1]
    return pl.pallas_call(
        _mlp_kernel,
        grid=(m // tm, f // tf),
        in_specs=[
            pl.BlockSpec((tm, d), lambda i, j: (i, 0), pipeline_mode=pl.Buffered(1)),
            pl.BlockSpec((1, d), lambda i, j: (0, 0)),
            pl.BlockSpec((d, tf), lambda i, j: (0, j)),
            pl.BlockSpec((tf, d), lambda i, j: (j, 0)),
        ],
        out_specs=pl.BlockSpec((tm, d), lambda i, j: (i, 0)),
        out_shape=jax.ShapeDtypeStruct((m, d), F32),
        scratch_shapes=[pltpu.VMEM((tm, d), BF16)],
        compiler_params=pltpu.CompilerParams(
            dimension_semantics=("arbitrary", "arbitrary"), vmem_limit_bytes=VMEM_LIMIT_BYTES),
        name="mlp",
    )(x, g.reshape(1, d), wu, wd)


def _rotate_half(x):
    return jnp.concatenate([x[:, HALF:], x[:, :HALF]], axis=-1)


def _norm_rope(x, g, cos2, sin2):
    xn = _rms_rows(x, g)
    return xn * cos2 + _rotate_half(xn) * sin2


def _layernorm_swish(c, g, b):
    mu = jnp.mean(c, axis=-1, keepdims=True)
    xc = c - mu
    y = xc * lax.rsqrt(jnp.mean(xc * xc, axis=-1, keepdims=True) + EPS)
    y = y * g + b
    return y * jax.nn.sigmoid(y)


def _conv_taps_by_row_phase():
    groups = [[] for _ in range(SUBLANES)]
    for j in range(CONV_WIDTH):
        e = j + CONV_PAD - CONV_STATE
        groups[e % SUBLANES].append((j, e // SUBLANES))
    return groups


def _prompt_mixer_kernel(sinks_ref, z_ref, cos_ref, sin_ref, qg_ref, kg_ref, cw_ref, cb_ref,
                         lg_ref, lb_ref, mix_ref, nk_ref, nv_ref, nc_ref,
                         klo_ref, khi_ref, vt_ref, ubuf_ref):
    n = pl.program_id(1)
    nb = pl.num_programs(1)
    n_q_slabs = ATTN_WIDTH // LANES
    n_k_slabs = KV_DIM // LANES

    @pl.when(n == 0)
    def _():
        klo_ref[:, 0:WINDOW, :] = jnp.zeros((N_KV_HEADS, WINDOW, LANES), BF16)
        khi_ref[:, 0:WINDOW, :] = jnp.zeros((N_KV_HEADS, WINDOW, LANES), BF16)
        vt_ref[:, 0:HEAD_DIM, 0:WINDOW] = jnp.zeros((N_KV_HEADS, HEAD_DIM, WINDOW), BF16)
        vt_ref[:, HEAD_DIM:, :] = jnp.ones((N_KV_HEADS, VT_ROWS - HEAD_DIM, 2 * WINDOW), BF16)
        ubuf_ref[0:CONV_PAD, :] = jnp.zeros((CONV_PAD, CONV_CH), F32)

    @pl.when(n > 0)
    def _():
        klo_ref[:, 0:WINDOW, :] = klo_ref[:, WINDOW:2 * WINDOW, :]
        khi_ref[:, 0:WINDOW, :] = khi_ref[:, WINDOW:2 * WINDOW, :]
        vt_ref[:, 0:HEAD_DIM, 0:WINDOW] = vt_ref[:, 0:HEAD_DIM, WINDOW:2 * WINDOW]
        ubuf_ref[0:CONV_PAD, :] = ubuf_ref[WINDOW:WINDOW + CONV_PAD, :]

    u = z_ref[:, VAL_OFF:VAL_OFF + CONV_CH] * jax.nn.sigmoid(z_ref[:, GATE_OFF:GATE_OFF + CONV_CH])
    ubuf_ref[CONV_PAD:CONV_PAD + WINDOW, :] = u

    lane = lax.broadcasted_iota(jnp.int32, (WINDOW, LANES), 1)
    first_head = lane < HEAD_DIM
    first_half = (lane & HALF) == 0

    n_slabs = n_q_slabs + n_k_slabs
    x = jnp.concatenate([z_ref[:, m * LANES:(m + 1) * LANES] for m in range(n_slabs)], axis=0)
    sq = x * x
    sq_hi = sq.astype(BF16)
    sq_lo = (sq - sq_hi.astype(F32)).astype(BF16)
    er = lax.broadcasted_iota(jnp.int32, (LANES, LANES), 0)
    ec = lax.broadcasted_iota(jnp.int32, (LANES, LANES), 1)
    head_mean = jnp.where((er < HEAD_DIM) == (ec < HEAD_DIM), 1.0 / HEAD_DIM, 0.0).astype(BF16)
    ms = (jnp.dot(sq_hi, head_mean, preferred_element_type=F32)
          + jnp.dot(sq_lo, head_mean, preferred_element_type=F32))
    xn = x * lax.rsqrt(ms + EPS)

    cos4 = cos_ref[...]
    sin4 = sin_ref[...]

    def rope(xs):
        rot = jnp.where(first_half, pltpu.roll(xs, LANES - HALF, 1), pltpu.roll(xs, HALF, 1))
        return xs * cos4 + rot * sin4

    q_slabs = [rope(xn[m * WINDOW:(m + 1) * WINDOW] * qg_ref[...]).astype(BF16) for m in range(n_q_slabs)]

    zero = jnp.zeros((WINDOW, LANES), F32)
    for ks in range(n_k_slabs):
        r0 = (n_q_slabs + ks) * WINDOW
        kr = rope(xn[r0:r0 + WINDOW] * kg_ref[...])
        nk_ref[0, :, ks * LANES:(ks + 1) * LANES] = kr
        kr_swapped = pltpu.roll(kr, HEAD_DIM, 1)
        klo_ref[2 * ks, WINDOW:2 * WINDOW, :] = jnp.where(first_head, kr, zero).astype(BF16)
        khi_ref[2 * ks, WINDOW:2 * WINDOW, :] = jnp.where(first_head, zero, kr_swapped).astype(BF16)
        klo_ref[2 * ks + 1, WINDOW:2 * WINDOW, :] = jnp.where(first_head, kr_swapped, zero).astype(BF16)
        khi_ref[2 * ks + 1, WINDOW:2 * WINDOW, :] = jnp.where(first_head, zero, kr).astype(BF16)

        vs = z_ref[:, V_OFF + ks * LANES:V_OFF + (ks + 1) * LANES]
        nv_ref[0, :, ks * LANES:(ks + 1) * LANES] = vs
        vs_t = vs.T.astype(BF16)
        vt_ref[2 * ks, 0:HEAD_DIM, WINDOW:2 * WINDOW] = vs_t[0:HEAD_DIM]
        vt_ref[2 * ks + 1, 0:HEAD_DIM, WINDOW:2 * WINDOW] = vs_t[HEAD_DIM:2 * HEAD_DIM]

    key = lax.broadcasted_iota(jnp.int32, (2 * WINDOW, WINDOW), 0)
    qry = lax.broadcasted_iota(jnp.int32, (2 * WINDOW, WINDOW), 1)
    diff = qry + WINDOW - key
    band = (diff >= 0) & (diff <= jnp.where(n > 0, WINDOW - 1, qry))
    bias = jnp.where(band, 0.0, NEG)
    bias2 = jnp.concatenate([bias, bias], axis=1)

    def conv_chunk(c):
        cl = slice(c * LANES, (c + 1) * LANES)
        y = cb_ref[:, cl]
        for phase, taps in enumerate(_conv_taps_by_row_phase()):
            rows_needed = WINDOW + (SUBLANES if phase else 0)
            part = None
            for j, tile in taps:
                term = cw_ref[j:j + 1, cl] * ubuf_ref[tile * SUBLANES:tile * SUBLANES + rows_needed, cl]
                part = term if part is None else part + term
            if phase:
                part = pltpu.roll(part, rows_needed - phase, 0)[0:WINDOW]
            y = y + part
        return y

    conv_per_head = CONV_CH // LANES // N_KV_HEADS
    y_chunks = []
    for h in range(N_KV_HEADS):
        q2 = jnp.concatenate([q_slabs[2 * h], q_slabs[2 * h + 1]], axis=0)
        normed_t = []
        for second, k_ref in ((0, klo_ref), (1, khi_ref)):
            s_t = lax.dot_general(k_ref[h], q2, (((1,), (1,)), ((), ())), preferred_element_type=F32)
            s_t = s_t + bias2
            sink_row = jnp.concatenate(
                [jnp.full((1, WINDOW), sinks_ref[GROUP * h + 2 * r + second], F32) for r in range(2)], axis=1)
            m = jnp.maximum(jnp.max(s_t, axis=0, keepdims=True), sink_row)
            p_t = jnp.exp(s_t - m).astype(BF16)
            o_t = jnp.dot(vt_ref[h], p_t, preferred_element_type=F32)
            den = o_t[HEAD_DIM:HEAD_DIM + 1] + jnp.exp(sink_row - m)
            normed_t.append(o_t[0:HEAD_DIM] * (1.0 / den))
        for r in range(2):
            a_t = jnp.concatenate([normed_t[0][:, r * WINDOW:(r + 1) * WINDOW],
                                   normed_t[1][:, r * WINDOW:(r + 1) * WINDOW]], axis=0)
            m_out = 2 * h + r
            mix_ref[:, m_out * LANES:(m_out + 1) * LANES] = a_t.T.astype(BF16)
        y_chunks += [conv_chunk(conv_per_head * h + i) for i in range(conv_per_head)]

    conv = jnp.concatenate(y_chunks, axis=-1)
    mix_ref[:, ATTN_WIDTH:] = _layernorm_swish(conv, lg_ref[...], lb_ref[...]).astype(BF16)

    @pl.when(n == nb - 1)
    def _():
        nc_ref[0] = ubuf_ref[CONV_PAD + WINDOW - CONV_STATE:CONV_PAD + WINDOW, :]


def _prompt_mixer(z, batch, seq, cos4, sin4, qg2, kg2, sinks, cw, cb, lg, lb):
    nb = seq // WINDOW
    n_k_slabs = KV_DIM // LANES
    row = lambda b, n: (b * nb + n, 0)
    const = lambda b, n: (0, 0)
    per_batch = lambda b, n: (b, 0, 0)
    return pl.pallas_call(
        _prompt_mixer_kernel,
        grid=(batch, nb),
        in_specs=[
            pl.BlockSpec(memory_space=pltpu.SMEM),
            pl.BlockSpec((WINDOW, IN_COLS), row),
            pl.BlockSpec((WINDOW, LANES), lambda b, n: (n, 0)),
            pl.BlockSpec((WINDOW, LANES), lambda b, n: (n, 0)),
            pl.BlockSpec((1, LANES), const),
            pl.BlockSpec((1, LANES), const),
            pl.BlockSpec((CONV_WIDTH, CONV_CH), const),
            pl.BlockSpec((1, CONV_CH), const),
            pl.BlockSpec((1, CONV_CH), const),
            pl.BlockSpec((1, CONV_CH), const),
        ],
        out_specs=[
            pl.BlockSpec((WINDOW, ATTN_WIDTH + CONV_CH), row),
            pl.BlockSpec((1, WINDOW, KV_DIM), per_batch),
            pl.BlockSpec((1, WINDOW, KV_DIM), per_batch),
            pl.BlockSpec((1, CONV_STATE, CONV_CH), per_batch),
        ],
        out_shape=[
            jax.ShapeDtypeStruct((batch * seq, ATTN_WIDTH + CONV_CH), BF16),
            jax.ShapeDtypeStruct((batch, WINDOW, KV_DIM), F32),
            jax.ShapeDtypeStruct((batch, WINDOW, KV_DIM), F32),
            jax.ShapeDtypeStruct((batch, CONV_STATE, CONV_CH), F32),
        ],
        scratch_shapes=[
            pltpu.VMEM((N_KV_HEADS, 2 * WINDOW, LANES), BF16),
            pltpu.VMEM((N_KV_HEADS, 2 * WINDOW, LANES), BF16),
            pltpu.VMEM((N_KV_HEADS, VT_ROWS, 2 * WINDOW), BF16),
            pltpu.VMEM((CONV_PAD + WINDOW, CONV_CH), F32),
        ],
        compiler_params=pltpu.CompilerParams(
            dimension_semantics=("arbitrary", "arbitrary"), vmem_limit_bytes=VMEM_LIMIT_BYTES),
        name="prompt_mixer",
    )(sinks, z, cos4, sin4, qg2, kg2, cw, cb, lg, lb)


def _sample_mixer_kernel(sinks_ref, q_ref, k_ref, z_ref, ck_ref, cv_ref, st_ref, cos_ref, sin_ref,
                         qg_ref, kg_ref, cw_ref, cb_ref, lg_ref, lb_ref,
                         a_ref, c_ref, nk_ref, nv_ref, nc_ref, knew_ref):
    bs = z_ref.shape[0]
    cos2 = cos_ref[...]
    sin2 = sin_ref[...]

    qn = _norm_rope(q_ref[...], qg_ref[...], cos2, sin2).astype(BF16)
    tile_r = lax.broadcasted_iota(jnp.int32, (HEAD_DIM, KV_DIM), 0)
    tile_c = lax.broadcasted_iota(jnp.int32, (HEAD_DIM, KV_DIM), 1)
    tile = jnp.where((tile_c & (HEAD_DIM - 1)) == tile_r, 1.0, 0.0).astype(BF16)
    qt = jnp.dot(qn, tile, preferred_element_type=F32)
    q_row = lax.broadcasted_iota(jnp.int32, (bs * N_HEADS, KV_DIM), 0)
    q_col = lax.broadcasted_iota(jnp.int32, (bs * N_HEADS, KV_DIM), 1)
    own = ((q_row & (N_HEADS - 1)) // GROUP) == (q_col // HEAD_DIM)
    qexp = jnp.where(own, qt, 0.0)

    kn = _norm_rope(k_ref[...], kg_ref[...], cos2, sin2)
    for b in range(bs):
        for h in range(N_KV_HEADS):
            r = b * N_KV_HEADS + h
            knew_ref[b:b + 1, h * HEAD_DIM:(h + 1) * HEAD_DIM] = kn[r:r + 1, :]
    knew = knew_ref[...]
    vnew = z_ref[:, V_OFF:V_OFF + KV_DIM]
    knew_r = knew.astype(BF16).astype(F32)
    vnew_r = vnew.astype(BF16).astype(F32)

    sink_col = jnp.concatenate([jnp.full((1, 1), sinks_ref[hh], F32) for hh in range(N_HEADS)], axis=0)
    key_idx = lax.broadcasted_iota(jnp.int32, (N_HEADS, WINDOW), 1)
    key_ok = key_idx >= 1
    o_row = lax.broadcasted_iota(jnp.int32, (N_HEADS, KV_DIM), 0)
    o_col = lax.broadcasted_iota(jnp.int32, (N_HEADS, KV_DIM), 1)
    o_own = (o_row // GROUP) == (o_col // HEAD_DIM)
    fold_r = lax.broadcasted_iota(jnp.int32, (KV_DIM, HEAD_DIM), 0)
    fold_c = lax.broadcasted_iota(jnp.int32, (KV_DIM, HEAD_DIM), 1)
    fold = jnp.where((fold_r & (HEAD_DIM - 1)) == fold_c, 1.0, 0.0).astype(BF16)

    for b in range(bs):
        qb = qexp[b * N_HEADS:(b + 1) * N_HEADS]
        kc = ck_ref[b]
        vc = cv_ref[b]
        s = lax.dot_general(qb.astype(BF16), kc.astype(BF16), (((1,), (1,)), ((), ())),
                            preferred_element_type=F32)
        s = jnp.where(key_ok, s * (HEAD_DIM ** -0.5), NEG)
        s_new = jnp.sum(qb * knew_r[b:b + 1, :], axis=-1, keepdims=True) * (HEAD_DIM ** -0.5)
        m = jnp.maximum(jnp.maximum(jnp.max(s, axis=-1, keepdims=True), s_new), sink_col)
        p = jnp.exp(s - m)
        p_new = jnp.exp(s_new - m)
        den = jnp.sum(p, axis=-1, keepdims=True) + p_new + jnp.exp(sink_col - m)
        p = p / den
        p_new = (p_new / den).astype(BF16).astype(F32)
        o = jnp.dot(p.astype(BF16), vc.astype(BF16), preferred_element_type=F32)
        o = o + p_new * vnew_r[b:b + 1, :]
        o = jnp.where(o_own, o, 0.0).astype(BF16)
        a_ref[b * N_HEADS:(b + 1) * N_HEADS, :] = jnp.dot(o, fold, preferred_element_type=F32)

        nk_ref[b, 0:WINDOW - 1, :] = ck_ref[b, 1:WINDOW, :]
        nk_ref[b, WINDOW - 1:WINDOW, :] = knew[b:b + 1, :]
        nv_ref[b, 0:WINDOW - 1, :] = cv_ref[b, 1:WINDOW, :]
        nv_ref[b, WINDOW - 1:WINDOW, :] = vnew[b:b + 1, :]

    u = z_ref[:, VAL_OFF:VAL_OFF + CONV_CH] * jax.nn.sigmoid(z_ref[:, GATE_OFF:GATE_OFF + CONV_CH])
    w_hist = cw_ref[0:CONV_STATE, :]
    w_last = cw_ref[CONV_STATE:CONV_WIDTH, :]
    for b in range(bs):
        st = st_ref[b]
        yb = jnp.sum(st * w_hist, axis=0, keepdims=True) + u[b:b + 1, :] * w_last + cb_ref[...]
        c_ref[b:b + 1, :] = _layernorm_swish(yb, lg_ref[...], lb_ref[...])
        nc_ref[b, 0:CONV_STATE - 1, :] = st_ref[b, 1:CONV_STATE, :]
        nc_ref[b, CONV_STATE - 1:CONV_STATE, :] = u[b:b + 1, :]


def _sample_mixer(zs, cache_k, cache_v, state, cos2, sin2, qg, kg, sinks, cw, cb, lg, lb, *, bs):
    nsamp = zs.shape[0]
    q_rows = zs[:, :ATTN_WIDTH].reshape(nsamp * N_HEADS, HEAD_DIM)
    k_rows = zs[:, K_OFF:K_OFF + KV_DIM].reshape(nsamp * N_KV_HEADS, HEAD_DIM)
    const = lambda i: (0, 0)
    step = lambda i: (i, 0)
    step3 = lambda i: (i, 0, 0)
    return pl.pallas_call(
        _sample_mixer_kernel,
        grid=(nsamp // bs,),
        in_specs=[
            pl.BlockSpec(memory_space=pltpu.SMEM),
            pl.BlockSpec((bs * N_HEADS, HEAD_DIM), step),
            pl.BlockSpec((bs * N_KV_HEADS, HEAD_DIM), step),
            pl.BlockSpec((bs, IN_COLS), step),
            pl.BlockSpec((bs, WINDOW, KV_DIM), step3),
            pl.BlockSpec((bs, WINDOW, KV_DIM), step3),
            pl.BlockSpec((bs, CONV_STATE, CONV_CH), step3),
            pl.BlockSpec((1, HEAD_DIM), const),
            pl.BlockSpec((1, HEAD_DIM), const),
            pl.BlockSpec((1, HEAD_DIM), const),
            pl.BlockSpec((1, HEAD_DIM), const),
            pl.BlockSpec((CONV_WIDTH, CONV_CH), const),
            pl.BlockSpec((1, CONV_CH), const),
            pl.BlockSpec((1, CONV_CH), const),
            pl.BlockSpec((1, CONV_CH), const),
        ],
        out_specs=[
            pl.BlockSpec((bs * N_HEADS, HEAD_DIM), step),
            pl.BlockSpec((bs, CONV_CH), step),
            pl.BlockSpec((bs, WINDOW, KV_DIM), step3),
            pl.BlockSpec((bs, WINDOW, KV_DIM), step3),
            pl.BlockSpec((bs, CONV_STATE, CONV_CH), step3),
        ],
        out_shape=[
            jax.ShapeDtypeStruct((nsamp * N_HEADS, HEAD_DIM), F32),
            jax.ShapeDtypeStruct((nsamp, CONV_CH), F32),
            jax.ShapeDtypeStruct((nsamp, WINDOW, KV_DIM), F32),
            jax.ShapeDtypeStruct((nsamp, WINDOW, KV_DIM), F32),
            jax.ShapeDtypeStruct((nsamp, CONV_STATE, CONV_CH), F32),
        ],
        scratch_shapes=[pltpu.VMEM((bs, KV_DIM), F32)],
        compiler_params=pltpu.CompilerParams(
            dimension_semantics=("arbitrary",), vmem_limit_bytes=VMEM_LIMIT_BYTES),
        name="sample_mixer",
    )(sinks, q_rows, k_rows, zs, cache_k, cache_v, state, cos2, sin2, qg, kg, cw, cb, lg, lb)


def _rope_tables(pos):
    inv = ROPE_THETA ** (-jnp.arange(HALF, dtype=F32) / HALF)
    ang = pos.astype(F32)[:, None] * inv[None, :]
    cos = jnp.cos(ang)
    sin = jnp.sin(ang)
    return jnp.concatenate([cos, cos], axis=-1), jnp.concatenate([-sin, sin], axis=-1)


def kernel(x_prompt, x_sample, cache_k, cache_v, state_conv, norm_mix_g, w_in, q_norm_g, k_norm_g,
           sinks, conv_w, conv_b, conv_ln_g, conv_ln_b, w_out, norm_mlp_g, w_up, w_down):
    batch, seq, d = x_prompt.shape
    nsamp, dec_seq, _ = x_sample.shape
    depth = w_in.shape[0]
    assert dec_seq == 1 and seq % WINDOW == 0 and cache_k.shape[2] == WINDOW

    cos_p, sin_p = _rope_tables(jnp.arange(seq))
    cos_s, sin_s = _rope_tables(PAST_LEN + jnp.arange(dec_seq))

    xp = x_prompt.reshape(batch * seq, d)
    xs = x_sample.reshape(nsamp * dec_seq, d)
    kp, vp, cp, ksm, vsm, csm = [], [], [], [], [], []
    for l in range(depth):
        qg = q_norm_g[l].reshape(1, HEAD_DIM)
        kg = k_norm_g[l].reshape(1, HEAD_DIM)
        cb = conv_b[l].reshape(1, CONV_CH)
        lg = conv_ln_g[l].reshape(1, CONV_CH)
        lb = conv_ln_b[l].reshape(1, CONV_CH)

        zp = _norm_matmul(xp, norm_mix_g[l], w_in[l], tm=1024, tn=512)
        qg2 = jnp.tile(qg * (HEAD_DIM ** -0.5), (1, LANES // HEAD_DIM))
        kg2 = jnp.tile(kg, (1, LANES // HEAD_DIM))
        cos4 = jnp.tile(cos_p, (1, LANES // HEAD_DIM))
        sin4 = jnp.tile(sin_p, (1, LANES // HEAD_DIM))
        mixp, nk, nv, nc = _prompt_mixer(zp, batch, seq, cos4, sin4, qg2, kg2, sinks[l],
                                         conv_w[l], cb, lg, lb)
        x1 = _residual_matmul(mixp, w_out[l], xp, tm=1024, tn=512)
        xp = _mlp(x1, norm_mlp_g[l], w_up[l], w_down[l], tm=1024, tf=512)
        kp.append(nk.reshape(batch, WINDOW, N_KV_HEADS, HEAD_DIM))
        vp.append(nv.reshape(batch, WINDOW, N_KV_HEADS, HEAD_DIM))
        cp.append(nc)

        zs = _norm_matmul(xs, norm_mix_g[l], w_in[l], tm=nsamp, tn=512)
        a_s, c_s, nk, nv, nc = _sample_mixer(
            zs, cache_k[l].reshape(nsamp, WINDOW, KV_DIM), cache_v[l].reshape(nsamp, WINDOW, KV_DIM),
            state_conv[l], cos_s, sin_s, qg, kg, sinks[l], conv_w[l], cb, lg, lb, bs=16)
        mixs = jnp.concatenate([a_s.reshape(nsamp, ATTN_WIDTH), c_s], axis=-1).astype(BF16)
        x1s = _residual_matmul(mixs, w_out[l], xs, tm=nsamp, tn=512)
        xs = _mlp(x1s, norm_mlp_g[l], w_up[l], w_down[l], tm=nsamp, tf=512)
        ksm.append(nk.reshape(nsamp, WINDOW, N_KV_HEADS, HEAD_DIM))
        vsm.append(nv.reshape(nsamp, WINDOW, N_KV_HEADS, HEAD_DIM))
        csm.append(nc)

    return (xp.reshape(batch, seq, d), xs.reshape(nsamp, dec_seq, d),
            jnp.stack(kp), jnp.stack(vp), jnp.stack(cp),
            jnp.stack(ksm), jnp.stack(vsm), jnp.stack(csm))
```

```python
import functools

import jax
import jax.numpy as jnp
from jax import lax
from jax.experimental import pallas as pl
from jax.experimental.pallas import tpu as pltpu

D_MODEL = 2048
ATTN_WIDTH = 1024
CONV_CH = 1024
HEAD_DIM = 64
HALF = HEAD_DIM // 2
N_HEADS = 16
N_KV_HEADS = 4
GROUP = N_HEADS // N_KV_HEADS
KV_DIM = N_KV_HEADS * HEAD_DIM
WINDOW = 128
CONV_WIDTH = 31
CONV_STATE = CONV_WIDTH - 1
D_FF = 4 * D_MODEL
ROPE_THETA = 10000.0
EPS = 1e-6
IN_COLS = ATTN_WIDTH + 2 * KV_DIM + 2 * CONV_CH
NEG = -1e30
PAST_LEN = 16384

K_OFF = ATTN_WIDTH
V_OFF = ATTN_WIDTH + KV_DIM
VAL_OFF = ATTN_WIDTH + 2 * KV_DIM
GATE_OFF = VAL_OFF + CONV_CH

VMEM_LIMIT_BYTES = 56 * 1024 * 1024
SUBLANES = 8
LANES = 128
BF16_ROWS = 16
VT_ROWS = HEAD_DIM + BF16_ROWS
CONV_PAD = 32

ROW_BLOCK = 1024
IN_PROJ_TILE = 896
OUT_PROJ_TILE = 1024
MLP_TILE = 512
CAST_ROWS = 256

F32 = jnp.float32
BF16 = jnp.bfloat16


def _rms_rows(x, g):
    ms = jnp.mean(x * x, axis=-1, keepdims=True)
    return x * lax.rsqrt(ms + EPS) * g


_GRID_PARAMS = dict(dimension_semantics=("arbitrary", "arbitrary"), vmem_limit_bytes=VMEM_LIMIT_BYTES)


def _col_tile(j, width):
    return pl.ds(pl.multiple_of(j * width, LANES), width)


def _norm_matmul_kernel(x_ref, xs_ref, g_ref, w_ref, o_ref, os_ref, xn_ref, xsn_ref):
    i, j = pl.program_id(0), pl.program_id(1)

    @pl.when(j == 0)
    def _():
        xn_ref[...] = _rms_rows(x_ref[...], g_ref[...]).astype(BF16)

    @pl.when((i == 0) & (j == 0))
    def _():
        xsn_ref[...] = _rms_rows(xs_ref[...], g_ref[...]).astype(BF16)

    w = w_ref[...].astype(BF16)
    o_ref[...] = jnp.dot(xn_ref[...], w, preferred_element_type=F32)

    @pl.when(i == 0)
    def _():
        os_ref[:, _col_tile(j, w.shape[1])] = jnp.dot(xsn_ref[...], w, preferred_element_type=F32)


def _norm_matmul(x, xs, g, w, *, tm, tn):
    m, k = x.shape
    ms = xs.shape[0]
    n = w.shape[1]
    return pl.pallas_call(
        _norm_matmul_kernel,
        grid=(m // tm, n // tn),
        in_specs=[
            pl.BlockSpec((tm, k), lambda i, j: (i, 0), pipeline_mode=pl.Buffered(1)),
            pl.BlockSpec((ms, k), lambda i, j: (0, 0)),
            pl.BlockSpec((1, k), lambda i, j: (0, 0)),
            pl.BlockSpec((k, tn), lambda i, j: (0, j)),
        ],
        out_specs=[
            pl.BlockSpec((tm, tn), lambda i, j: (i, j)),
            pl.BlockSpec((ms, n), lambda i, j: (0, 0)),
        ],
        out_shape=[jax.ShapeDtypeStruct((m, n), F32), jax.ShapeDtypeStruct((ms, n), F32)],
        scratch_shapes=[pltpu.VMEM((tm, k), BF16), pltpu.VMEM((ms, k), BF16)],
        compiler_params=pltpu.CompilerParams(**_GRID_PARAMS),
        name="norm_matmul",
    )(x, xs, g.reshape(1, k), w)


def _residual_matmul_kernel(a_ref, as_ref, w_ref, x_ref, xs_ref, o_ref, os_ref):
    i, j = pl.program_id(0), pl.program_id(1)
    w = w_ref[...].astype(BF16)
    o_ref[...] = x_ref[...] + jnp.dot(a_ref[...], w, preferred_element_type=F32)

    @pl.when(i == 0)
    def _():
        cols = _col_tile(j, w.shape[1])
        os_ref[:, cols] = xs_ref[:, cols] + jnp.dot(as_ref[...], w, preferred_element_type=F32)


def _residual_matmul(a, a_s, w, x, xs, *, tm, tn):
    m, k = a.shape
    ms = a_s.shape[0]
    n = w.shape[1]
    return pl.pallas_call(
        _residual_matmul_kernel,
        grid=(m // tm, n // tn),
        in_specs=[
            pl.BlockSpec((tm, k), lambda i, j: (i, 0)),
            pl.BlockSpec((ms, k), lambda i, j: (0, 0)),
            pl.BlockSpec((k, tn), lambda i, j: (0, j)),
            pl.BlockSpec((tm, tn), lambda i, j: (i, j)),
            pl.BlockSpec((ms, n), lambda i, j: (0, 0)),
        ],
        out_specs=[
            pl.BlockSpec((tm, tn), lambda i, j: (i, j)),
            pl.BlockSpec((ms, n), lambda i, j: (0, 0)),
        ],
        out_shape=[jax.ShapeDtypeStruct((m, n), F32), jax.ShapeDtypeStruct((ms, n), F32)],
        compiler_params=pltpu.CompilerParams(**_GRID_PARAMS),
        name="residual_matmul",
    )(a, a_s, w, x, xs)


def _cast_tiles_kernel(w_ref, o_ref):
    n_tiles, _, tn = o_ref.shape
    for t in range(n_tiles):
        o_ref[t] = w_ref[:, t * tn:(t + 1) * tn].astype(BF16)


def _cast_tiles(w, *, tk, tn):
    k, n = w.shape
    return pl.pallas_call(
        _cast_tiles_kernel,
        grid=(k // tk,),
        in_specs=[pl.BlockSpec((tk, n), lambda r: (r, 0))],
        out_specs=pl.BlockSpec((n // tn, tk, tn), lambda r: (0, r, 0)),
        out_shape=jax.ShapeDtypeStruct((n // tn, k, tn), BF16),
        compiler_params=pltpu.CompilerParams(
            dimension_semantics=("arbitrary",), vmem_limit_bytes=VMEM_LIMIT_BYTES),
        name="cast_tiles",
    )(w)


def _mlp_kernel(x_ref, xs_ref, g_ref, wu_ref, wd_ref, o_ref, os_ref, hm_ref, hms_ref):
    i, j = pl.program_id(0), pl.program_id(1)

    @pl.when(j == 0)
    def _():
        x = x_ref[...]
        hm_ref[...] = _rms_rows(x, g_ref[...]).astype(BF16)
        o_ref[...] = x

    @pl.when((i == 0) & (j == 0))
    def _():
        xs = xs_ref[...]
        hms_ref[...] = _rms_rows(xs, g_ref[...]).astype(BF16)
        os_ref[...] = xs

    wu = wu_ref[0]
    wd = wd_ref[...].astype(BF16)

    def up_down(hm):
        h = jnp.dot(hm, wu, preferred_element_type=F32)
        h = jnp.square(jnp.maximum(h, 0.0)).astype(BF16)
        return jnp.dot(h, wd, preferred_element_type=F32)

    o_ref[...] += up_down(hm_ref[...])

    @pl.when(i == 0)
    def _():
        os_ref[...] += up_down(hms_ref[...])


def _mlp(x, xs, g, wu_tiles, wd, *, tm):
    m, d = x.shape
    ms = xs.shape[0]
    n_tiles, _, tf = wu_tiles.shape
    return pl.pallas_call(
        _mlp_kernel,
        grid=(m // tm, n_tiles),
        in_specs=[
            pl.BlockSpec((tm, d), lambda i, j: (i, 0), pipeline_mode=pl.Buffered(1)),
            pl.BlockSpec((ms, d), lambda i, j: (0, 0)),
            pl.BlockSpec((1, d), lambda i, j: (0, 0)),
            pl.BlockSpec((1, d, tf), lambda i, j: (j, 0, 0)),
            pl.BlockSpec((tf, d), lambda i, j: (j, 0)),
        ],
        out_specs=[
            pl.BlockSpec((tm, d), lambda i, j: (i, 0)),
            pl.BlockSpec((ms, d), lambda i, j: (0, 0)),
        ],
        out_shape=[jax.ShapeDtypeStruct((m, d), F32), jax.ShapeDtypeStruct((ms, d), F32)],
        scratch_shapes=[pltpu.VMEM((tm, d), BF16), pltpu.VMEM((ms, d), BF16)],
        compiler_params=pltpu.CompilerParams(**_GRID_PARAMS),
        name="mlp",
    )(x, xs, g.reshape(1, d), wu_tiles, wd)


def _rotate_half(x):
    return jnp.concatenate([x[:, HALF:], x[:, :HALF]], axis=-1)


def _norm_rope(x, g, cos2, sin2):
    xn = _rms_rows(x, g)
    return xn * cos2 + _rotate_half(xn) * sin2


def _layernorm_swish(c, g, b):
    mu = jnp.mean(c, axis=-1, keepdims=True)
    xc = c - mu
    y = xc * lax.rsqrt(jnp.mean(xc * xc, axis=-1, keepdims=True) + EPS)
    y = y * g + b
    return y * jax.nn.sigmoid(y)


def _conv_taps_by_row_phase():
    groups = [[] for _ in range(SUBLANES)]
    for j in range(CONV_WIDTH):
        e = j + CONV_PAD - CONV_STATE
        groups[e % SUBLANES].append((j, e // SUBLANES))
    return groups


def _prompt_mixer_kernel(sinks_ref, z_ref, cos_ref, sin_ref, qg_ref, kg_ref, cw_ref, cb_ref,
                         lg_ref, lb_ref, mix_ref, nk_ref, nv_ref, nc_ref,
                         klo_ref, khi_ref, vt_ref, ubuf_ref):
    n = pl.program_id(1)
    nb = pl.num_programs(1)
    n_q_slabs = ATTN_WIDTH // LANES
    n_k_slabs = KV_DIM // LANES

    @pl.when(n == 0)
    def _():
        klo_ref[:, 0:WINDOW, :] = jnp.zeros((N_KV_HEADS, WINDOW, LANES), BF16)
        khi_ref[:, 0:WINDOW, :] = jnp.zeros((N_KV_HEADS, WINDOW, LANES), BF16)
        vt_ref[:, 0:HEAD_DIM, 0:WINDOW] = jnp.zeros((N_KV_HEADS, HEAD_DIM, WINDOW), BF16)
        vt_ref[:, HEAD_DIM:, :] = jnp.ones((N_KV_HEADS, VT_ROWS - HEAD_DIM, 2 * WINDOW), BF16)
        ubuf_ref[0:CONV_PAD, :] = jnp.zeros((CONV_PAD, CONV_CH), F32)

    @pl.when(n > 0)
    def _():
        klo_ref[:, 0:WINDOW, :] = klo_ref[:, WINDOW:2 * WINDOW, :]
        khi_ref[:, 0:WINDOW, :] = khi_ref[:, WINDOW:2 * WINDOW, :]
        vt_ref[:, 0:HEAD_DIM, 0:WINDOW] = vt_ref[:, 0:HEAD_DIM, WINDOW:2 * WINDOW]
        ubuf_ref[0:CONV_PAD, :] = ubuf_ref[WINDOW:WINDOW + CONV_PAD, :]

    u = z_ref[:, VAL_OFF:VAL_OFF + CONV_CH] * jax.nn.sigmoid(z_ref[:, GATE_OFF:GATE_OFF + CONV_CH])
    ubuf_ref[CONV_PAD:CONV_PAD + WINDOW, :] = u

    lane = lax.broadcasted_iota(jnp.int32, (WINDOW, LANES), 1)
    first_head = lane < HEAD_DIM
    first_half = (lane & HALF) == 0

    n_slabs = n_q_slabs + n_k_slabs
    x = jnp.concatenate([z_ref[:, m * LANES:(m + 1) * LANES] for m in range(n_slabs)], axis=0)
    sq = x * x
    sq_hi = sq.astype(BF16)
    sq_lo = (sq - sq_hi.astype(F32)).astype(BF16)
    er = lax.broadcasted_iota(jnp.int32, (LANES, LANES), 0)
    ec = lax.broadcasted_iota(jnp.int32, (LANES, LANES), 1)
    head_mean = jnp.where((er < HEAD_DIM) == (ec < HEAD_DIM), 1.0 / HEAD_DIM, 0.0).astype(BF16)
    ms = (jnp.dot(sq_hi, head_mean, preferred_element_type=F32)
          + jnp.dot(sq_lo, head_mean, preferred_element_type=F32))
    xn = x * lax.rsqrt(ms + EPS)

    cos4 = cos_ref[...]
    sin4 = sin_ref[...]

    def rope(xs):
        rot = jnp.where(first_half, pltpu.roll(xs, LANES - HALF, 1), pltpu.roll(xs, HALF, 1))
        return xs * cos4 + rot * sin4

    q_slabs = [rope(xn[m * WINDOW:(m + 1) * WINDOW] * qg_ref[...]).astype(BF16) for m in range(n_q_slabs)]

    zero = jnp.zeros((WINDOW, LANES), F32)
    for ks in range(n_k_slabs):
        r0 = (n_q_slabs + ks) * WINDOW
        kr = rope(xn[r0:r0 + WINDOW] * kg_ref[...])
        nk_ref[0, :, ks * LANES:(ks + 1) * LANES] = kr
        kr_swapped = pltpu.roll(kr, HEAD_DIM, 1)
        klo_ref[2 * ks, WINDOW:2 * WINDOW, :] = jnp.where(first_head, kr, zero).astype(BF16)
        khi_ref[2 * ks, WINDOW:2 * WINDOW, :] = jnp.where(first_head, zero, kr_swapped).astype(BF16)
        klo_ref[2 * ks + 1, WINDOW:2 * WINDOW, :] = jnp.where(first_head, kr_swapped, zero).astype(BF16)
        khi_ref[2 * ks + 1, WINDOW:2 * WINDOW, :] = jnp.where(first_head, zero, kr).astype(BF16)

        vs = z_ref[:, V_OFF + ks * LANES:V_OFF + (ks + 1) * LANES]
        nv_ref[0, :, ks * LANES:(ks + 1) * LANES] = vs
        vs_t = vs.T.astype(BF16)
        vt_ref[2 * ks, 0:HEAD_DIM, WINDOW:2 * WINDOW] = vs_t[0:HEAD_DIM]
        vt_ref[2 * ks + 1, 0:HEAD_DIM, WINDOW:2 * WINDOW] = vs_t[HEAD_DIM:2 * HEAD_DIM]

    key = lax.broadcasted_iota(jnp.int32, (2 * WINDOW, WINDOW), 0)
    qry = lax.broadcasted_iota(jnp.int32, (2 * WINDOW, WINDOW), 1)
    diff = qry + WINDOW - key
    band = (diff >= 0) & (diff <= jnp.where(n > 0, WINDOW - 1, qry))
    bias = jnp.where(band, 0.0, NEG)
    bias2 = jnp.concatenate([bias, bias], axis=1)

    def conv_chunk(c):
        cl = slice(c * LANES, (c + 1) * LANES)
        y = cb_ref[:, cl]
        for phase, taps in enumerate(_conv_taps_by_row_phase()):
            rows_needed = WINDOW + (SUBLANES if phase else 0)
            part = None
            for j, tile in taps:
                term = cw_ref[j:j + 1, cl] * ubuf_ref[tile * SUBLANES:tile * SUBLANES + rows_needed, cl]
                part = term if part is None else part + term
            if phase:
                part = pltpu.roll(part, rows_needed - phase, 0)[0:WINDOW]
            y = y + part
        return y

    conv_per_head = CONV_CH // LANES // N_KV_HEADS
    y_chunks = []
    for h in range(N_KV_HEADS):
        q2 = jnp.concatenate([q_slabs[2 * h], q_slabs[2 * h + 1]], axis=0)
        normed_t = []
        for second, k_ref in ((0, klo_ref), (1, khi_ref)):
            s_t = lax.dot_general(k_ref[h], q2, (((1,), (1,)), ((), ())), preferred_element_type=F32)
            s_t = s_t + bias2
            sink_row = jnp.concatenate(
                [jnp.full((1, WINDOW), sinks_ref[GROUP * h + 2 * r + second], F32) for r in range(2)], axis=1)
            m = jnp.maximum(jnp.max(s_t, axis=0, keepdims=True), sink_row)
            p_t = jnp.exp(s_t - m).astype(BF16)
            o_t = jnp.dot(vt_ref[h], p_t, preferred_element_type=F32)
            den = o_t[HEAD_DIM:HEAD_DIM + 1] + jnp.exp(sink_row - m)
            normed_t.append(o_t[0:HEAD_DIM] * (1.0 / den))
        for r in range(2):
            a_t = jnp.concatenate([normed_t[0][:, r * WINDOW:(r + 1) * WINDOW],
                                   normed_t[1][:, r * WINDOW:(r + 1) * WINDOW]], axis=0)
            m_out = 2 * h + r
            mix_ref[:, m_out * LANES:(m_out + 1) * LANES] = a_t.T.astype(BF16)
        y_chunks += [conv_chunk(conv_per_head * h + i) for i in range(conv_per_head)]

    conv = jnp.concatenate(y_chunks, axis=-1)
    mix_ref[:, ATTN_WIDTH:] = _layernorm_swish(conv, lg_ref[...], lb_ref[...]).astype(BF16)

    @pl.when(n == nb - 1)
    def _():
        nc_ref[0] = ubuf_ref[CONV_PAD + WINDOW - CONV_STATE:CONV_PAD + WINDOW, :]


def _prompt_mixer(z, batch, seq, cos4, sin4, qg2, kg2, sinks, cw, cb, lg, lb):
    nb = seq // WINDOW
    n_k_slabs = KV_DIM // LANES
    row = lambda b, n: (b * nb + n, 0)
    const = lambda b, n: (0, 0)
    per_batch = lambda b, n: (b, 0, 0)
    return pl.pallas_call(
        _prompt_mixer_kernel,
        grid=(batch, nb),
        in_specs=[
            pl.BlockSpec(memory_space=pltpu.SMEM),
            pl.BlockSpec((WINDOW, IN_COLS), row),
            pl.BlockSpec((WINDOW, LANES), lambda b, n: (n, 0)),
            pl.BlockSpec((WINDOW, LANES), lambda b, n: (n, 0)),
            pl.BlockSpec((1, LANES), const),
            pl.BlockSpec((1, LANES), const),
            pl.BlockSpec((CONV_WIDTH, CONV_CH), const),
            pl.BlockSpec((1, CONV_CH), const),
            pl.BlockSpec((1, CONV_CH), const),
            pl.BlockSpec((1, CONV_CH), const),
        ],
        out_specs=[
            pl.BlockSpec((WINDOW, ATTN_WIDTH + CONV_CH), row),
            pl.BlockSpec((1, WINDOW, KV_DIM), per_batch),
            pl.BlockSpec((1, WINDOW, KV_DIM), per_batch),
            pl.BlockSpec((1, CONV_STATE, CONV_CH), per_batch),
        ],
        out_shape=[
            jax.ShapeDtypeStruct((batch * seq, ATTN_WIDTH + CONV_CH), BF16),
            jax.ShapeDtypeStruct((batch, WINDOW, KV_DIM), F32),
            jax.ShapeDtypeStruct((batch, WINDOW, KV_DIM), F32),
            jax.ShapeDtypeStruct((batch, CONV_STATE, CONV_CH), F32),
        ],
        scratch_shapes=[
            pltpu.VMEM((N_KV_HEADS, 2 * WINDOW, LANES), BF16),
            pltpu.VMEM((N_KV_HEADS, 2 * WINDOW, LANES), BF16),
            pltpu.VMEM((N_KV_HEADS, VT_ROWS, 2 * WINDOW), BF16),
            pltpu.VMEM((CONV_PAD + WINDOW, CONV_CH), F32),
        ],
        compiler_params=pltpu.CompilerParams(
            dimension_semantics=("arbitrary", "arbitrary"), vmem_limit_bytes=VMEM_LIMIT_BYTES),
        name="prompt_mixer",
    )(sinks, z, cos4, sin4, qg2, kg2, cw, cb, lg, lb)


def _sample_mixer_kernel(sinks_ref, q_ref, k_ref, z_ref, ck_ref, cv_ref, st_ref, cos_ref, sin_ref,
                         qg_ref, kg_ref, cw_ref, cb_ref, lg_ref, lb_ref,
                         a_ref, c_ref, nk_ref, nv_ref, nc_ref, knew_ref):
    bs = z_ref.shape[0]
    cos2 = cos_ref[...]
    sin2 = sin_ref[...]

    qn = _norm_rope(q_ref[...], qg_ref[...], cos2, sin2).astype(BF16)
    tile_r = lax.broadcasted_iota(jnp.int32, (HEAD_DIM, KV_DIM), 0)
    tile_c = lax.broadcasted_iota(jnp.int32, (HEAD_DIM, KV_DIM), 1)
    tile = jnp.where((tile_c & (HEAD_DIM - 1)) == tile_r, 1.0, 0.0).astype(BF16)
    qt = jnp.dot(qn, tile, preferred_element_type=F32)
    q_row = lax.broadcasted_iota(jnp.int32, (bs * N_HEADS, KV_DIM), 0)
    q_col = lax.broadcasted_iota(jnp.int32, (bs * N_HEADS, KV_DIM), 1)
    own = ((q_row & (N_HEADS - 1)) // GROUP) == (q_col // HEAD_DIM)
    qexp = jnp.where(own, qt, 0.0)

    kn = _norm_rope(k_ref[...], kg_ref[...], cos2, sin2)
    for b in range(bs):
        for h in range(N_KV_HEADS):
            r = b * N_KV_HEADS + h
            knew_ref[b:b + 1, h * HEAD_DIM:(h + 1) * HEAD_DIM] = kn[r:r + 1, :]
    knew = knew_ref[...]
    vnew = z_ref[:, V_OFF:V_OFF + KV_DIM]
    knew_r = knew.astype(BF16).astype(F32)
    vnew_r = vnew.astype(BF16).astype(F32)

    sink_col = jnp.concatenate([jnp.full((1, 1), sinks_ref[hh], F32) for hh in range(N_HEADS)], axis=0)
    key_idx = lax.broadcasted_iota(jnp.int32, (N_HEADS, WINDOW), 1)
    key_ok = key_idx >= 1
    o_row = lax.broadcasted_iota(jnp.int32, (N_HEADS, KV_DIM), 0)
    o_col = lax.broadcasted_iota(jnp.int32, (N_HEADS, KV_DIM), 1)
    o_own = (o_row // GROUP) == (o_col // HEAD_DIM)
    fold_r = lax.broadcasted_iota(jnp.int32, (KV_DIM, HEAD_DIM), 0)
    fold_c = lax.broadcasted_iota(jnp.int32, (KV_DIM, HEAD_DIM), 1)
    fold = jnp.where((fold_r & (HEAD_DIM - 1)) == fold_c, 1.0, 0.0).astype(BF16)

    for b in range(bs):
        qb = qexp[b * N_HEADS:(b + 1) * N_HEADS]
        kc = ck_ref[b]
        vc = cv_ref[b]
        s = lax.dot_general(qb.astype(BF16), kc.astype(BF16), (((1,), (1,)), ((), ())),
                            preferred_element_type=F32)
        s = jnp.where(key_ok, s * (HEAD_DIM ** -0.5), NEG)
        s_new = jnp.sum(qb * knew_r[b:b + 1, :], axis=-1, keepdims=True) * (HEAD_DIM ** -0.5)
        m = jnp.maximum(jnp.maximum(jnp.max(s, axis=-1, keepdims=True), s_new), sink_col)
        p = jnp.exp(s - m)
        p_new = jnp.exp(s_new - m)
        den = jnp.sum(p, axis=-1, keepdims=True) + p_new + jnp.exp(sink_col - m)
        p = p / den
        p_new = (p_new / den).astype(BF16).astype(F32)
        o = jnp.dot(p.astype(BF16), vc.astype(BF16), preferred_element_type=F32)
        o = o + p_new * vnew_r[b:b + 1, :]
        o = jnp.where(o_own, o, 0.0).astype(BF16)
        a_ref[b * N_HEADS:(b + 1) * N_HEADS, :] = jnp.dot(o, fold, preferred_element_type=F32)

        nk_ref[b, 0:WINDOW - 1, :] = ck_ref[b, 1:WINDOW, :]
        nk_ref[b, WINDOW - 1:WINDOW, :] = knew[b:b + 1, :]
        nv_ref[b, 0:WINDOW - 1, :] = cv_ref[b, 1:WINDOW, :]
        nv_ref[b, WINDOW - 1:WINDOW, :] = vnew[b:b + 1, :]

    u = z_ref[:, VAL_OFF:VAL_OFF + CONV_CH] * jax.nn.sigmoid(z_ref[:, GATE_OFF:GATE_OFF + CONV_CH])
    w_hist = cw_ref[0:CONV_STATE, :]
    w_last = cw_ref[CONV_STATE:CONV_WIDTH, :]
    for b in range(bs):
        st = st_ref[b]
        yb = jnp.sum(st * w_hist, axis=0, keepdims=True) + u[b:b + 1, :] * w_last + cb_ref[...]
        c_ref[b:b + 1, :] = _layernorm_swish(yb, lg_ref[...], lb_ref[...])
        nc_ref[b, 0:CONV_STATE - 1, :] = st_ref[b, 1:CONV_STATE, :]
        nc_ref[b, CONV_STATE - 1:CONV_STATE, :] = u[b:b + 1, :]


def _sample_mixer(zs, cache_k, cache_v, state, cos2, sin2, qg, kg, sinks, cw, cb, lg, lb, *, bs):
    nsamp = zs.shape[0]
    q_rows = zs[:, :ATTN_WIDTH].reshape(nsamp * N_HEADS, HEAD_DIM)
    k_rows = zs[:, K_OFF:K_OFF + KV_DIM].reshape(nsamp * N_KV_HEADS, HEAD_DIM)
    const = lambda i: (0, 0)
    step = lambda i: (i, 0)
    step3 = lambda i: (i, 0, 0)
    return pl.pallas_call(
        _sample_mixer_kernel,
        grid=(nsamp // bs,),
        in_specs=[
            pl.BlockSpec(memory_space=pltpu.SMEM),
            pl.BlockSpec((bs * N_HEADS, HEAD_DIM), step),
            pl.BlockSpec((bs * N_KV_HEADS, HEAD_DIM), step),
            pl.BlockSpec((bs, IN_COLS), step),
            pl.BlockSpec((bs, WINDOW, KV_DIM), step3),
            pl.BlockSpec((bs, WINDOW, KV_DIM), step3),
            pl.BlockSpec((bs, CONV_STATE, CONV_CH), step3),
            pl.BlockSpec((1, HEAD_DIM), const),
            pl.BlockSpec((1, HEAD_DIM), const),
            pl.BlockSpec((1, HEAD_DIM), const),
            pl.BlockSpec((1, HEAD_DIM), const),
            pl.BlockSpec((CONV_WIDTH, CONV_CH), const),
            pl.BlockSpec((1, CONV_CH), const),
            pl.BlockSpec((1, CONV_CH), const),
            pl.BlockSpec((1, CONV_CH), const),
        ],
        out_specs=[
            pl.BlockSpec((bs * N_HEADS, HEAD_DIM), step),
            pl.BlockSpec((bs, CONV_CH), step),
            pl.BlockSpec((bs, WINDOW, KV_DIM), step3),
            pl.BlockSpec((bs, WINDOW, KV_DIM), step3),
            pl.BlockSpec((bs, CONV_STATE, CONV_CH), step3),
        ],
        out_shape=[
            jax.ShapeDtypeStruct((nsamp * N_HEADS, HEAD_DIM), F32),
            jax.ShapeDtypeStruct((nsamp, CONV_CH), F32),
            jax.ShapeDtypeStruct((nsamp, WINDOW, KV_DIM), F32),
            jax.ShapeDtypeStruct((nsamp, WINDOW, KV_DIM), F32),
            jax.ShapeDtypeStruct((nsamp, CONV_STATE, CONV_CH), F32),
        ],
        scratch_shapes=[pltpu.VMEM((bs, KV_DIM), F32)],
        compiler_params=pltpu.CompilerParams(
            dimension_semantics=("arbitrary",), vmem_limit_bytes=VMEM_LIMIT_BYTES),
        name="sample_mixer",
    )(sinks, q_rows, k_rows, zs, cache_k, cache_v, state, cos2, sin2, qg, kg, cw, cb, lg, lb)


def _rope_tables(pos):
    inv = ROPE_THETA ** (-jnp.arange(HALF, dtype=F32) / HALF)
    ang = pos.astype(F32)[:, None] * inv[None, :]
    cos = jnp.cos(ang)
    sin = jnp.sin(ang)
    return jnp.concatenate([cos, cos], axis=-1), jnp.concatenate([-sin, sin], axis=-1)


def kernel(x_prompt, x_sample, cache_k, cache_v, state_conv, norm_mix_g, w_in, q_norm_g, k_norm_g,
           sinks, conv_w, conv_b, conv_ln_g, conv_ln_b, w_out, norm_mlp_g, w_up, w_down):
    batch, seq, d = x_prompt.shape
    nsamp, dec_seq, _ = x_sample.shape
    depth = w_in.shape[0]
    assert dec_seq == 1 and seq % WINDOW == 0 and cache_k.shape[2] == WINDOW

    cos_p, sin_p = _rope_tables(jnp.arange(seq))
    cos_s, sin_s = _rope_tables(PAST_LEN + jnp.arange(dec_seq))

    xp = x_prompt.reshape(batch * seq, d)
    xs = x_sample.reshape(nsamp * dec_seq, d)
    kp, vp, cp, ksm, vsm, csm = [], [], [], [], [], []
    for l in range(depth):
        qg = q_norm_g[l].reshape(1, HEAD_DIM)
        kg = k_norm_g[l].reshape(1, HEAD_DIM)
        cb = conv_b[l].reshape(1, CONV_CH)
        lg = conv_ln_g[l].reshape(1, CONV_CH)
        lb = conv_ln_b[l].reshape(1, CONV_CH)

        zp, zs = _norm_matmul(xp, xs, norm_mix_g[l], w_in[l], tm=ROW_BLOCK, tn=IN_PROJ_TILE)
        qg2 = jnp.tile(qg * (HEAD_DIM ** -0.5), (1, LANES // HEAD_DIM))
        kg2 = jnp.tile(kg, (1, LANES // HEAD_DIM))
        cos4 = jnp.tile(cos_p, (1, LANES // HEAD_DIM))
        sin4 = jnp.tile(sin_p, (1, LANES // HEAD_DIM))
        mixp, nk, nv, nc = _prompt_mixer(zp, batch, seq, cos4, sin4, qg2, kg2, sinks[l],
                                         conv_w[l], cb, lg, lb)
        kp.append(nk.reshape(batch, WINDOW, N_KV_HEADS, HEAD_DIM))
        vp.append(nv.reshape(batch, WINDOW, N_KV_HEADS, HEAD_DIM))
        cp.append(nc)

        a_s, c_s, nk, nv, nc = _sample_mixer(
            zs, cache_k[l].reshape(nsamp, WINDOW, KV_DIM), cache_v[l].reshape(nsamp, WINDOW, KV_DIM),
            state_conv[l], cos_s, sin_s, qg, kg, sinks[l], conv_w[l], cb, lg, lb, bs=16)
        mixs = jnp.concatenate([a_s.reshape(nsamp, ATTN_WIDTH), c_s], axis=-1).astype(BF16)
        x1, x1s = _residual_matmul(mixp, mixs, w_out[l], xp, xs, tm=ROW_BLOCK, tn=OUT_PROJ_TILE)
        wu_tiles = _cast_tiles(w_up[l], tk=CAST_ROWS, tn=MLP_TILE)
        xp, xs = _mlp(x1, x1s, norm_mlp_g[l], wu_tiles, w_down[l], tm=ROW_BLOCK)
        ksm.append(nk.reshape(nsamp, WINDOW, N_KV_HEADS, HEAD_DIM))
        vsm.append(nv.reshape(nsamp, WINDOW, N_KV_HEADS, HEAD_DIM))
        csm.append(nc)

    return (xp.reshape(batch, seq, d), xs.reshape(nsamp, dec_seq, d),
            jnp.stack(kp), jnp.stack(vp), jnp.stack(cp),
            jnp.stack(ksm), jnp.stack(vsm), jnp.stack(csm))
```

```python
import functools

import jax
import jax.numpy as jnp
from jax import lax
from jax.experimental import pallas as pl
from jax.experimental.pallas import tpu as pltpu

D_MODEL = 2048
ATTN_WIDTH = 1024
CONV_CH = 1024
HEAD_DIM = 64
HALF = HEAD_DIM // 2
N_HEADS = 16
N_KV_HEADS = 4
GROUP = N_HEADS // N_KV_HEADS
KV_DIM = N_KV_HEADS * HEAD_DIM
WINDOW = 128
CONV_WIDTH = 31
CONV_STATE = CONV_WIDTH - 1
D_FF = 4 * D_MODEL
ROPE_THETA = 10000.0
EPS = 1e-6
IN_COLS = ATTN_WIDTH + 2 * KV_DIM + 2 * CONV_CH
NEG = -1e30
PAST_LEN = 16384

K_OFF = ATTN_WIDTH
V_OFF = ATTN_WIDTH + KV_DIM
VAL_OFF = ATTN_WIDTH + 2 * KV_DIM
GATE_OFF = VAL_OFF + CONV_CH

VMEM_LIMIT_BYTES = 56 * 1024 * 1024
SUBLANES = 8
LANES = 128
BF16_ROWS = 16
VT_ROWS = HEAD_DIM + BF16_ROWS
CONV_PAD = 32

MLP_ROW_BLOCK = 1024
MLP_TILE = 512
CAST_ROWS = 256
PROJ_CHUNK = 256

F32 = jnp.float32
BF16 = jnp.bfloat16


def _rms_rows(x, g):
    ms = jnp.mean(x * x, axis=-1, keepdims=True)
    return x * lax.rsqrt(ms + EPS) * g


def _norm_dot(x, g, w):
    return jnp.dot(_rms_rows(x, g).astype(BF16), w, preferred_element_type=F32)


def _cast_kernel(w_ref, o_ref):
    o_ref[...] = w_ref[...].astype(BF16)


def _cast_bf16(w):
    k, n = w.shape
    return pl.pallas_call(
        _cast_kernel,
        grid=(k // CAST_ROWS,),
        in_specs=[pl.BlockSpec((CAST_ROWS, n), lambda r: (r, 0))],
        out_specs=pl.BlockSpec((CAST_ROWS, n), lambda r: (r, 0)),
        out_shape=jax.ShapeDtypeStruct((k, n), BF16),
        compiler_params=pltpu.CompilerParams(
            dimension_semantics=("arbitrary",), vmem_limit_bytes=VMEM_LIMIT_BYTES),
        name="cast_bf16",
    )(w)


def _residual_matmul_kernel(a_ref, w_ref, x_ref, o_ref):
    o_ref[...] = x_ref[...] + jnp.dot(a_ref[...], w_ref[...], preferred_element_type=F32)


def _residual_matmul(a, w, x):
    return pl.pallas_call(
        _residual_matmul_kernel,
        out_shape=jax.ShapeDtypeStruct(x.shape, F32),
        compiler_params=pltpu.CompilerParams(vmem_limit_bytes=VMEM_LIMIT_BYTES),
        name="sample_out_proj",
    )(a, w, x)


def _mlp_kernel(x_ref, xs_ref, g_ref, wu_ref, wd_ref, o_ref, os_ref, hm_ref):
    i, j = pl.program_id(0), pl.program_id(1)
    tm = x_ref.shape[0]

    @pl.when(j == 0)
    def _():
        x = x_ref[...]
        hm_ref[0:tm, :] = _rms_rows(x, g_ref[...]).astype(BF16)
        o_ref[...] = x

    @pl.when((i == 0) & (j == 0))
    def _():
        xs = xs_ref[...]
        hm_ref[tm:, :] = _rms_rows(xs, g_ref[...]).astype(BF16)
        os_ref[...] = xs

    wu = wu_ref[...].astype(BF16)
    wd = wd_ref[...].astype(BF16)

    def up_down(hm):
        h = jnp.dot(hm, wu, preferred_element_type=F32)
        h = jnp.square(jnp.maximum(h, 0.0)).astype(BF16)
        return jnp.dot(h, wd, preferred_element_type=F32)

    @pl.when(i == 0)
    def _():
        y = up_down(hm_ref[...])
        o_ref[...] += y[0:tm]
        os_ref[...] += y[tm:]

    @pl.when(i > 0)
    def _():
        o_ref[...] += up_down(hm_ref[0:tm, :])


def _mlp(x, xs, g, wu, wd):
    m, d = x.shape
    ms = xs.shape[0]
    f = wu.shape[1]
    tm, tf = MLP_ROW_BLOCK, MLP_TILE
    return pl.pallas_call(
        _mlp_kernel,
        grid=(m // tm, f // tf),
        in_specs=[
            pl.BlockSpec((tm, d), lambda i, j: (i, 0), pipeline_mode=pl.Buffered(1)),
            pl.BlockSpec((ms, d), lambda i, j: (0, 0)),
            pl.BlockSpec((1, d), lambda i, j: (0, 0)),
            pl.BlockSpec((d, tf), lambda i, j: (0, j)),
            pl.BlockSpec((tf, d), lambda i, j: (j, 0)),
        ],
        out_specs=[
            pl.BlockSpec((tm, d), lambda i, j: (i, 0)),
            pl.BlockSpec((ms, d), lambda i, j: (0, 0)),
        ],
        out_shape=[jax.ShapeDtypeStruct((m, d), F32), jax.ShapeDtypeStruct((ms, d), F32)],
        scratch_shapes=[pltpu.VMEM((tm + ms, d), BF16)],
        compiler_params=pltpu.CompilerParams(
            dimension_semantics=("arbitrary", "arbitrary"), vmem_limit_bytes=VMEM_LIMIT_BYTES),
        name="mlp",
    )(x, xs, g.reshape(1, d), wu, wd)


def _rotate_half(x):
    return jnp.concatenate([x[:, HALF:], x[:, :HALF]], axis=-1)


def _norm_rope(x, g, cos2, sin2):
    xn = _rms_rows(x, g)
    return xn * cos2 + _rotate_half(xn) * sin2


def _layernorm_swish(c, g, b):
    mu = jnp.mean(c, axis=-1, keepdims=True)
    xc = c - mu
    y = xc * lax.rsqrt(jnp.mean(xc * xc, axis=-1, keepdims=True) + EPS)
    y = y * g + b
    return y * jax.nn.sigmoid(y)


def _conv_taps_by_row_phase():
    groups = [[] for _ in range(SUBLANES)]
    for j in range(CONV_WIDTH):
        e = j + CONV_PAD - CONV_STATE
        groups[e % SUBLANES].append((j, e // SUBLANES))
    return groups


def _mixer_carry(n, klo_ref, khi_ref, vt_ref, ubuf_ref):
    @pl.when(n == 0)
    def _():
        klo_ref[:, 0:WINDOW, :] = jnp.zeros((N_KV_HEADS, WINDOW, LANES), BF16)
        khi_ref[:, 0:WINDOW, :] = jnp.zeros((N_KV_HEADS, WINDOW, LANES), BF16)
        vt_ref[:, 0:HEAD_DIM, 0:WINDOW] = jnp.zeros((N_KV_HEADS, HEAD_DIM, WINDOW), BF16)
        vt_ref[:, HEAD_DIM:, :] = jnp.ones((N_KV_HEADS, VT_ROWS - HEAD_DIM, 2 * WINDOW), BF16)
        ubuf_ref[0:CONV_PAD, :] = jnp.zeros((CONV_PAD, CONV_CH), F32)

    @pl.when(n > 0)
    def _():
        klo_ref[:, 0:WINDOW, :] = klo_ref[:, WINDOW:2 * WINDOW, :]
        khi_ref[:, 0:WINDOW, :] = khi_ref[:, WINDOW:2 * WINDOW, :]
        vt_ref[:, 0:HEAD_DIM, 0:WINDOW] = vt_ref[:, 0:HEAD_DIM, WINDOW:2 * WINDOW]
        ubuf_ref[0:CONV_PAD, :] = ubuf_ref[WINDOW:WINDOW + CONV_PAD, :]


def _mixer_block(n, z_ref, mix_ref, kr_ref, sinks_ref, cos_ref, sin_ref, qg_ref, kg_ref, cw_ref, cb_ref,
                 lg_ref, lb_ref, klo_ref, khi_ref, vt_ref, ubuf_ref, side_jobs=()):
    n_q_slabs = ATTN_WIDTH // LANES
    n_k_slabs = KV_DIM // LANES
    side = iter(side_jobs)

    def tick(k=1):
        for _ in range(k):
            job = next(side, None)
            if job is not None:
                job()

    u = z_ref[:, VAL_OFF:VAL_OFF + CONV_CH] * jax.nn.sigmoid(z_ref[:, GATE_OFF:GATE_OFF + CONV_CH])
    ubuf_ref[CONV_PAD:CONV_PAD + WINDOW, :] = u
    tick(2)

    lane = lax.broadcasted_iota(jnp.int32, (WINDOW, LANES), 1)
    first_head = lane < HEAD_DIM
    first_half = (lane & HALF) == 0

    n_slabs = n_q_slabs + n_k_slabs
    x = jnp.concatenate([z_ref[:, m * LANES:(m + 1) * LANES] for m in range(n_slabs)], axis=0)
    sq = x * x
    sq_hi = sq.astype(BF16)
    sq_lo = (sq - sq_hi.astype(F32)).astype(BF16)
    er = lax.broadcasted_iota(jnp.int32, (LANES, LANES), 0)
    ec = lax.broadcasted_iota(jnp.int32, (LANES, LANES), 1)
    head_mean = jnp.where((er < HEAD_DIM) == (ec < HEAD_DIM), 1.0 / HEAD_DIM, 0.0).astype(BF16)
    ms = (jnp.dot(sq_hi, head_mean, preferred_element_type=F32)
          + jnp.dot(sq_lo, head_mean, preferred_element_type=F32))
    xn = x * lax.rsqrt(ms + EPS)
    tick(2)

    cos4 = cos_ref[...]
    sin4 = sin_ref[...]

    def rope(xs):
        rot = jnp.where(first_half, pltpu.roll(xs, LANES - HALF, 1), pltpu.roll(xs, HALF, 1))
        return xs * cos4 + rot * sin4

    q_slabs = [rope(xn[m * WINDOW:(m + 1) * WINDOW] * qg_ref[...]).astype(BF16) for m in range(n_q_slabs)]
    tick(2)

    zero = jnp.zeros((WINDOW, LANES), F32)
    for ks in range(n_k_slabs):
        r0 = (n_q_slabs + ks) * WINDOW
        kr = rope(xn[r0:r0 + WINDOW] * kg_ref[...])
        kr_ref[:, ks * LANES:(ks + 1) * LANES] = kr
        kr_swapped = pltpu.roll(kr, HEAD_DIM, 1)
        klo_ref[2 * ks, WINDOW:2 * WINDOW, :] = jnp.where(first_head, kr, zero).astype(BF16)
        khi_ref[2 * ks, WINDOW:2 * WINDOW, :] = jnp.where(first_head, zero, kr_swapped).astype(BF16)
        klo_ref[2 * ks + 1, WINDOW:2 * WINDOW, :] = jnp.where(first_head, kr_swapped, zero).astype(BF16)
        khi_ref[2 * ks + 1, WINDOW:2 * WINDOW, :] = jnp.where(first_head, zero, kr).astype(BF16)

        vs_t = z_ref[:, V_OFF + ks * LANES:V_OFF + (ks + 1) * LANES].T.astype(BF16)
        vt_ref[2 * ks, 0:HEAD_DIM, WINDOW:2 * WINDOW] = vs_t[0:HEAD_DIM]
        vt_ref[2 * ks + 1, 0:HEAD_DIM, WINDOW:2 * WINDOW] = vs_t[HEAD_DIM:2 * HEAD_DIM]
        tick()

    key = lax.broadcasted_iota(jnp.int32, (2 * WINDOW, WINDOW), 0)
    qry = lax.broadcasted_iota(jnp.int32, (2 * WINDOW, WINDOW), 1)
    diff = qry + WINDOW - key
    band = (diff >= 0) & (diff <= jnp.where(n > 0, WINDOW - 1, qry))
    bias = jnp.where(band, 0.0, NEG)
    bias2 = jnp.concatenate([bias, bias], axis=1)

    def conv_chunk(c):
        cl = slice(c * LANES, (c + 1) * LANES)
        y = cb_ref[:, cl]
        for phase, taps in enumerate(_conv_taps_by_row_phase()):
            rows_needed = WINDOW + (SUBLANES if phase else 0)
            part = None
            for j, tile in taps:
                term = cw_ref[j:j + 1, cl] * ubuf_ref[tile * SUBLANES:tile * SUBLANES + rows_needed, cl]
                part = term if part is None else part + term
            if phase:
                part = pltpu.roll(part, rows_needed - phase, 0)[0:WINDOW]
            y = y + part
        return y

    conv_per_head = CONV_CH // LANES // N_KV_HEADS
    y_chunks = []
    for h in range(N_KV_HEADS):
        q2 = jnp.concatenate([q_slabs[2 * h], q_slabs[2 * h + 1]], axis=0)
        normed_t = []
        for second, k_ref in ((0, klo_ref), (1, khi_ref)):
            s_t = lax.dot_general(k_ref[h], q2, (((1,), (1,)), ((), ())), preferred_element_type=F32)
            s_t = s_t + bias2
            sink_row = jnp.concatenate(
                [jnp.full((1, WINDOW), sinks_ref[GROUP * h + 2 * r + second], F32) for r in range(2)], axis=1)
            m = jnp.maximum(jnp.max(s_t, axis=0, keepdims=True), sink_row)
            p_t = jnp.exp(s_t - m).astype(BF16)
            o_t = jnp.dot(vt_ref[h], p_t, preferred_element_type=F32)
            den = o_t[HEAD_DIM:HEAD_DIM + 1] + jnp.exp(sink_row - m)
            normed_t.append(o_t[0:HEAD_DIM] * (1.0 / den))
            tick()
        for r in range(2):
            a_t = jnp.concatenate([normed_t[0][:, r * WINDOW:(r + 1) * WINDOW],
                                   normed_t[1][:, r * WINDOW:(r + 1) * WINDOW]], axis=0)
            m_out = 2 * h + r
            mix_ref[:, m_out * LANES:(m_out + 1) * LANES] = a_t.T.astype(BF16)
        y_chunks += [conv_chunk(conv_per_head * h + i) for i in range(conv_per_head)]
        tick()

    conv = jnp.concatenate(y_chunks, axis=-1)
    mix_ref[:, ATTN_WIDTH:] = _layernorm_swish(conv, lg_ref[...], lb_ref[...]).astype(BF16)
    tick(len(side_jobs))


def _front_kernel(sinks_ref, x_next_ref, x_res_ref, xs_ref, g_ref, win_ref, wout_ref, cos_ref, sin_ref,
                  qg_ref, kg_ref, cw_ref, cb_ref, lg_ref, lb_ref,
                  x1_ref, zs_ref, nk_ref, nv_ref, nc_ref,
                  z_next_ref, z_ref, mix_ref, kr_ref, klo_ref, khi_ref, vt_ref, ubuf_ref, *, nb):
    t = pl.program_id(0)
    drain = pl.num_programs(0) - 1
    n = lax.rem(t, nb)

    @pl.when(t == 0)
    def _():
        z_ref[...] = _norm_dot(x_res_ref[...], g_ref[...], win_ref[...])
        zs_ref[...] = _norm_dot(xs_ref[...], g_ref[...], win_ref[...])
        mix_ref[...] = jnp.zeros(mix_ref.shape, BF16)

    _mixer_carry(n, klo_ref, khi_ref, vt_ref, ubuf_ref)

    xn_next = _rms_rows(x_next_ref[...], g_ref[...]).astype(BF16)
    mix_prev = mix_ref[...]
    jobs = []
    for c in range(IN_COLS // PROJ_CHUNK):
        def in_job(cols=slice(c * PROJ_CHUNK, (c + 1) * PROJ_CHUNK)):
            z_next_ref[:, cols] = jnp.dot(xn_next, win_ref[:, cols], preferred_element_type=F32)
        jobs.append(in_job)
    for c in range(x1_ref.shape[1] // PROJ_CHUNK):
        def out_job(cols=slice(c * PROJ_CHUNK, (c + 1) * PROJ_CHUNK)):
            x1_ref[:, cols] = x_res_ref[:, cols] + jnp.dot(mix_prev, wout_ref[:, cols],
                                                           preferred_element_type=F32)
        jobs.append(out_job)
    _mixer_block(n, z_ref, mix_ref, kr_ref, sinks_ref, cos_ref, sin_ref, qg_ref, kg_ref, cw_ref, cb_ref,
                 lg_ref, lb_ref, klo_ref, khi_ref, vt_ref, ubuf_ref, side_jobs=jobs)

    @pl.when((n == nb - 1) & (t < drain))
    def _():
        nk_ref[0] = kr_ref[...]
        nv_ref[0] = z_ref[:, V_OFF:V_OFF + KV_DIM]
        nc_ref[0] = ubuf_ref[CONV_PAD + WINDOW - CONV_STATE:CONV_PAD + WINDOW, :]

    z_ref[...] = z_next_ref[...]


def _prompt_front(x, xs, g, w_in, w_out, batch, seq, cos4, sin4, qg2, kg2, sinks, cw, cb, lg, lb):
    nb = seq // WINDOW
    steps = batch * nb
    d = x.shape[1]
    ms = xs.shape[0]
    const = lambda t: (0, 0)
    behind = lambda t: (jnp.maximum(t - 1, 0), 0)
    ahead = lambda t: (jnp.minimum(t + 1, steps - 1), 0)
    pos = lambda t: (lax.rem(t, nb), 0)
    per_batch = lambda t: (jnp.minimum(t // nb, batch - 1), 0, 0)
    resident = dict(pipeline_mode=pl.Buffered(1))
    return pl.pallas_call(
        functools.partial(_front_kernel, nb=nb),
        grid=(steps + 1,),
        in_specs=[
            pl.BlockSpec(memory_space=pltpu.SMEM),
            pl.BlockSpec((WINDOW, d), ahead),
            pl.BlockSpec((WINDOW, d), behind),
            pl.BlockSpec((ms, d), const),
            pl.BlockSpec((1, d), const),
            pl.BlockSpec((d, IN_COLS), const, **resident),
            pl.BlockSpec((ATTN_WIDTH + CONV_CH, d), const, **resident),
            pl.BlockSpec((WINDOW, LANES), pos),
            pl.BlockSpec((WINDOW, LANES), pos),
            pl.BlockSpec((1, LANES), const),
            pl.BlockSpec((1, LANES), const),
            pl.BlockSpec((CONV_WIDTH, CONV_CH), const),
            pl.BlockSpec((1, CONV_CH), const),
            pl.BlockSpec((1, CONV_CH), const),
            pl.BlockSpec((1, CONV_CH), const),
        ],
        out_specs=[
            pl.BlockSpec((WINDOW, d), behind),
            pl.BlockSpec((ms, IN_COLS), const),
            pl.BlockSpec((1, WINDOW, KV_DIM), per_batch),
            pl.BlockSpec((1, WINDOW, KV_DIM), per_batch),
            pl.BlockSpec((1, CONV_STATE, CONV_CH), per_batch),
        ],
        out_shape=[
            jax.ShapeDtypeStruct((steps * WINDOW, d), F32),
            jax.ShapeDtypeStruct((ms, IN_COLS), F32),
            jax.ShapeDtypeStruct((batch, WINDOW, KV_DIM), F32),
            jax.ShapeDtypeStruct((batch, WINDOW, KV_DIM), F32),
            jax.ShapeDtypeStruct((batch, CONV_STATE, CONV_CH), F32),
        ],
        scratch_shapes=[
            pltpu.VMEM((WINDOW, IN_COLS), F32),
            pltpu.VMEM((WINDOW, IN_COLS), F32),
            pltpu.VMEM((WINDOW, ATTN_WIDTH + CONV_CH), BF16),
            pltpu.VMEM((WINDOW, KV_DIM), F32),
            pltpu.VMEM((N_KV_HEADS, 2 * WINDOW, LANES), BF16),
            pltpu.VMEM((N_KV_HEADS, 2 * WINDOW, LANES), BF16),
            pltpu.VMEM((N_KV_HEADS, VT_ROWS, 2 * WINDOW), BF16),
            pltpu.VMEM((CONV_PAD + WINDOW, CONV_CH), F32),
        ],
        compiler_params=pltpu.CompilerParams(
            dimension_semantics=("arbitrary",), vmem_limit_bytes=VMEM_LIMIT_BYTES),
        name="prompt_front",
    )(sinks, x, x, xs, g.reshape(1, d), w_in, w_out, cos4, sin4, qg2, kg2, cw, cb, lg, lb)


def _sample_mixer_kernel(sinks_ref, q_ref, k_ref, z_ref, ck_ref, cv_ref, st_ref, cos_ref, sin_ref,
                         qg_ref, kg_ref, cw_ref, cb_ref, lg_ref, lb_ref,
                         a_ref, c_ref, nk_ref, nv_ref, nc_ref, knew_ref):
    bs = z_ref.shape[0]
    cos2 = cos_ref[...]
    sin2 = sin_ref[...]

    qn = _norm_rope(q_ref[...], qg_ref[...], cos2, sin2).astype(BF16)
    tile_r = lax.broadcasted_iota(jnp.int32, (HEAD_DIM, KV_DIM), 0)
    tile_c = lax.broadcasted_iota(jnp.int32, (HEAD_DIM, KV_DIM), 1)
    tile = jnp.where((tile_c & (HEAD_DIM - 1)) == tile_r, 1.0, 0.0).astype(BF16)
    qt = jnp.dot(qn, tile, preferred_element_type=F32)
    q_row = lax.broadcasted_iota(jnp.int32, (bs * N_HEADS, KV_DIM), 0)
    q_col = lax.broadcasted_iota(jnp.int32, (bs * N_HEADS, KV_DIM), 1)
    own = ((q_row & (N_HEADS - 1)) // GROUP) == (q_col // HEAD_DIM)
    qexp = jnp.where(own, qt, 0.0)

    kn = _norm_rope(k_ref[...], kg_ref[...], cos2, sin2)
    for b in range(bs):
        for h in range(N_KV_HEADS):
            r = b * N_KV_HEADS + h
            knew_ref[b:b + 1, h * HEAD_DIM:(h + 1) * HEAD_DIM] = kn[r:r + 1, :]
    knew = knew_ref[...]
    vnew = z_ref[:, V_OFF:V_OFF + KV_DIM]
    knew_r = knew.astype(BF16).astype(F32)
    vnew_r = vnew.astype(BF16).astype(F32)

    sink_col = jnp.concatenate([jnp.full((1, 1), sinks_ref[hh], F32) for hh in range(N_HEADS)], axis=0)
    key_idx = lax.broadcasted_iota(jnp.int32, (N_HEADS, WINDOW), 1)
    key_ok = key_idx >= 1
    o_row = lax.broadcasted_iota(jnp.int32, (N_HEADS, KV_DIM), 0)
    o_col = lax.broadcasted_iota(jnp.int32, (N_HEADS, KV_DIM), 1)
    o_own = (o_row // GROUP) == (o_col // HEAD_DIM)
    fold_r = lax.broadcasted_iota(jnp.int32, (KV_DIM, HEAD_DIM), 0)
    fold_c = lax.broadcasted_iota(jnp.int32, (KV_DIM, HEAD_DIM), 1)
    fold = jnp.where((fold_r & (HEAD_DIM - 1)) == fold_c, 1.0, 0.0).astype(BF16)

    for b in range(bs):
        qb = qexp[b * N_HEADS:(b + 1) * N_HEADS]
        kc = ck_ref[b]
        vc = cv_ref[b]
        s = lax.dot_general(qb.astype(BF16), kc.astype(BF16), (((1,), (1,)), ((), ())),
                            preferred_element_type=F32)
        s = jnp.where(key_ok, s * (HEAD_DIM ** -0.5), NEG)
        s_new = jnp.sum(qb * knew_r[b:b + 1, :], axis=-1, keepdims=True) * (HEAD_DIM ** -0.5)
        m = jnp.maximum(jnp.maximum(jnp.max(s, axis=-1, keepdims=True), s_new), sink_col)
        p = jnp.exp(s - m)
        p_new = jnp.exp(s_new - m)
        den = jnp.sum(p, axis=-1, keepdims=True) + p_new + jnp.exp(sink_col - m)
        p = p / den
        p_new = (p_new / den).astype(BF16).astype(F32)
        o = jnp.dot(p.astype(BF16), vc.astype(BF16), preferred_element_type=F32)
        o = o + p_new * vnew_r[b:b + 1, :]
        o = jnp.where(o_own, o, 0.0).astype(BF16)
        a_ref[b * N_HEADS:(b + 1) * N_HEADS, :] = jnp.dot(o, fold, preferred_element_type=F32)

        nk_ref[b, 0:WINDOW - 1, :] = ck_ref[b, 1:WINDOW, :]
        nk_ref[b, WINDOW - 1:WINDOW, :] = knew[b:b + 1, :]
        nv_ref[b, 0:WINDOW - 1, :] = cv_ref[b, 1:WINDOW, :]
        nv_ref[b, WINDOW - 1:WINDOW, :] = vnew[b:b + 1, :]

    u = z_ref[:, VAL_OFF:VAL_OFF + CONV_CH] * jax.nn.sigmoid(z_ref[:, GATE_OFF:GATE_OFF + CONV_CH])
    w_hist = cw_ref[0:CONV_STATE, :]
    w_last = cw_ref[CONV_STATE:CONV_WIDTH, :]
    for b in range(bs):
        st = st_ref[b]
        yb = jnp.sum(st * w_hist, axis=0, keepdims=True) + u[b:b + 1, :] * w_last + cb_ref[...]
        c_ref[b:b + 1, :] = _layernorm_swish(yb, lg_ref[...], lb_ref[...])
        nc_ref[b, 0:CONV_STATE - 1, :] = st_ref[b, 1:CONV_STATE, :]
        nc_ref[b, CONV_STATE - 1:CONV_STATE, :] = u[b:b + 1, :]


def _sample_mixer(zs, cache_k, cache_v, state, cos2, sin2, qg, kg, sinks, cw, cb, lg, lb, *, bs):
    nsamp = zs.shape[0]
    q_rows = zs[:, :ATTN_WIDTH].reshape(nsamp * N_HEADS, HEAD_DIM)
    k_rows = zs[:, K_OFF:K_OFF + KV_DIM].reshape(nsamp * N_KV_HEADS, HEAD_DIM)
    const = lambda i: (0, 0)
    step = lambda i: (i, 0)
    step3 = lambda i: (i, 0, 0)
    return pl.pallas_call(
        _sample_mixer_kernel,
        grid=(nsamp // bs,),
        in_specs=[
            pl.BlockSpec(memory_space=pltpu.SMEM),
            pl.BlockSpec((bs * N_HEADS, HEAD_DIM), step),
            pl.BlockSpec((bs * N_KV_HEADS, HEAD_DIM), step),
            pl.BlockSpec((bs, IN_COLS), step),
            pl.BlockSpec((bs, WINDOW, KV_DIM), step3),
            pl.BlockSpec((bs, WINDOW, KV_DIM), step3),
            pl.BlockSpec((bs, CONV_STATE, CONV_CH), step3),
            pl.BlockSpec((1, HEAD_DIM), const),
            pl.BlockSpec((1, HEAD_DIM), const),
            pl.BlockSpec((1, HEAD_DIM), const),
            pl.BlockSpec((1, HEAD_DIM), const),
            pl.BlockSpec((CONV_WIDTH, CONV_CH), const),
            pl.BlockSpec((1, CONV_CH), const),
            pl.BlockSpec((1, CONV_CH), const),
            pl.BlockSpec((1, CONV_CH), const),
        ],
        out_specs=[
            pl.BlockSpec((bs * N_HEADS, HEAD_DIM), step),
            pl.BlockSpec((bs, CONV_CH), step),
            pl.BlockSpec((bs, WINDOW, KV_DIM), step3),
            pl.BlockSpec((bs, WINDOW, KV_DIM), step3),
            pl.BlockSpec((bs, CONV_STATE, CONV_CH), step3),
        ],
        out_shape=[
            jax.ShapeDtypeStruct((nsamp * N_HEADS, HEAD_DIM), F32),
            jax.ShapeDtypeStruct((nsamp, CONV_CH), F32),
            jax.ShapeDtypeStruct((nsamp, WINDOW, KV_DIM), F32),
            jax.ShapeDtypeStruct((nsamp, WINDOW, KV_DIM), F32),
            jax.ShapeDtypeStruct((nsamp, CONV_STATE, CONV_CH), F32),
        ],
        scratch_shapes=[pltpu.VMEM((bs, KV_DIM), F32)],
        compiler_params=pltpu.CompilerParams(
            dimension_semantics=("arbitrary",), vmem_limit_bytes=VMEM_LIMIT_BYTES),
        name="sample_mixer",
    )(sinks, q_rows, k_rows, zs, cache_k, cache_v, state, cos2, sin2, qg, kg, cw, cb, lg, lb)


def _rope_tables(pos):
    inv = ROPE_THETA ** (-jnp.arange(HALF, dtype=F32) / HALF)
    ang = pos.astype(F32)[:, None] * inv[None, :]
    cos = jnp.cos(ang)
    sin = jnp.sin(ang)
    return jnp.concatenate([cos, cos], axis=-1), jnp.concatenate([-sin, sin], axis=-1)


def kernel(x_prompt, x_sample, cache_k, cache_v, state_conv, norm_mix_g, w_in, q_norm_g, k_norm_g,
           sinks, conv_w, conv_b, conv_ln_g, conv_ln_b, w_out, norm_mlp_g, w_up, w_down):
    batch, seq, d = x_prompt.shape
    nsamp, dec_seq, _ = x_sample.shape
    depth = w_in.shape[0]
    assert dec_seq == 1 and seq % WINDOW == 0 and cache_k.shape[2] == WINDOW

    cos_p, sin_p = _rope_tables(jnp.arange(seq))
    cos_s, sin_s = _rope_tables(PAST_LEN + jnp.arange(dec_seq))
    cos4 = jnp.tile(cos_p, (1, LANES // HEAD_DIM))
    sin4 = jnp.tile(sin_p, (1, LANES // HEAD_DIM))

    xp = x_prompt.reshape(batch * seq, d)
    xs = x_sample.reshape(nsamp * dec_seq, d)
    kp, vp, cp, ksm, vsm, csm = [], [], [], [], [], []
    for l in range(depth):
        qg = q_norm_g[l].reshape(1, HEAD_DIM)
        kg = k_norm_g[l].reshape(1, HEAD_DIM)
        cb = conv_b[l].reshape(1, CONV_CH)
        lg = conv_ln_g[l].reshape(1, CONV_CH)
        lb = conv_ln_b[l].reshape(1, CONV_CH)
        qg2 = jnp.tile(qg * (HEAD_DIM ** -0.5), (1, LANES // HEAD_DIM))
        kg2 = jnp.tile(kg, (1, LANES // HEAD_DIM))
        w_in_bf = _cast_bf16(w_in[l])
        w_out_bf = _cast_bf16(w_out[l])

        x1, zs, nk, nv, nc = _prompt_front(xp, xs, norm_mix_g[l], w_in_bf, w_out_bf, batch, seq,
                                           cos4, sin4, qg2, kg2, sinks[l], conv_w[l], cb, lg, lb)
        kp.append(nk.reshape(batch, WINDOW, N_KV_HEADS, HEAD_DIM))
        vp.append(nv.reshape(batch, WINDOW, N_KV_HEADS, HEAD_DIM))
        cp.append(nc)

        a_s, c_s, nk, nv, nc = _sample_mixer(
            zs, cache_k[l].reshape(nsamp, WINDOW, KV_DIM), cache_v[l].reshape(nsamp, WINDOW, KV_DIM),
            state_conv[l], cos_s, sin_s, qg, kg, sinks[l], conv_w[l], cb, lg, lb, bs=16)
        mixs = jnp.concatenate([a_s.reshape(nsamp, ATTN_WIDTH), c_s], axis=-1).astype(BF16)
        x1s = _residual_matmul(mixs, w_out_bf, xs)
        ksm.append(nk.reshape(nsamp, WINDOW, N_KV_HEADS, HEAD_DIM))
        vsm.append(nv.reshape(nsamp, WINDOW, N_KV_HEADS, HEAD_DIM))
        csm.append(nc)

        xp, xs = _mlp(x1, x1s, norm_mlp_g[l], w_up[l], w_down[l])

    return (xp.reshape(batch, seq, d), xs.reshape(nsamp, dec_seq, d),
            jnp.stack(kp), jnp.stack(vp), jnp.stack(cp),
            jnp.stack(ksm), jnp.stack(vsm), jnp.stack(csm))
```

```python
import functools

import jax
import jax.numpy as jnp
from jax import lax
from jax.experimental import pallas as pl
from jax.experimental.pallas import tpu as pltpu

D_MODEL = 2048
ATTN_WIDTH = 1024
CONV_CH = 1024
HEAD_DIM = 64
HALF = HEAD_DIM // 2
N_HEADS = 16
N_KV_HEADS = 4
GROUP = N_HEADS // N_KV_HEADS
KV_DIM = N_KV_HEADS * HEAD_DIM
WINDOW = 128
CONV_WIDTH = 31
CONV_STATE = CONV_WIDTH - 1
D_FF = 4 * D_MODEL
ROPE_THETA = 10000.0
EPS = 1e-6
IN_COLS = ATTN_WIDTH + 2 * KV_DIM + 2 * CONV_CH
NEG = -1e30
PAST_LEN = 16384

K_OFF = ATTN_WIDTH
V_OFF = ATTN_WIDTH + KV_DIM
VAL_OFF = ATTN_WIDTH + 2 * KV_DIM
GATE_OFF = VAL_OFF + CONV_CH

VMEM_LIMIT_BYTES = 56 * 1024 * 1024
SUBLANES = 8
LANES = 128
BF16_ROWS = 16
VT_ROWS = HEAD_DIM + BF16_ROWS
CONV_PAD = 32

MLP_ROW_BLOCK = 1024
MLP_TILE = 512
CAST_ROWS = 256
PROJ_CHUNK = 256
SIDE_JOBS_PER_SOFTMAX = 3

F32 = jnp.float32
BF16 = jnp.bfloat16


def _rms_rows(x, g):
    ms = jnp.mean(x * x, axis=-1, keepdims=True)
    return x * lax.rsqrt(ms + EPS) * g


def _norm_dot(x, g, w):
    return jnp.dot(_rms_rows(x, g).astype(BF16), w, preferred_element_type=F32)


def _cast_kernel(a_ref, b_ref, oa_ref, ob_ref):
    oa_ref[...] = a_ref[...].astype(BF16)
    ob_ref[...] = b_ref[...].astype(BF16)


def _cast_pair_bf16(a, b):
    k = a.shape[0]
    assert b.shape[0] == k
    spec = lambda w: pl.BlockSpec((CAST_ROWS, w.shape[1]), lambda r: (r, 0))
    return pl.pallas_call(
        _cast_kernel,
        grid=(k // CAST_ROWS,),
        in_specs=[spec(a), spec(b)],
        out_specs=[spec(a), spec(b)],
        out_shape=[jax.ShapeDtypeStruct(a.shape, BF16), jax.ShapeDtypeStruct(b.shape, BF16)],
        compiler_params=pltpu.CompilerParams(
            dimension_semantics=("arbitrary",), vmem_limit_bytes=VMEM_LIMIT_BYTES),
        name="cast_bf16",
    )(a, b)


def _residual_matmul_kernel(a_ref, w_ref, x_ref, o_ref):
    o_ref[...] = x_ref[...] + jnp.dot(a_ref[...], w_ref[...], preferred_element_type=F32)


def _residual_matmul(a, w, x):
    return pl.pallas_call(
        _residual_matmul_kernel,
        out_shape=jax.ShapeDtypeStruct(x.shape, F32),
        compiler_params=pltpu.CompilerParams(vmem_limit_bytes=VMEM_LIMIT_BYTES),
        name="sample_out_proj",
    )(a, w, x)


def _mlp_kernel(x_ref, xs_ref, g_ref, wu_ref, wd_ref, o_ref, os_ref, hm_ref):
    i, j = pl.program_id(0), pl.program_id(1)
    tm = x_ref.shape[0]

    @pl.when(j == 0)
    def _():
        x = x_ref[...]
        hm_ref[0:tm, :] = _rms_rows(x, g_ref[...]).astype(BF16)
        o_ref[...] = x

    @pl.when((i == 0) & (j == 0))
    def _():
        xs = xs_ref[...]
        hm_ref[tm:, :] = _rms_rows(xs, g_ref[...]).astype(BF16)
        os_ref[...] = xs

    def up_down(hm):
        h = jnp.dot(hm, wu_ref[...].astype(BF16), preferred_element_type=F32)
        h = jnp.square(jnp.maximum(h, 0.0)).astype(BF16)
        return jnp.dot(h, wd_ref[...].astype(BF16), preferred_element_type=F32)

    @pl.when(i == 0)
    def _():
        y = up_down(hm_ref[...])
        o_ref[...] += y[0:tm]
        os_ref[...] += y[tm:]

    @pl.when(i > 0)
    def _():
        o_ref[...] += up_down(hm_ref[0:tm, :])


def _mlp(x, xs, g, wu, wd):
    m, d = x.shape
    ms = xs.shape[0]
    f = wu.shape[1]
    tm, tf = MLP_ROW_BLOCK, MLP_TILE
    return pl.pallas_call(
        _mlp_kernel,
        grid=(m // tm, f // tf),
        in_specs=[
            pl.BlockSpec((tm, d), lambda i, j: (i, 0), pipeline_mode=pl.Buffered(1)),
            pl.BlockSpec((ms, d), lambda i, j: (0, 0)),
            pl.BlockSpec((1, d), lambda i, j: (0, 0)),
            pl.BlockSpec((d, tf), lambda i, j: (0, j)),
            pl.BlockSpec((tf, d), lambda i, j: (j, 0)),
        ],
        out_specs=[
            pl.BlockSpec((tm, d), lambda i, j: (i, 0)),
            pl.BlockSpec((ms, d), lambda i, j: (0, 0)),
        ],
        out_shape=[jax.ShapeDtypeStruct((m, d), F32), jax.ShapeDtypeStruct((ms, d), F32)],
        scratch_shapes=[pltpu.VMEM((tm + ms, d), BF16)],
        compiler_params=pltpu.CompilerParams(
            dimension_semantics=("arbitrary", "arbitrary"), vmem_limit_bytes=VMEM_LIMIT_BYTES),
        name="mlp",
    )(x, xs, g.reshape(1, d), wu, wd)


def _rotate_half(x):
    return jnp.concatenate([x[:, HALF:], x[:, :HALF]], axis=-1)


def _norm_rope(x, g, cos2, sin2):
    xn = _rms_rows(x, g)
    return xn * cos2 + _rotate_half(xn) * sin2


def _layernorm_swish(c, g, b):
    mu = jnp.mean(c, axis=-1, keepdims=True)
    xc = c - mu
    y = xc * lax.rsqrt(jnp.mean(xc * xc, axis=-1, keepdims=True) + EPS)
    y = y * g + b
    return y * jax.nn.sigmoid(y)


def _conv_taps_by_row_phase():
    groups = [[] for _ in range(SUBLANES)]
    for j in range(CONV_WIDTH):
        e = j + CONV_PAD - CONV_STATE
        groups[e % SUBLANES].append((j, e // SUBLANES))
    return groups


def _mixer_carry(n, klo_ref, khi_ref, vt_ref, ubuf_ref):
    @pl.when(n == 0)
    def _():
        klo_ref[:, 0:WINDOW, :] = jnp.zeros((N_KV_HEADS, WINDOW, LANES), BF16)
        khi_ref[:, 0:WINDOW, :] = jnp.zeros((N_KV_HEADS, WINDOW, LANES), BF16)
        vt_ref[:, 0:HEAD_DIM, 0:WINDOW] = jnp.zeros((N_KV_HEADS, HEAD_DIM, WINDOW), BF16)
        vt_ref[:, HEAD_DIM:, :] = jnp.ones((N_KV_HEADS, VT_ROWS - HEAD_DIM, 2 * WINDOW), BF16)
        ubuf_ref[0:CONV_PAD, :] = jnp.zeros((CONV_PAD, CONV_CH), F32)

    @pl.when(n > 0)
    def _():
        klo_ref[:, 0:WINDOW, :] = klo_ref[:, WINDOW:2 * WINDOW, :]
        khi_ref[:, 0:WINDOW, :] = khi_ref[:, WINDOW:2 * WINDOW, :]
        vt_ref[:, 0:HEAD_DIM, 0:WINDOW] = vt_ref[:, 0:HEAD_DIM, WINDOW:2 * WINDOW]
        ubuf_ref[0:CONV_PAD, :] = ubuf_ref[WINDOW:WINDOW + CONV_PAD, :]


def _mixer_block(n, z_ref, mix_ref, kr_ref, sinks_ref, cos_ref, sin_ref, qg_ref, kg_ref, cw_ref, cb_ref,
                 lg_ref, lb_ref, klo_ref, khi_ref, vt_ref, ubuf_ref, side_jobs=()):
    n_q_slabs = ATTN_WIDTH // LANES
    n_k_slabs = KV_DIM // LANES
    side = iter(side_jobs)

    def tick(k=1):
        for _ in range(k):
            job = next(side, None)
            if job is not None:
                job()

    u = z_ref[:, VAL_OFF:VAL_OFF + CONV_CH] * jax.nn.sigmoid(z_ref[:, GATE_OFF:GATE_OFF + CONV_CH])
    ubuf_ref[CONV_PAD:CONV_PAD + WINDOW, :] = u

    lane = lax.broadcasted_iota(jnp.int32, (WINDOW, LANES), 1)
    first_head = lane < HEAD_DIM
    first_half = (lane & HALF) == 0

    n_slabs = n_q_slabs + n_k_slabs
    x = jnp.concatenate([z_ref[:, m * LANES:(m + 1) * LANES] for m in range(n_slabs)], axis=0)
    sq = x * x
    sq_hi = sq.astype(BF16)
    sq_lo = (sq - sq_hi.astype(F32)).astype(BF16)
    er = lax.broadcasted_iota(jnp.int32, (LANES, LANES), 0)
    ec = lax.broadcasted_iota(jnp.int32, (LANES, LANES), 1)
    head_mean = jnp.where((er < HEAD_DIM) == (ec < HEAD_DIM), 1.0 / HEAD_DIM, 0.0).astype(BF16)
    ms = (jnp.dot(sq_hi, head_mean, preferred_element_type=F32)
          + jnp.dot(sq_lo, head_mean, preferred_element_type=F32))
    xn = x * lax.rsqrt(ms + EPS)

    cos4 = cos_ref[...]
    sin4 = sin_ref[...]

    def rope(xs):
        rot = jnp.where(first_half, pltpu.roll(xs, LANES - HALF, 1), pltpu.roll(xs, HALF, 1))
        return xs * cos4 + rot * sin4

    q_slabs = [rope(xn[m * WINDOW:(m + 1) * WINDOW] * qg_ref[...]).astype(BF16) for m in range(n_q_slabs)]

    zero = jnp.zeros((WINDOW, LANES), F32)
    for ks in range(n_k_slabs):
        r0 = (n_q_slabs + ks) * WINDOW
        kr = rope(xn[r0:r0 + WINDOW] * kg_ref[...])
        kr_ref[:, ks * LANES:(ks + 1) * LANES] = kr
        kr_swapped = pltpu.roll(kr, HEAD_DIM, 1)
        klo_ref[2 * ks, WINDOW:2 * WINDOW, :] = jnp.where(first_head, kr, zero).astype(BF16)
        khi_ref[2 * ks, WINDOW:2 * WINDOW, :] = jnp.where(first_head, zero, kr_swapped).astype(BF16)
        klo_ref[2 * ks + 1, WINDOW:2 * WINDOW, :] = jnp.where(first_head, kr_swapped, zero).astype(BF16)
        khi_ref[2 * ks + 1, WINDOW:2 * WINDOW, :] = jnp.where(first_head, zero, kr).astype(BF16)

        vs_t = z_ref[:, V_OFF + ks * LANES:V_OFF + (ks + 1) * LANES].T.astype(BF16)
        vt_ref[2 * ks, 0:HEAD_DIM, WINDOW:2 * WINDOW] = vs_t[0:HEAD_DIM]
        vt_ref[2 * ks + 1, 0:HEAD_DIM, WINDOW:2 * WINDOW] = vs_t[HEAD_DIM:2 * HEAD_DIM]

    key = lax.broadcasted_iota(jnp.int32, (2 * WINDOW, WINDOW), 0)
    qry = lax.broadcasted_iota(jnp.int32, (2 * WINDOW, WINDOW), 1)
    diff = qry + WINDOW - key
    band = (diff >= 0) & (diff <= jnp.where(n > 0, WINDOW - 1, qry))
    bias = jnp.where(band, 0.0, NEG)
    bias2 = jnp.concatenate([bias, bias], axis=1)

    def conv_chunk(c):
        cl = slice(c * LANES, (c + 1) * LANES)
        y = cb_ref[:, cl]
        for phase, taps in enumerate(_conv_taps_by_row_phase()):
            rows_needed = WINDOW + (SUBLANES if phase else 0)
            part = None
            for j, tile in taps:
                term = cw_ref[j:j + 1, cl] * ubuf_ref[tile * SUBLANES:tile * SUBLANES + rows_needed, cl]
                part = term if part is None else part + term
            if phase:
                part = pltpu.roll(part, rows_needed - phase, 0)[0:WINDOW]
            y = y + part
        return y

    conv_per_head = CONV_CH // LANES // N_KV_HEADS
    y_chunks = []
    for h in range(N_KV_HEADS):
        q2 = jnp.concatenate([q_slabs[2 * h], q_slabs[2 * h + 1]], axis=0)
        normed_t = []
        for second, k_ref in ((0, klo_ref), (1, khi_ref)):
            s_t = lax.dot_general(k_ref[h], q2, (((1,), (1,)), ((), ())), preferred_element_type=F32)
            s_t = s_t + bias2
            sink_row = jnp.concatenate(
                [jnp.full((1, WINDOW), sinks_ref[GROUP * h + 2 * r + second], F32) for r in range(2)], axis=1)
            m = jnp.maximum(jnp.max(s_t, axis=0, keepdims=True), sink_row)
            p_t = jnp.exp(s_t - m).astype(BF16)
            o_t = jnp.dot(vt_ref[h], p_t, preferred_element_type=F32)
            den = o_t[HEAD_DIM:HEAD_DIM + 1] + jnp.exp(sink_row - m)
            normed_t.append(o_t[0:HEAD_DIM] * (1.0 / den))
            tick(SIDE_JOBS_PER_SOFTMAX)
        for r in range(2):
            a_t = jnp.concatenate([normed_t[0][:, r * WINDOW:(r + 1) * WINDOW],
                                   normed_t[1][:, r * WINDOW:(r + 1) * WINDOW]], axis=0)
            m_out = 2 * h + r
            mix_ref[:, m_out * LANES:(m_out + 1) * LANES] = a_t.T.astype(BF16)
        y_chunks += [conv_chunk(conv_per_head * h + i) for i in range(conv_per_head)]

    conv = jnp.concatenate(y_chunks, axis=-1)
    mix_ref[:, ATTN_WIDTH:] = _layernorm_swish(conv, lg_ref[...], lb_ref[...]).astype(BF16)
    tick(len(side_jobs))


def _front_kernel(sinks_ref, x_next_ref, x_res_ref, xs_ref, g_ref, win_ref, wout_ref, cos_ref, sin_ref,
                  qg_ref, kg_ref, cw_ref, cb_ref, lg_ref, lb_ref,
                  x1_ref, zs_ref, nk_ref, nv_ref, nc_ref,
                  z_next_ref, z_ref, mix_ref, kr_ref, klo_ref, khi_ref, vt_ref, ubuf_ref, *, nb):
    t = pl.program_id(0)
    drain = pl.num_programs(0) - 1
    n = lax.rem(t, nb)

    @pl.when(t == 0)
    def _():
        z_ref[...] = _norm_dot(x_res_ref[...], g_ref[...], win_ref[...])
        zs_ref[...] = _norm_dot(xs_ref[...], g_ref[...], win_ref[...])
        mix_ref[...] = jnp.zeros(mix_ref.shape, BF16)

    _mixer_carry(n, klo_ref, khi_ref, vt_ref, ubuf_ref)

    xn_next = _rms_rows(x_next_ref[...], g_ref[...]).astype(BF16)
    mix_prev = mix_ref[...]
    jobs = []
    for c in range(IN_COLS // PROJ_CHUNK):
        def in_job(cols=slice(c * PROJ_CHUNK, (c + 1) * PROJ_CHUNK)):
            z_next_ref[:, cols] = jnp.dot(xn_next, win_ref[:, cols], preferred_element_type=F32)
        jobs.append(in_job)
    for c in range(x1_ref.shape[1] // PROJ_CHUNK):
        def out_job(cols=slice(c * PROJ_CHUNK, (c + 1) * PROJ_CHUNK)):
            x1_ref[:, cols] = x_res_ref[:, cols] + jnp.dot(mix_prev, wout_ref[:, cols],
                                                           preferred_element_type=F32)
        jobs.append(out_job)
    _mixer_block(n, z_ref, mix_ref, kr_ref, sinks_ref, cos_ref, sin_ref, qg_ref, kg_ref, cw_ref, cb_ref,
                 lg_ref, lb_ref, klo_ref, khi_ref, vt_ref, ubuf_ref, side_jobs=jobs)

    @pl.when((n == nb - 1) & (t < drain))
    def _():
        nk_ref[0] = kr_ref[...]
        nv_ref[0] = z_ref[:, V_OFF:V_OFF + KV_DIM]
        nc_ref[0] = ubuf_ref[CONV_PAD + WINDOW - CONV_STATE:CONV_PAD + WINDOW, :]

    z_ref[...] = z_next_ref[...]


def _prompt_front(x, xs, g, w_in, w_out, batch, seq, cos4, sin4, qg2, kg2, sinks, cw, cb, lg, lb):
    nb = seq // WINDOW
    steps = batch * nb
    d = x.shape[1]
    ms = xs.shape[0]
    const = lambda t: (0, 0)
    behind = lambda t: (jnp.maximum(t - 1, 0), 0)
    ahead = lambda t: (jnp.minimum(t + 1, steps - 1), 0)
    pos = lambda t: (lax.rem(t, nb), 0)
    per_batch = lambda t: (jnp.minimum(t // nb, batch - 1), 0, 0)
    resident = dict(pipeline_mode=pl.Buffered(1))
    return pl.pallas_call(
        functools.partial(_front_kernel, nb=nb),
        grid=(steps + 1,),
        in_specs=[
            pl.BlockSpec(memory_space=pltpu.SMEM),
            pl.BlockSpec((WINDOW, d), ahead),
            pl.BlockSpec((WINDOW, d), behind),
            pl.BlockSpec((ms, d), const),
            pl.BlockSpec((1, d), const),
            pl.BlockSpec((d, IN_COLS), const, **resident),
            pl.BlockSpec((ATTN_WIDTH + CONV_CH, d), const, **resident),
            pl.BlockSpec((WINDOW, LANES), pos),
            pl.BlockSpec((WINDOW, LANES), pos),
            pl.BlockSpec((1, LANES), const),
            pl.BlockSpec((1, LANES), const),
            pl.BlockSpec((CONV_WIDTH, CONV_CH), const),
            pl.BlockSpec((1, CONV_CH), const),
            pl.BlockSpec((1, CONV_CH), const),
            pl.BlockSpec((1, CONV_CH), const),
        ],
        out_specs=[
            pl.BlockSpec((WINDOW, d), behind),
            pl.BlockSpec((ms, IN_COLS), const),
            pl.BlockSpec((1, WINDOW, KV_DIM), per_batch),
            pl.BlockSpec((1, WINDOW, KV_DIM), per_batch),
            pl.BlockSpec((1, CONV_STATE, CONV_CH), per_batch),
        ],
        out_shape=[
            jax.ShapeDtypeStruct((steps * WINDOW, d), F32),
            jax.ShapeDtypeStruct((ms, IN_COLS), F32),
            jax.ShapeDtypeStruct((batch, WINDOW, KV_DIM), F32),
            jax.ShapeDtypeStruct((batch, WINDOW, KV_DIM), F32),
            jax.ShapeDtypeStruct((batch, CONV_STATE, CONV_CH), F32),
        ],
        scratch_shapes=[
            pltpu.VMEM((WINDOW, IN_COLS), F32),
            pltpu.VMEM((WINDOW, IN_COLS), F32),
            pltpu.VMEM((WINDOW, ATTN_WIDTH + CONV_CH), BF16),
            pltpu.VMEM((WINDOW, KV_DIM), F32),
            pltpu.VMEM((N_KV_HEADS, 2 * WINDOW, LANES), BF16),
            pltpu.VMEM((N_KV_HEADS, 2 * WINDOW, LANES), BF16),
            pltpu.VMEM((N_KV_HEADS, VT_ROWS, 2 * WINDOW), BF16),
            pltpu.VMEM((CONV_PAD + WINDOW, CONV_CH), F32),
        ],
        compiler_params=pltpu.CompilerParams(
            dimension_semantics=("arbitrary",), vmem_limit_bytes=VMEM_LIMIT_BYTES),
        name="prompt_front",
    )(sinks, x, x, xs, g.reshape(1, d), w_in, w_out, cos4, sin4, qg2, kg2, cw, cb, lg, lb)


def _sample_mixer_kernel(sinks_ref, q_ref, k_ref, z_ref, ck_ref, cv_ref, st_ref, cos_ref, sin_ref,
                         qg_ref, kg_ref, cw_ref, cb_ref, lg_ref, lb_ref,
                         a_ref, c_ref, nk_ref, nv_ref, nc_ref, knew_ref):
    bs = z_ref.shape[0]
    cos2 = cos_ref[...]
    sin2 = sin_ref[...]

    qn = _norm_rope(q_ref[...], qg_ref[...], cos2, sin2).astype(BF16)
    tile_r = lax.broadcasted_iota(jnp.int32, (HEAD_DIM, KV_DIM), 0)
    tile_c = lax.broadcasted_iota(jnp.int32, (HEAD_DIM, KV_DIM), 1)
    tile = jnp.where((tile_c & (HEAD_DIM - 1)) == tile_r, 1.0, 0.0).astype(BF16)
    qt = jnp.dot(qn, tile, preferred_element_type=F32)
    q_row = lax.broadcasted_iota(jnp.int32, (bs * N_HEADS, KV_DIM), 0)
    q_col = lax.broadcasted_iota(jnp.int32, (bs * N_HEADS, KV_DIM), 1)
    own = ((q_row & (N_HEADS - 1)) // GROUP) == (q_col // HEAD_DIM)
    qexp = jnp.where(own, qt, 0.0)

    kn = _norm_rope(k_ref[...], kg_ref[...], cos2, sin2)
    for b in range(bs):
        for h in range(N_KV_HEADS):
            r = b * N_KV_HEADS + h
            knew_ref[b:b + 1, h * HEAD_DIM:(h + 1) * HEAD_DIM] = kn[r:r + 1, :]
    knew = knew_ref[...]
    vnew = z_ref[:, V_OFF:V_OFF + KV_DIM]
    knew_r = knew.astype(BF16).astype(F32)
    vnew_r = vnew.astype(BF16).astype(F32)

    sink_col = jnp.concatenate([jnp.full((1, 1), sinks_ref[hh], F32) for hh in range(N_HEADS)], axis=0)
    key_idx = lax.broadcasted_iota(jnp.int32, (N_HEADS, WINDOW), 1)
    key_ok = key_idx >= 1
    o_row = lax.broadcasted_iota(jnp.int32, (N_HEADS, KV_DIM), 0)
    o_col = lax.broadcasted_iota(jnp.int32, (N_HEADS, KV_DIM), 1)
    o_own = (o_row // GROUP) == (o_col // HEAD_DIM)
    fold_r = lax.broadcasted_iota(jnp.int32, (KV_DIM, HEAD_DIM), 0)
    fold_c = lax.broadcasted_iota(jnp.int32, (KV_DIM, HEAD_DIM), 1)
    fold = jnp.where((fold_r & (HEAD_DIM - 1)) == fold_c, 1.0, 0.0).astype(BF16)

    for b in range(bs):
        qb = qexp[b * N_HEADS:(b + 1) * N_HEADS]
        kc = ck_ref[b]
        vc = cv_ref[b]
        s = lax.dot_general(qb.astype(BF16), kc.astype(BF16), (((1,), (1,)), ((), ())),
                            preferred_element_type=F32)
        s = jnp.where(key_ok, s * (HEAD_DIM ** -0.5), NEG)
        s_new = jnp.sum(qb * knew_r[b:b + 1, :], axis=-1, keepdims=True) * (HEAD_DIM ** -0.5)
        m = jnp.maximum(jnp.maximum(jnp.max(s, axis=-1, keepdims=True), s_new), sink_col)
        p = jnp.exp(s - m)
        p_new = jnp.exp(s_new - m)
        den = jnp.sum(p, axis=-1, keepdims=True) + p_new + jnp.exp(sink_col - m)
        p = p / den
        p_new = (p_new / den).astype(BF16).astype(F32)
        o = jnp.dot(p.astype(BF16), vc.astype(BF16), preferred_element_type=F32)
        o = o + p_new * vnew_r[b:b + 1, :]
        o = jnp.where(o_own, o, 0.0).astype(BF16)
        a_ref[b * N_HEADS:(b + 1) * N_HEADS, :] = jnp.dot(o, fold, preferred_element_type=F32)

        nk_ref[b, 0:WINDOW - 1, :] = ck_ref[b, 1:WINDOW, :]
        nk_ref[b, WINDOW - 1:WINDOW, :] = knew[b:b + 1, :]
        nv_ref[b, 0:WINDOW - 1, :] = cv_ref[b, 1:WINDOW, :]
        nv_ref[b, WINDOW - 1:WINDOW, :] = vnew[b:b + 1, :]

    u = z_ref[:, VAL_OFF:VAL_OFF + CONV_CH] * jax.nn.sigmoid(z_ref[:, GATE_OFF:GATE_OFF + CONV_CH])
    w_hist = cw_ref[0:CONV_STATE, :]
    w_last = cw_ref[CONV_STATE:CONV_WIDTH, :]
    for b in range(bs):
        st = st_ref[b]
        yb = jnp.sum(st * w_hist, axis=0, keepdims=True) + u[b:b + 1, :] * w_last + cb_ref[...]
        c_ref[b:b + 1, :] = _layernorm_swish(yb, lg_ref[...], lb_ref[...])
        nc_ref[b, 0:CONV_STATE - 1, :] = st_ref[b, 1:CONV_STATE, :]
        nc_ref[b, CONV_STATE - 1:CONV_STATE, :] = u[b:b + 1, :]


def _sample_mixer(zs, cache_k, cache_v, state, cos2, sin2, qg, kg, sinks, cw, cb, lg, lb, *, bs):
    nsamp = zs.shape[0]
    q_rows = zs[:, :ATTN_WIDTH].reshape(nsamp * N_HEADS, HEAD_DIM)
    k_rows = zs[:, K_OFF:K_OFF + KV_DIM].reshape(nsamp * N_KV_HEADS, HEAD_DIM)
    const = lambda i: (0, 0)
    step = lambda i: (i, 0)
    step3 = lambda i: (i, 0, 0)
    return pl.pallas_call(
        _sample_mixer_kernel,
        grid=(nsamp // bs,),
        in_specs=[
            pl.BlockSpec(memory_space=pltpu.SMEM),
            pl.BlockSpec((bs * N_HEADS, HEAD_DIM), step),
            pl.BlockSpec((bs * N_KV_HEADS, HEAD_DIM), step),
            pl.BlockSpec((bs, IN_COLS), step),
            pl.BlockSpec((bs, WINDOW, KV_DIM), step3),
            pl.BlockSpec((bs, WINDOW, KV_DIM), step3),
            pl.BlockSpec((bs, CONV_STATE, CONV_CH), step3),
            pl.BlockSpec((1, HEAD_DIM), const),
            pl.BlockSpec((1, HEAD_DIM), const),
            pl.BlockSpec((1, HEAD_DIM), const),
            pl.BlockSpec((1, HEAD_DIM), const),
            pl.BlockSpec((CONV_WIDTH, CONV_CH), const),
            pl.BlockSpec((1, CONV_CH), const),
            pl.BlockSpec((1, CONV_CH), const),
            pl.BlockSpec((1, CONV_CH), const),
        ],
        out_specs=[
            pl.BlockSpec((bs * N_HEADS, HEAD_DIM), step),
            pl.BlockSpec((bs, CONV_CH), step),
            pl.BlockSpec((bs, WINDOW, KV_DIM), step3),
            pl.BlockSpec((bs, WINDOW, KV_DIM), step3),
            pl.BlockSpec((bs, CONV_STATE, CONV_CH), step3),
        ],
        out_shape=[
            jax.ShapeDtypeStruct((nsamp * N_HEADS, HEAD_DIM), F32),
            jax.ShapeDtypeStruct((nsamp, CONV_CH), F32),
            jax.ShapeDtypeStruct((nsamp, WINDOW, KV_DIM), F32),
            jax.ShapeDtypeStruct((nsamp, WINDOW, KV_DIM), F32),
            jax.ShapeDtypeStruct((nsamp, CONV_STATE, CONV_CH), F32),
        ],
        scratch_shapes=[pltpu.VMEM((bs, KV_DIM), F32)],
        compiler_params=pltpu.CompilerParams(
            dimension_semantics=("arbitrary",), vmem_limit_bytes=VMEM_LIMIT_BYTES),
        name="sample_mixer",
    )(sinks, q_rows, k_rows, zs, cache_k, cache_v, state, cos2, sin2, qg, kg, cw, cb, lg, lb)


def _rope_tables(pos):
    inv = ROPE_THETA ** (-jnp.arange(HALF, dtype=F32) / HALF)
    ang = pos.astype(F32)[:, None] * inv[None, :]
    cos = jnp.cos(ang)
    sin = jnp.sin(ang)
    return jnp.concatenate([cos, cos], axis=-1), jnp.concatenate([-sin, sin], axis=-1)


def kernel(x_prompt, x_sample, cache_k, cache_v, state_conv, norm_mix_g, w_in, q_norm_g, k_norm_g,
           sinks, conv_w, conv_b, conv_ln_g, conv_ln_b, w_out, norm_mlp_g, w_up, w_down):
    batch, seq, d = x_prompt.shape
    nsamp, dec_seq, _ = x_sample.shape
    depth = w_in.shape[0]
    assert dec_seq == 1 and seq % WINDOW == 0 and cache_k.shape[2] == WINDOW

    cos_p, sin_p = _rope_tables(jnp.arange(seq))
    cos_s, sin_s = _rope_tables(PAST_LEN + jnp.arange(dec_seq))
    cos4 = jnp.tile(cos_p, (1, LANES // HEAD_DIM))
    sin4 = jnp.tile(sin_p, (1, LANES // HEAD_DIM))

    xp = x_prompt.reshape(batch * seq, d)
    xs = x_sample.reshape(nsamp * dec_seq, d)
    kp, vp, cp, ksm, vsm, csm = [], [], [], [], [], []
    for l in range(depth):
        qg = q_norm_g[l].reshape(1, HEAD_DIM)
        kg = k_norm_g[l].reshape(1, HEAD_DIM)
        cb = conv_b[l].reshape(1, CONV_CH)
        lg = conv_ln_g[l].reshape(1, CONV_CH)
        lb = conv_ln_b[l].reshape(1, CONV_CH)
        qg2 = jnp.tile(qg * (HEAD_DIM ** -0.5), (1, LANES // HEAD_DIM))
        kg2 = jnp.tile(kg, (1, LANES // HEAD_DIM))
        w_in_bf, w_out_bf = _cast_pair_bf16(w_in[l], w_out[l])

        x1, zs, nk, nv, nc = _prompt_front(xp, xs, norm_mix_g[l], w_in_bf, w_out_bf, batch, seq,
                                           cos4, sin4, qg2, kg2, sinks[l], conv_w[l], cb, lg, lb)
        kp.append(nk.reshape(batch, WINDOW, N_KV_HEADS, HEAD_DIM))
        vp.append(nv.reshape(batch, WINDOW, N_KV_HEADS, HEAD_DIM))
        cp.append(nc)

        a_s, c_s, nk, nv, nc = _sample_mixer(
            zs, cache_k[l].reshape(nsamp, WINDOW, KV_DIM), cache_v[l].reshape(nsamp, WINDOW, KV_DIM),
            state_conv[l], cos_s, sin_s, qg, kg, sinks[l], conv_w[l], cb, lg, lb, bs=16)
        mixs = jnp.concatenate([a_s.reshape(nsamp, ATTN_WIDTH), c_s], axis=-1).astype(BF16)
        x1s = _residual_matmul(mixs, w_out_bf, xs)
        ksm.append(nk.reshape(nsamp, WINDOW, N_KV_HEADS, HEAD_DIM))
        vsm.append(nv.reshape(nsamp, WINDOW, N_KV_HEADS, HEAD_DIM))
        csm.append(nc)

        xp, xs = _mlp(x1, x1s, norm_mlp_g[l], w_up[l], w_down[l])

    return (xp.reshape(batch, seq, d), xs.reshape(nsamp, dec_seq, d),
            jnp.stack(kp), jnp.stack(vp), jnp.stack(cp),
            jnp.stack(ksm), jnp.stack(vsm), jnp.stack(csm))
```

```python
import functools

import jax
import jax.numpy as jnp
from jax import lax
from jax.experimental import pallas as pl
from jax.experimental.pallas import tpu as pltpu

D_MODEL = 2048
ATTN_WIDTH = 1024
CONV_CH = 1024
HEAD_DIM = 64
HALF = HEAD_DIM // 2
N_HEADS = 16
N_KV_HEADS = 4
GROUP = N_HEADS // N_KV_HEADS
KV_DIM = N_KV_HEADS * HEAD_DIM
WINDOW = 128
CONV_WIDTH = 31
CONV_STATE = CONV_WIDTH - 1
D_FF = 4 * D_MODEL
ROPE_THETA = 10000.0
EPS = 1e-6
IN_COLS = ATTN_WIDTH + 2 * KV_DIM + 2 * CONV_CH
NEG = -1e30
PAST_LEN = 16384

K_OFF = ATTN_WIDTH
V_OFF = ATTN_WIDTH + KV_DIM
VAL_OFF = ATTN_WIDTH + 2 * KV_DIM
GATE_OFF = VAL_OFF + CONV_CH

VMEM_LIMIT_BYTES = 56 * 1024 * 1024
SUBLANES = 8
LANES = 128
BF16_ROWS = 16
VT_ROWS = HEAD_DIM + BF16_ROWS
CONV_PAD = 32

MLP_ROW_BLOCK = 1024
MLP_TILE = 512
CAST_ROWS = 256
SAMPLE_BLOCK = 16
PROJ_CHUNK = 256
SIDE_JOBS_PER_SOFTMAX = 3

F32 = jnp.float32
BF16 = jnp.bfloat16


def _rms_rows(x, g):
    ms = jnp.mean(x * x, axis=-1, keepdims=True)
    return x * lax.rsqrt(ms + EPS) * g


def _norm_dot(x, g, w):
    return jnp.dot(_rms_rows(x, g).astype(BF16), w, preferred_element_type=F32)


def _cast_kernel(a_ref, b_ref, oa_ref, ob_ref):
    oa_ref[...] = a_ref[...].astype(BF16)
    ob_ref[...] = b_ref[...].astype(BF16)


def _cast_pair_bf16(a, b):
    k = a.shape[0]
    assert b.shape[0] == k
    spec = lambda w: pl.BlockSpec((CAST_ROWS, w.shape[1]), lambda r: (r, 0))
    return pl.pallas_call(
        _cast_kernel,
        grid=(k // CAST_ROWS,),
        in_specs=[spec(a), spec(b)],
        out_specs=[spec(a), spec(b)],
        out_shape=[jax.ShapeDtypeStruct(a.shape, BF16), jax.ShapeDtypeStruct(b.shape, BF16)],
        compiler_params=pltpu.CompilerParams(
            dimension_semantics=("arbitrary",), vmem_limit_bytes=VMEM_LIMIT_BYTES),
        name="cast_bf16",
    )(a, b)


def _mlp_kernel(x_ref, xs_ref, g_ref, wu_ref, wd_ref, o_ref, os_ref, hm_ref):
    i, j = pl.program_id(0), pl.program_id(1)
    tm = x_ref.shape[0]

    @pl.when(j == 0)
    def _():
        x = x_ref[...]
        hm_ref[0:tm, :] = _rms_rows(x, g_ref[...]).astype(BF16)
        o_ref[...] = x

    @pl.when((i == 0) & (j == 0))
    def _():
        xs = xs_ref[...]
        hm_ref[tm:, :] = _rms_rows(xs, g_ref[...]).astype(BF16)
        os_ref[...] = xs

    def up_down(hm):
        h = jnp.dot(hm, wu_ref[...].astype(BF16), preferred_element_type=F32)
        h = jnp.square(jnp.maximum(h, 0.0)).astype(BF16)
        return jnp.dot(h, wd_ref[...].astype(BF16), preferred_element_type=F32)

    @pl.when(i == 0)
    def _():
        y = up_down(hm_ref[...])
        o_ref[...] += y[0:tm]
        os_ref[...] += y[tm:]

    @pl.when(i > 0)
    def _():
        o_ref[...] += up_down(hm_ref[0:tm, :])


def _mlp(x, xs, g, wu, wd):
    m, d = x.shape
    ms = xs.shape[0]
    f = wu.shape[1]
    tm, tf = MLP_ROW_BLOCK, MLP_TILE
    return pl.pallas_call(
        _mlp_kernel,
        grid=(m // tm, f // tf),
        in_specs=[
            pl.BlockSpec((tm, d), lambda i, j: (i, 0), pipeline_mode=pl.Buffered(1)),
            pl.BlockSpec((ms, d), lambda i, j: (0, 0)),
            pl.BlockSpec((1, d), lambda i, j: (0, 0)),
            pl.BlockSpec((d, tf), lambda i, j: (0, j)),
            pl.BlockSpec((tf, d), lambda i, j: (j, 0)),
        ],
        out_specs=[
            pl.BlockSpec((tm, d), lambda i, j: (i, 0)),
            pl.BlockSpec((ms, d), lambda i, j: (0, 0)),
        ],
        out_shape=[jax.ShapeDtypeStruct((m, d), F32), jax.ShapeDtypeStruct((ms, d), F32)],
        scratch_shapes=[pltpu.VMEM((tm + ms, d), BF16)],
        compiler_params=pltpu.CompilerParams(
            dimension_semantics=("arbitrary", "arbitrary"), vmem_limit_bytes=VMEM_LIMIT_BYTES),
        name="mlp",
    )(x, xs, g.reshape(1, d), wu, wd)


def _rotate_half(x):
    return jnp.concatenate([x[:, HALF:], x[:, :HALF]], axis=-1)


def _norm_rope(x, g, cos2, sin2):
    xn = _rms_rows(x, g)
    return xn * cos2 + _rotate_half(xn) * sin2


def _layernorm_swish(c, g, b):
    mu = jnp.mean(c, axis=-1, keepdims=True)
    xc = c - mu
    y = xc * lax.rsqrt(jnp.mean(xc * xc, axis=-1, keepdims=True) + EPS)
    y = y * g + b
    return y * jax.nn.sigmoid(y)


def _conv_taps_by_row_phase():
    groups = [[] for _ in range(SUBLANES)]
    for j in range(CONV_WIDTH):
        e = j + CONV_PAD - CONV_STATE
        groups[e % SUBLANES].append((j, e // SUBLANES))
    return groups


def _mixer_carry(n, klo_ref, khi_ref, vt_ref, ubuf_ref):
    @pl.when(n == 0)
    def _():
        klo_ref[:, 0:WINDOW, :] = jnp.zeros((N_KV_HEADS, WINDOW, LANES), BF16)
        khi_ref[:, 0:WINDOW, :] = jnp.zeros((N_KV_HEADS, WINDOW, LANES), BF16)
        vt_ref[:, 0:HEAD_DIM, 0:WINDOW] = jnp.zeros((N_KV_HEADS, HEAD_DIM, WINDOW), BF16)
        vt_ref[:, HEAD_DIM:, :] = jnp.ones((N_KV_HEADS, VT_ROWS - HEAD_DIM, 2 * WINDOW), BF16)
        ubuf_ref[0:CONV_PAD, :] = jnp.zeros((CONV_PAD, CONV_CH), F32)

    @pl.when(n > 0)
    def _():
        klo_ref[:, 0:WINDOW, :] = klo_ref[:, WINDOW:2 * WINDOW, :]
        khi_ref[:, 0:WINDOW, :] = khi_ref[:, WINDOW:2 * WINDOW, :]
        vt_ref[:, 0:HEAD_DIM, 0:WINDOW] = vt_ref[:, 0:HEAD_DIM, WINDOW:2 * WINDOW]
        ubuf_ref[0:CONV_PAD, :] = ubuf_ref[WINDOW:WINDOW + CONV_PAD, :]


def _mixer_block(n, z_ref, mix_ref, kr_ref, sinks_ref, cos_ref, sin_ref, qg_ref, kg_ref, cw_ref, cb_ref,
                 lg_ref, lb_ref, klo_ref, khi_ref, vt_ref, ubuf_ref, side_jobs=()):
    n_q_slabs = ATTN_WIDTH // LANES
    n_k_slabs = KV_DIM // LANES
    side = iter(side_jobs)

    def tick(k=1):
        for _ in range(k):
            job = next(side, None)
            if job is not None:
                job()

    u = z_ref[:, VAL_OFF:VAL_OFF + CONV_CH] * jax.nn.sigmoid(z_ref[:, GATE_OFF:GATE_OFF + CONV_CH])
    ubuf_ref[CONV_PAD:CONV_PAD + WINDOW, :] = u

    lane = lax.broadcasted_iota(jnp.int32, (WINDOW, LANES), 1)
    first_head = lane < HEAD_DIM
    first_half = (lane & HALF) == 0

    n_slabs = n_q_slabs + n_k_slabs
    x = jnp.concatenate([z_ref[:, m * LANES:(m + 1) * LANES] for m in range(n_slabs)], axis=0)
    sq = x * x
    sq_hi = sq.astype(BF16)
    sq_lo = (sq - sq_hi.astype(F32)).astype(BF16)
    er = lax.broadcasted_iota(jnp.int32, (LANES, LANES), 0)
    ec = lax.broadcasted_iota(jnp.int32, (LANES, LANES), 1)
    head_mean = jnp.where((er < HEAD_DIM) == (ec < HEAD_DIM), 1.0 / HEAD_DIM, 0.0).astype(BF16)
    ms = (jnp.dot(sq_hi, head_mean, preferred_element_type=F32)
          + jnp.dot(sq_lo, head_mean, preferred_element_type=F32))
    xn = x * lax.rsqrt(ms + EPS)

    cos4 = cos_ref[...]
    sin4 = sin_ref[...]

    def rope(xs):
        rot = jnp.where(first_half, pltpu.roll(xs, LANES - HALF, 1), pltpu.roll(xs, HALF, 1))
        return xs * cos4 + rot * sin4

    q_slabs = [rope(xn[m * WINDOW:(m + 1) * WINDOW] * qg_ref[...]).astype(BF16) for m in range(n_q_slabs)]

    zero = jnp.zeros((WINDOW, LANES), F32)
    for ks in range(n_k_slabs):
        r0 = (n_q_slabs + ks) * WINDOW
        kr = rope(xn[r0:r0 + WINDOW] * kg_ref[...])
        kr_ref[:, ks * LANES:(ks + 1) * LANES] = kr
        kr_swapped = pltpu.roll(kr, HEAD_DIM, 1)
        klo_ref[2 * ks, WINDOW:2 * WINDOW, :] = jnp.where(first_head, kr, zero).astype(BF16)
        khi_ref[2 * ks, WINDOW:2 * WINDOW, :] = jnp.where(first_head, zero, kr_swapped).astype(BF16)
        klo_ref[2 * ks + 1, WINDOW:2 * WINDOW, :] = jnp.where(first_head, kr_swapped, zero).astype(BF16)
        khi_ref[2 * ks + 1, WINDOW:2 * WINDOW, :] = jnp.where(first_head, zero, kr).astype(BF16)

        vs_t = z_ref[:, V_OFF + ks * LANES:V_OFF + (ks + 1) * LANES].T.astype(BF16)
        vt_ref[2 * ks, 0:HEAD_DIM, WINDOW:2 * WINDOW] = vs_t[0:HEAD_DIM]
        vt_ref[2 * ks + 1, 0:HEAD_DIM, WINDOW:2 * WINDOW] = vs_t[HEAD_DIM:2 * HEAD_DIM]

    key = lax.broadcasted_iota(jnp.int32, (2 * WINDOW, WINDOW), 0)
    qry = lax.broadcasted_iota(jnp.int32, (2 * WINDOW, WINDOW), 1)
    diff = qry + WINDOW - key
    band = (diff >= 0) & (diff <= jnp.where(n > 0, WINDOW - 1, qry))
    bias = jnp.where(band, 0.0, NEG)
    bias2 = jnp.concatenate([bias, bias], axis=1)

    def conv_chunk(c):
        cl = slice(c * LANES, (c + 1) * LANES)
        y = cb_ref[:, cl]
        for phase, taps in enumerate(_conv_taps_by_row_phase()):
            rows_needed = WINDOW + (SUBLANES if phase else 0)
            part = None
            for j, tile in taps:
                term = cw_ref[j:j + 1, cl] * ubuf_ref[tile * SUBLANES:tile * SUBLANES + rows_needed, cl]
                part = term if part is None else part + term
            if phase:
                part = pltpu.roll(part, rows_needed - phase, 0)[0:WINDOW]
            y = y + part
        return y

    conv_per_head = CONV_CH // LANES // N_KV_HEADS
    y_chunks = []
    for h in range(N_KV_HEADS):
        q2 = jnp.concatenate([q_slabs[2 * h], q_slabs[2 * h + 1]], axis=0)
        normed_t = []
        for second, k_ref in ((0, klo_ref), (1, khi_ref)):
            s_t = lax.dot_general(k_ref[h], q2, (((1,), (1,)), ((), ())), preferred_element_type=F32)
            s_t = s_t + bias2
            sink_row = jnp.concatenate(
                [jnp.full((1, WINDOW), sinks_ref[GROUP * h + 2 * r + second], F32) for r in range(2)], axis=1)
            m = jnp.maximum(jnp.max(s_t, axis=0, keepdims=True), sink_row)
            p_t = jnp.exp(s_t - m).astype(BF16)
            o_t = jnp.dot(vt_ref[h], p_t, preferred_element_type=F32)
            den = o_t[HEAD_DIM:HEAD_DIM + 1] + jnp.exp(sink_row - m)
            normed_t.append(o_t[0:HEAD_DIM] * (1.0 / den))
            tick(SIDE_JOBS_PER_SOFTMAX)
        for r in range(2):
            a_t = jnp.concatenate([normed_t[0][:, r * WINDOW:(r + 1) * WINDOW],
                                   normed_t[1][:, r * WINDOW:(r + 1) * WINDOW]], axis=0)
            m_out = 2 * h + r
            mix_ref[:, m_out * LANES:(m_out + 1) * LANES] = a_t.T.astype(BF16)
        y_chunks += [conv_chunk(conv_per_head * h + i) for i in range(conv_per_head)]

    conv = jnp.concatenate(y_chunks, axis=-1)
    mix_ref[:, ATTN_WIDTH:] = _layernorm_swish(conv, lg_ref[...], lb_ref[...]).astype(BF16)
    tick(len(side_jobs))


def _front_kernel(sinks_ref, x_next_ref, x_res_ref, xs_ref, g_ref, win_ref, wout_ref, cos_ref, sin_ref,
                  qg_ref, kg_ref, cw_ref, cb_ref, lg_ref, lb_ref,
                  x1_ref, zs_ref, nk_ref, nv_ref, nc_ref,
                  z_next_ref, z_ref, mix_ref, kr_ref, klo_ref, khi_ref, vt_ref, ubuf_ref, *, nb):
    t = pl.program_id(0)
    drain = pl.num_programs(0) - 1
    n = lax.rem(t, nb)

    @pl.when(t == 0)
    def _():
        z_ref[...] = _norm_dot(x_res_ref[...], g_ref[...], win_ref[...])
        zs_ref[...] = _norm_dot(xs_ref[...], g_ref[...], win_ref[...])
        mix_ref[...] = jnp.zeros(mix_ref.shape, BF16)

    _mixer_carry(n, klo_ref, khi_ref, vt_ref, ubuf_ref)

    xn_next = _rms_rows(x_next_ref[...], g_ref[...]).astype(BF16)
    mix_prev = mix_ref[...]
    jobs = []
    for c in range(IN_COLS // PROJ_CHUNK):
        def in_job(cols=slice(c * PROJ_CHUNK, (c + 1) * PROJ_CHUNK)):
            z_next_ref[:, cols] = jnp.dot(xn_next, win_ref[:, cols], preferred_element_type=F32)
        jobs.append(in_job)
    for c in range(x1_ref.shape[1] // PROJ_CHUNK):
        def out_job(cols=slice(c * PROJ_CHUNK, (c + 1) * PROJ_CHUNK)):
            x1_ref[:, cols] = x_res_ref[:, cols] + jnp.dot(mix_prev, wout_ref[:, cols],
                                                           preferred_element_type=F32)
        jobs.append(out_job)
    _mixer_block(n, z_ref, mix_ref, kr_ref, sinks_ref, cos_ref, sin_ref, qg_ref, kg_ref, cw_ref, cb_ref,
                 lg_ref, lb_ref, klo_ref, khi_ref, vt_ref, ubuf_ref, side_jobs=jobs)

    @pl.when((n == nb - 1) & (t < drain))
    def _():
        nk_ref[0] = kr_ref[...]
        nv_ref[0] = z_ref[:, V_OFF:V_OFF + KV_DIM]
        nc_ref[0] = ubuf_ref[CONV_PAD + WINDOW - CONV_STATE:CONV_PAD + WINDOW, :]

    z_ref[...] = z_next_ref[...]


def _prompt_front(x, xs, g, w_in, w_out, batch, seq, cos4, sin4, qg2, kg2, sinks, cw, cb, lg, lb):
    nb = seq // WINDOW
    steps = batch * nb
    d = x.shape[1]
    ms = xs.shape[0]
    const = lambda t: (0, 0)
    behind = lambda t: (jnp.maximum(t - 1, 0), 0)
    ahead = lambda t: (jnp.minimum(t + 1, steps - 1), 0)
    pos = lambda t: (lax.rem(t, nb), 0)
    per_batch = lambda t: (jnp.minimum(t // nb, batch - 1), 0, 0)
    resident = dict(pipeline_mode=pl.Buffered(1))
    return pl.pallas_call(
        functools.partial(_front_kernel, nb=nb),
        grid=(steps + 1,),
        in_specs=[
            pl.BlockSpec(memory_space=pltpu.SMEM),
            pl.BlockSpec((WINDOW, d), ahead),
            pl.BlockSpec((WINDOW, d), behind),
            pl.BlockSpec((ms, d), const),
            pl.BlockSpec((1, d), const),
            pl.BlockSpec((d, IN_COLS), const, **resident),
            pl.BlockSpec((ATTN_WIDTH + CONV_CH, d), const, **resident),
            pl.BlockSpec((WINDOW, LANES), pos),
            pl.BlockSpec((WINDOW, LANES), pos),
            pl.BlockSpec((1, LANES), const),
            pl.BlockSpec((1, LANES), const),
            pl.BlockSpec((CONV_WIDTH, CONV_CH), const),
            pl.BlockSpec((1, CONV_CH), const),
            pl.BlockSpec((1, CONV_CH), const),
            pl.BlockSpec((1, CONV_CH), const),
        ],
        out_specs=[
            pl.BlockSpec((WINDOW, d), behind),
            pl.BlockSpec((ms, IN_COLS), const),
            pl.BlockSpec((1, WINDOW, KV_DIM), per_batch),
            pl.BlockSpec((1, WINDOW, KV_DIM), per_batch),
            pl.BlockSpec((1, CONV_STATE, CONV_CH), per_batch),
        ],
        out_shape=[
            jax.ShapeDtypeStruct((steps * WINDOW, d), F32),
            jax.ShapeDtypeStruct((ms, IN_COLS), F32),
            jax.ShapeDtypeStruct((batch, WINDOW, KV_DIM), F32),
            jax.ShapeDtypeStruct((batch, WINDOW, KV_DIM), F32),
            jax.ShapeDtypeStruct((batch, CONV_STATE, CONV_CH), F32),
        ],
        scratch_shapes=[
            pltpu.VMEM((WINDOW, IN_COLS), F32),
            pltpu.VMEM((WINDOW, IN_COLS), F32),
            pltpu.VMEM((WINDOW, ATTN_WIDTH + CONV_CH), BF16),
            pltpu.VMEM((WINDOW, KV_DIM), F32),
            pltpu.VMEM((N_KV_HEADS, 2 * WINDOW, LANES), BF16),
            pltpu.VMEM((N_KV_HEADS, 2 * WINDOW, LANES), BF16),
            pltpu.VMEM((N_KV_HEADS, VT_ROWS, 2 * WINDOW), BF16),
            pltpu.VMEM((CONV_PAD + WINDOW, CONV_CH), F32),
        ],
        compiler_params=pltpu.CompilerParams(
            dimension_semantics=("arbitrary",), vmem_limit_bytes=VMEM_LIMIT_BYTES),
        name="prompt_front",
    )(sinks, x, x, xs, g.reshape(1, d), w_in, w_out, cos4, sin4, qg2, kg2, cw, cb, lg, lb)


def _sample_mixer_kernel(sinks_ref, z_ref, xs_ref, ck_ref, cv_ref, st_ref, cos_ref, sin_ref,
                         qg_ref, kg_ref, cw_ref, cb_ref, lg_ref, lb_ref, wout_ref,
                         x1_ref, nk_ref, nv_ref, nc_ref, qexp_ref, knew_ref, a_ref, c_ref):
    bs = z_ref.shape[0]
    cos2 = cos_ref[...]
    sin2 = sin_ref[...]

    q = jnp.concatenate([z_ref[:, h * HEAD_DIM:(h + 1) * HEAD_DIM] for h in range(N_HEADS)], axis=0)
    qn = _norm_rope(q, qg_ref[...], cos2, sin2).astype(BF16)
    tile_r = lax.broadcasted_iota(jnp.int32, (HEAD_DIM, KV_DIM), 0)
    tile_c = lax.broadcasted_iota(jnp.int32, (HEAD_DIM, KV_DIM), 1)
    tile = jnp.where((tile_c & (HEAD_DIM - 1)) == tile_r, 1.0, 0.0).astype(BF16)
    qt = jnp.dot(qn, tile, preferred_element_type=F32)
    q_row = lax.broadcasted_iota(jnp.int32, (N_HEADS * bs, KV_DIM), 0)
    q_col = lax.broadcasted_iota(jnp.int32, (N_HEADS * bs, KV_DIM), 1)
    own = (q_row // (bs * GROUP)) == (q_col // HEAD_DIM)
    qexp = jnp.where(own, qt, 0.0)
    for half in range(KV_DIM // LANES):
        qexp_ref[half] = qexp[:, half * LANES:(half + 1) * LANES]

    k = jnp.concatenate([z_ref[:, K_OFF + h * HEAD_DIM:K_OFF + (h + 1) * HEAD_DIM]
                         for h in range(N_KV_HEADS)], axis=0)
    kn = _norm_rope(k, kg_ref[...], cos2, sin2)
    for h in range(N_KV_HEADS):
        knew_ref[:, h * HEAD_DIM:(h + 1) * HEAD_DIM] = kn[h * bs:(h + 1) * bs]
    knew = knew_ref[...]
    vnew = z_ref[:, V_OFF:V_OFF + KV_DIM]
    knew_r = knew.astype(BF16).astype(F32)
    vnew_r = vnew.astype(BF16).astype(F32)

    sink_col = jnp.concatenate([jnp.full((1, 1), sinks_ref[hh], F32) for hh in range(N_HEADS)], axis=0)
    key_idx = lax.broadcasted_iota(jnp.int32, (N_HEADS, WINDOW), 1)
    key_ok = key_idx >= 1
    o_row = lax.broadcasted_iota(jnp.int32, (N_HEADS, KV_DIM), 0)
    o_col = lax.broadcasted_iota(jnp.int32, (N_HEADS, KV_DIM), 1)
    o_own = (o_row // GROUP) == (o_col // HEAD_DIM)
    fold_r = lax.broadcasted_iota(jnp.int32, (KV_DIM, LANES), 0)
    fold_c = lax.broadcasted_iota(jnp.int32, (KV_DIM, LANES), 1)
    fold = jnp.where((fold_r & (HEAD_DIM - 1)) == fold_c, 1.0, 0.0).astype(BF16)

    u = z_ref[:, VAL_OFF:VAL_OFF + CONV_CH] * jax.nn.sigmoid(z_ref[:, GATE_OFF:GATE_OFF + CONV_CH])
    w_hist = cw_ref[0:CONV_STATE, :]
    w_last = cw_ref[CONV_STATE:CONV_WIDTH, :]

    for b in range(bs):
        heads_of_b = pl.ds(b, N_HEADS, stride=bs)
        qb = jnp.concatenate([qexp_ref[half, heads_of_b, :] for half in range(KV_DIM // LANES)], axis=1)
        kc = ck_ref[b]
        vc = cv_ref[b]
        s = lax.dot_general(qb.astype(BF16), kc.astype(BF16), (((1,), (1,)), ((), ())),
                            preferred_element_type=F32)
        s = jnp.where(key_ok, s * (HEAD_DIM ** -0.5), NEG)
        s_new = jnp.sum(qb * knew_r[b:b + 1, :], axis=-1, keepdims=True) * (HEAD_DIM ** -0.5)
        m = jnp.maximum(jnp.maximum(jnp.max(s, axis=-1, keepdims=True), s_new), sink_col)
        p = jnp.exp(s - m)
        p_new = jnp.exp(s_new - m)
        den = jnp.sum(p, axis=-1, keepdims=True) + p_new + jnp.exp(sink_col - m)
        p = p / den
        p_new = (p_new / den).astype(BF16).astype(F32)
        o = jnp.dot(p.astype(BF16), vc.astype(BF16), preferred_element_type=F32)
        o = o + p_new * vnew_r[b:b + 1, :]
        o = jnp.where(o_own, o, 0.0).astype(BF16)
        a_ref[heads_of_b, :] = jnp.dot(o, fold, preferred_element_type=F32)

        nk_ref[b, 0:WINDOW - 1, :] = ck_ref[b, 1:WINDOW, :]
        nk_ref[b, WINDOW - 1:WINDOW, :] = knew[b:b + 1, :]
        nv_ref[b, 0:WINDOW - 1, :] = cv_ref[b, 1:WINDOW, :]
        nv_ref[b, WINDOW - 1:WINDOW, :] = vnew[b:b + 1, :]

        st = st_ref[b]
        yb = jnp.sum(st * w_hist, axis=0, keepdims=True) + u[b:b + 1, :] * w_last + cb_ref[...]
        c_ref[b:b + 1, :] = _layernorm_swish(yb, lg_ref[...], lb_ref[...])
        nc_ref[b, 0:CONV_STATE - 1, :] = st_ref[b, 1:CONV_STATE, :]
        nc_ref[b, CONV_STATE - 1:CONV_STATE, :] = u[b:b + 1, :]

    x1 = xs_ref[...] + jnp.dot(c_ref[...].astype(BF16), wout_ref[ATTN_WIDTH:, :], preferred_element_type=F32)
    for h in range(N_HEADS):
        x1 = x1 + jnp.dot(a_ref[h * bs:(h + 1) * bs, 0:HEAD_DIM].astype(BF16),
                          wout_ref[h * HEAD_DIM:(h + 1) * HEAD_DIM, :], preferred_element_type=F32)
    x1_ref[...] = x1


def _sample_mixer(zs, xs, cache_k, cache_v, state, cos2, sin2, qg, kg, sinks, cw, cb, lg, lb, w_out, *, bs):
    nsamp, d = xs.shape
    const = lambda i: (0, 0)
    step = lambda i: (i, 0)
    step3 = lambda i: (i, 0, 0)
    return pl.pallas_call(
        _sample_mixer_kernel,
        grid=(nsamp // bs,),
        in_specs=[
            pl.BlockSpec(memory_space=pltpu.SMEM),
            pl.BlockSpec((bs, IN_COLS), step),
            pl.BlockSpec((bs, d), step),
            pl.BlockSpec((bs, WINDOW, KV_DIM), step3),
            pl.BlockSpec((bs, WINDOW, KV_DIM), step3),
            pl.BlockSpec((bs, CONV_STATE, CONV_CH), step3),
            pl.BlockSpec((1, HEAD_DIM), const),
            pl.BlockSpec((1, HEAD_DIM), const),
            pl.BlockSpec((1, HEAD_DIM), const),
            pl.BlockSpec((1, HEAD_DIM), const),
            pl.BlockSpec((CONV_WIDTH, CONV_CH), const),
            pl.BlockSpec((1, CONV_CH), const),
            pl.BlockSpec((1, CONV_CH), const),
            pl.BlockSpec((1, CONV_CH), const),
            pl.BlockSpec((ATTN_WIDTH + CONV_CH, d), const, pipeline_mode=pl.Buffered(1)),
        ],
        out_specs=[
            pl.BlockSpec((bs, d), step),
            pl.BlockSpec((bs, WINDOW, KV_DIM), step3),
            pl.BlockSpec((bs, WINDOW, KV_DIM), step3),
            pl.BlockSpec((bs, CONV_STATE, CONV_CH), step3),
        ],
        out_shape=[
            jax.ShapeDtypeStruct((nsamp, d), F32),
            jax.ShapeDtypeStruct((nsamp, WINDOW, KV_DIM), F32),
            jax.ShapeDtypeStruct((nsamp, WINDOW, KV_DIM), F32),
            jax.ShapeDtypeStruct((nsamp, CONV_STATE, CONV_CH), F32),
        ],
        scratch_shapes=[
            pltpu.VMEM((KV_DIM // LANES, N_HEADS * bs, LANES), F32),
            pltpu.VMEM((bs, KV_DIM), F32),
            pltpu.VMEM((N_HEADS * bs, LANES), F32),
            pltpu.VMEM((bs, CONV_CH), F32),
        ],
        compiler_params=pltpu.CompilerParams(
            dimension_semantics=("arbitrary",), vmem_limit_bytes=VMEM_LIMIT_BYTES),
        name="sample_mixer",
    )(sinks, zs, xs, cache_k, cache_v, state, cos2, sin2, qg, kg, cw, cb, lg, lb, w_out)


def _rope_tables(pos):
    inv = ROPE_THETA ** (-jnp.arange(HALF, dtype=F32) / HALF)
    ang = pos.astype(F32)[:, None] * inv[None, :]
    cos = jnp.cos(ang)
    sin = jnp.sin(ang)
    return jnp.concatenate([cos, cos], axis=-1), jnp.concatenate([-sin, sin], axis=-1)


def kernel(x_prompt, x_sample, cache_k, cache_v, state_conv, norm_mix_g, w_in, q_norm_g, k_norm_g,
           sinks, conv_w, conv_b, conv_ln_g, conv_ln_b, w_out, norm_mlp_g, w_up, w_down):
    batch, seq, d = x_prompt.shape
    nsamp, dec_seq, _ = x_sample.shape
    depth = w_in.shape[0]
    assert dec_seq == 1 and seq % WINDOW == 0 and cache_k.shape[2] == WINDOW

    cos_p, sin_p = _rope_tables(jnp.arange(seq))
    cos_s, sin_s = _rope_tables(PAST_LEN + jnp.arange(dec_seq))
    cos4 = jnp.tile(cos_p, (1, LANES // HEAD_DIM))
    sin4 = jnp.tile(sin_p, (1, LANES // HEAD_DIM))

    xp = x_prompt.reshape(batch * seq, d)
    xs = x_sample.reshape(nsamp * dec_seq, d)
    kp, vp, cp, ksm, vsm, csm = [], [], [], [], [], []
    for l in range(depth):
        qg = q_norm_g[l].reshape(1, HEAD_DIM)
        kg = k_norm_g[l].reshape(1, HEAD_DIM)
        cb = conv_b[l].reshape(1, CONV_CH)
        lg = conv_ln_g[l].reshape(1, CONV_CH)
        lb = conv_ln_b[l].reshape(1, CONV_CH)
        qg2 = jnp.tile(qg * (HEAD_DIM ** -0.5), (1, LANES // HEAD_DIM))
        kg2 = jnp.tile(kg, (1, LANES // HEAD_DIM))
        w_in_bf, w_out_bf = _cast_pair_bf16(w_in[l], w_out[l])

        x1, zs, nk, nv, nc = _prompt_front(xp, xs, norm_mix_g[l], w_in_bf, w_out_bf, batch, seq,
                                           cos4, sin4, qg2, kg2, sinks[l], conv_w[l], cb, lg, lb)
        kp.append(nk.reshape(batch, WINDOW, N_KV_HEADS, HEAD_DIM))
        vp.append(nv.reshape(batch, WINDOW, N_KV_HEADS, HEAD_DIM))
        cp.append(nc)

        x1s, nk, nv, nc = _sample_mixer(
            zs, xs, cache_k[l].reshape(nsamp, WINDOW, KV_DIM), cache_v[l].reshape(nsamp, WINDOW, KV_DIM),
            state_conv[l], cos_s, sin_s, qg, kg, sinks[l], conv_w[l], cb, lg, lb, w_out_bf,
            bs=SAMPLE_BLOCK)
        ksm.append(nk.reshape(nsamp, WINDOW, N_KV_HEADS, HEAD_DIM))
        vsm.append(nv.reshape(nsamp, WINDOW, N_KV_HEADS, HEAD_DIM))
        csm.append(nc)

        xp, xs = _mlp(x1, x1s, norm_mlp_g[l], w_up[l], w_down[l])

    return (xp.reshape(batch, seq, d), xs.reshape(nsamp, dec_seq, d),
            jnp.stack(kp), jnp.stack(vp), jnp.stack(cp),
            jnp.stack(ksm), jnp.stack(vsm), jnp.stack(csm))
```

```python
import functools

import jax
import jax.numpy as jnp
from jax import lax
from jax.experimental import pallas as pl
from jax.experimental.pallas import tpu as pltpu

D_MODEL = 2048
ATTN_WIDTH = 1024
CONV_CH = 1024
HEAD_DIM = 64
HALF = HEAD_DIM // 2
N_HEADS = 16
N_KV_HEADS = 4
GROUP = N_HEADS // N_KV_HEADS
KV_DIM = N_KV_HEADS * HEAD_DIM
WINDOW = 128
CONV_WIDTH = 31
CONV_STATE = CONV_WIDTH - 1
D_FF = 4 * D_MODEL
ROPE_THETA = 10000.0
EPS = 1e-6
IN_COLS = ATTN_WIDTH + 2 * KV_DIM + 2 * CONV_CH
NEG = -1e30
PAST_LEN = 16384

K_OFF = ATTN_WIDTH
V_OFF = ATTN_WIDTH + KV_DIM
VAL_OFF = ATTN_WIDTH + 2 * KV_DIM
GATE_OFF = VAL_OFF + CONV_CH

VMEM_LIMIT_BYTES = 56 * 1024 * 1024
SUBLANES = 8
LANES = 128
BF16_ROWS = 16
VT_ROWS = HEAD_DIM + BF16_ROWS
CONV_PAD = 32

MLP_ROW_BLOCK = 1024
MLP_TILE = 512
CAST_ROWS = 256
SAMPLE_BLOCK = 16
PROJ_CHUNK = 256
SIDE_JOBS_PER_SOFTMAX = 3

F32 = jnp.float32
BF16 = jnp.bfloat16


def _rms_rows(x, g):
    ms = jnp.mean(x * x, axis=-1, keepdims=True)
    return x * lax.rsqrt(ms + EPS) * g


def _norm_dot(x, g, w):
    return jnp.dot(_rms_rows(x, g).astype(BF16), w, preferred_element_type=F32)


def _cast_kernel(a_ref, b_ref, oa_ref, ob_ref):
    oa_ref[...] = a_ref[...].astype(BF16)
    ob_ref[...] = b_ref[...].astype(BF16)


def _cast_pair_bf16(a, b):
    k = a.shape[0]
    assert b.shape[0] == k
    spec = lambda w: pl.BlockSpec((CAST_ROWS, w.shape[1]), lambda r: (r, 0))
    return pl.pallas_call(
        _cast_kernel,
        grid=(k // CAST_ROWS,),
        in_specs=[spec(a), spec(b)],
        out_specs=[spec(a), spec(b)],
        out_shape=[jax.ShapeDtypeStruct(a.shape, BF16), jax.ShapeDtypeStruct(b.shape, BF16)],
        compiler_params=pltpu.CompilerParams(
            dimension_semantics=("arbitrary",), vmem_limit_bytes=VMEM_LIMIT_BYTES),
        name="cast_bf16",
    )(a, b)


def _mlp_kernel(x_ref, xs_ref, g_ref, wu_ref, wd_ref, o_ref, os_ref, hm_ref):
    i, j = pl.program_id(0), pl.program_id(1)
    tm = x_ref.shape[0]

    @pl.when(j == 0)
    def _():
        x = x_ref[...]
        hm_ref[0:tm, :] = _rms_rows(x, g_ref[...]).astype(BF16)
        o_ref[...] = x

    @pl.when((i == 0) & (j == 0))
    def _():
        xs = xs_ref[...]
        hm_ref[tm:, :] = _rms_rows(xs, g_ref[...]).astype(BF16)
        os_ref[...] = xs

    def up_down(hm):
        h = jnp.dot(hm, wu_ref[...].astype(BF16), preferred_element_type=F32)
        h = jnp.square(jnp.maximum(h, 0.0)).astype(BF16)
        return jnp.dot(h, wd_ref[...].astype(BF16), preferred_element_type=F32)

    @pl.when(i == 0)
    def _():
        y = up_down(hm_ref[...])
        o_ref[...] += y[0:tm]
        os_ref[...] += y[tm:]

    @pl.when(i > 0)
    def _():
        o_ref[...] += up_down(hm_ref[0:tm, :])


def _mlp(x, xs, g, wu, wd):
    m, d = x.shape
    ms = xs.shape[0]
    f = wu.shape[1]
    tm, tf = MLP_ROW_BLOCK, MLP_TILE
    return pl.pallas_call(
        _mlp_kernel,
        grid=(m // tm, f // tf),
        in_specs=[
            pl.BlockSpec((tm, d), lambda i, j: (i, 0), pipeline_mode=pl.Buffered(1)),
            pl.BlockSpec((ms, d), lambda i, j: (0, 0)),
            pl.BlockSpec((1, d), lambda i, j: (0, 0)),
            pl.BlockSpec((d, tf), lambda i, j: (0, j)),
            pl.BlockSpec((tf, d), lambda i, j: (j, 0)),
        ],
        out_specs=[
            pl.BlockSpec((tm, d), lambda i, j: (i, 0)),
            pl.BlockSpec((ms, d), lambda i, j: (0, 0)),
        ],
        out_shape=[jax.ShapeDtypeStruct((m, d), F32), jax.ShapeDtypeStruct((ms, d), F32)],
        scratch_shapes=[pltpu.VMEM((tm + ms, d), BF16)],
        compiler_params=pltpu.CompilerParams(
            dimension_semantics=("arbitrary", "arbitrary"), vmem_limit_bytes=VMEM_LIMIT_BYTES),
        name="mlp",
    )(x, xs, g.reshape(1, d), wu, wd)


def _rotate_half(x):
    return jnp.concatenate([x[:, HALF:], x[:, :HALF]], axis=-1)


def _norm_rope(x, g, cos2, sin2):
    xn = _rms_rows(x, g)
    return xn * cos2 + _rotate_half(xn) * sin2


def _layernorm_swish(c, g, b):
    mu = jnp.mean(c, axis=-1, keepdims=True)
    xc = c - mu
    y = xc * lax.rsqrt(jnp.mean(xc * xc, axis=-1, keepdims=True) + EPS)
    y = y * g + b
    return y * jax.nn.sigmoid(y)


def _conv_taps_by_row_phase():
    groups = [[] for _ in range(SUBLANES)]
    for j in range(CONV_WIDTH):
        e = j + CONV_PAD - CONV_STATE
        groups[e % SUBLANES].append((j, e // SUBLANES))
    return groups


def _mixer_carry(n, klo_ref, khi_ref, vt_ref, ubuf_ref):
    @pl.when(n == 0)
    def _():
        klo_ref[:, 0:WINDOW, :] = jnp.zeros((N_KV_HEADS, WINDOW, LANES), BF16)
        khi_ref[:, 0:WINDOW, :] = jnp.zeros((N_KV_HEADS, WINDOW, LANES), BF16)
        vt_ref[:, 0:HEAD_DIM, 0:WINDOW] = jnp.zeros((N_KV_HEADS, HEAD_DIM, WINDOW), BF16)
        vt_ref[:, HEAD_DIM:, :] = jnp.ones((N_KV_HEADS, VT_ROWS - HEAD_DIM, 2 * WINDOW), BF16)
        ubuf_ref[0:CONV_PAD, :] = jnp.zeros((CONV_PAD, CONV_CH), F32)

    @pl.when(n > 0)
    def _():
        klo_ref[:, 0:WINDOW, :] = klo_ref[:, WINDOW:2 * WINDOW, :]
        khi_ref[:, 0:WINDOW, :] = khi_ref[:, WINDOW:2 * WINDOW, :]
        vt_ref[:, 0:HEAD_DIM, 0:WINDOW] = vt_ref[:, 0:HEAD_DIM, WINDOW:2 * WINDOW]
        ubuf_ref[0:CONV_PAD, :] = ubuf_ref[WINDOW:WINDOW + CONV_PAD, :]


def _mixer_block(n, z_ref, mix_ref, kr_ref, sinks_ref, cos_ref, sin_ref, qg_ref, kg_ref, cw_ref, cb_ref,
                 lg_ref, lb_ref, klo_ref, khi_ref, vt_ref, ubuf_ref, side_jobs=()):
    n_q_slabs = ATTN_WIDTH // LANES
    n_k_slabs = KV_DIM // LANES
    side = iter(side_jobs)

    def tick(k=1):
        for _ in range(k):
            job = next(side, None)
            if job is not None:
                job()

    u = z_ref[:, VAL_OFF:VAL_OFF + CONV_CH] * jax.nn.sigmoid(z_ref[:, GATE_OFF:GATE_OFF + CONV_CH])
    ubuf_ref[CONV_PAD:CONV_PAD + WINDOW, :] = u

    lane = lax.broadcasted_iota(jnp.int32, (WINDOW, LANES), 1)
    first_head = lane < HEAD_DIM
    first_half = (lane & HALF) == 0

    n_slabs = n_q_slabs + n_k_slabs
    x = jnp.concatenate([z_ref[:, m * LANES:(m + 1) * LANES] for m in range(n_slabs)], axis=0)
    sq = x * x
    sq_hi = sq.astype(BF16)
    sq_lo = (sq - sq_hi.astype(F32)).astype(BF16)
    er = lax.broadcasted_iota(jnp.int32, (LANES, LANES), 0)
    ec = lax.broadcasted_iota(jnp.int32, (LANES, LANES), 1)
    head_mean = jnp.where((er < HEAD_DIM) == (ec < HEAD_DIM), 1.0 / HEAD_DIM, 0.0).astype(BF16)
    ms = (jnp.dot(sq_hi, head_mean, preferred_element_type=F32)
          + jnp.dot(sq_lo, head_mean, preferred_element_type=F32))
    xn = x * lax.rsqrt(ms + EPS)

    cos4 = cos_ref[...]
    sin4 = sin_ref[...]

    def rope(xs):
        rot = jnp.where(first_half, pltpu.roll(xs, LANES - HALF, 1), pltpu.roll(xs, HALF, 1))
        return xs * cos4 + rot * sin4

    q_slabs = [rope(xn[m * WINDOW:(m + 1) * WINDOW] * qg_ref[...]).astype(BF16) for m in range(n_q_slabs)]

    zero = jnp.zeros((WINDOW, LANES), F32)
    for ks in range(n_k_slabs):
        r0 = (n_q_slabs + ks) * WINDOW
        kr = rope(xn[r0:r0 + WINDOW] * kg_ref[...])
        kr_ref[:, ks * LANES:(ks + 1) * LANES] = kr
        kr_swapped = pltpu.roll(kr, HEAD_DIM, 1)
        klo_ref[2 * ks, WINDOW:2 * WINDOW, :] = jnp.where(first_head, kr, zero).astype(BF16)
        khi_ref[2 * ks, WINDOW:2 * WINDOW, :] = jnp.where(first_head, zero, kr_swapped).astype(BF16)
        klo_ref[2 * ks + 1, WINDOW:2 * WINDOW, :] = jnp.where(first_head, kr_swapped, zero).astype(BF16)
        khi_ref[2 * ks + 1, WINDOW:2 * WINDOW, :] = jnp.where(first_head, zero, kr).astype(BF16)

        vs_t = z_ref[:, V_OFF + ks * LANES:V_OFF + (ks + 1) * LANES].T.astype(BF16)
        vt_ref[2 * ks, 0:HEAD_DIM, WINDOW:2 * WINDOW] = vs_t[0:HEAD_DIM]
        vt_ref[2 * ks + 1, 0:HEAD_DIM, WINDOW:2 * WINDOW] = vs_t[HEAD_DIM:2 * HEAD_DIM]

    key = lax.broadcasted_iota(jnp.int32, (2 * WINDOW, WINDOW), 0)
    qry = lax.broadcasted_iota(jnp.int32, (2 * WINDOW, WINDOW), 1)
    diff = qry + WINDOW - key
    band = (diff >= 0) & (diff <= jnp.where(n > 0, WINDOW - 1, qry))
    bias = jnp.where(band, 0.0, NEG)
    bias2 = jnp.concatenate([bias, bias], axis=1)

    def conv_chunk(c):
        cl = slice(c * LANES, (c + 1) * LANES)
        y = cb_ref[:, cl]
        for phase, taps in enumerate(_conv_taps_by_row_phase()):
            rows_needed = WINDOW + (SUBLANES if phase else 0)
            part = None
            for j, tile in taps:
                term = cw_ref[j:j + 1, cl] * ubuf_ref[tile * SUBLANES:tile * SUBLANES + rows_needed, cl]
                part = term if part is None else part + term
            if phase:
                part = pltpu.roll(part, rows_needed - phase, 0)[0:WINDOW]
            y = y + part
        return y

    conv_per_head = CONV_CH // LANES // N_KV_HEADS
    y_chunks = []
    for h in range(N_KV_HEADS):
        q2 = jnp.concatenate([q_slabs[2 * h], q_slabs[2 * h + 1]], axis=0)
        normed_t = []
        for second, k_ref in ((0, klo_ref), (1, khi_ref)):
            s_t = lax.dot_general(k_ref[h], q2, (((1,), (1,)), ((), ())), preferred_element_type=F32)
            s_t = s_t + bias2
            sink_row = jnp.concatenate(
                [jnp.full((1, WINDOW), sinks_ref[GROUP * h + 2 * r + second], F32) for r in range(2)], axis=1)
            m = jnp.maximum(jnp.max(s_t, axis=0, keepdims=True), sink_row)
            p_t = jnp.exp(s_t - m).astype(BF16)
            o_t = jnp.dot(vt_ref[h], p_t, preferred_element_type=F32)
            den = o_t[HEAD_DIM:HEAD_DIM + 1] + jnp.exp(sink_row - m)
            normed_t.append(o_t[0:HEAD_DIM] * (1.0 / den))
            tick(SIDE_JOBS_PER_SOFTMAX)
        for r in range(2):
            a_t = jnp.concatenate([normed_t[0][:, r * WINDOW:(r + 1) * WINDOW],
                                   normed_t[1][:, r * WINDOW:(r + 1) * WINDOW]], axis=0)
            m_out = 2 * h + r
            mix_ref[:, m_out * LANES:(m_out + 1) * LANES] = a_t.T.astype(BF16)
        y_chunks += [conv_chunk(conv_per_head * h + i) for i in range(conv_per_head)]

    conv = jnp.concatenate(y_chunks, axis=-1)
    mix_ref[:, ATTN_WIDTH:] = _layernorm_swish(conv, lg_ref[...], lb_ref[...]).astype(BF16)
    tick(len(side_jobs))


def _front_kernel(sinks_ref, x_next_ref, x_res_ref, xs_ref, g_ref, win_ref, wout_ref, cos_ref, sin_ref,
                  qg_ref, kg_ref, cw_ref, cb_ref, lg_ref, lb_ref,
                  x1_ref, zs_ref, nk_ref, nv_ref, nc_ref,
                  z_next_ref, z_ref, mix_ref, kr_ref, klo_ref, khi_ref, vt_ref, ubuf_ref, *, nb):
    t = pl.program_id(0)
    drain = pl.num_programs(0) - 1
    n = lax.rem(t, nb)

    @pl.when(t == 0)
    def _():
        z_ref[...] = _norm_dot(x_res_ref[...], g_ref[...], win_ref[...])
        zs_ref[...] = _norm_dot(xs_ref[...], g_ref[...], win_ref[...])
        mix_ref[...] = jnp.zeros(mix_ref.shape, BF16)

    _mixer_carry(n, klo_ref, khi_ref, vt_ref, ubuf_ref)

    xn_next = _rms_rows(x_next_ref[...], g_ref[...]).astype(BF16)
    mix_prev = mix_ref[...]
    jobs = []
    for c in range(IN_COLS // PROJ_CHUNK):
        def in_job(cols=slice(c * PROJ_CHUNK, (c + 1) * PROJ_CHUNK)):
            z_next_ref[:, cols] = jnp.dot(xn_next, win_ref[:, cols], preferred_element_type=F32)
        jobs.append(in_job)
    for c in range(x1_ref.shape[1] // PROJ_CHUNK):
        def out_job(cols=slice(c * PROJ_CHUNK, (c + 1) * PROJ_CHUNK)):
            x1_ref[:, cols] = x_res_ref[:, cols] + jnp.dot(mix_prev, wout_ref[:, cols],
                                                           preferred_element_type=F32)
        jobs.append(out_job)
    _mixer_block(n, z_ref, mix_ref, kr_ref, sinks_ref, cos_ref, sin_ref, qg_ref, kg_ref, cw_ref, cb_ref,
                 lg_ref, lb_ref, klo_ref, khi_ref, vt_ref, ubuf_ref, side_jobs=jobs)

    @pl.when((n == nb - 1) & (t < drain))
    def _():
        nk_ref[0] = kr_ref[...].T
        nv_ref[0] = z_ref[:, V_OFF:V_OFF + KV_DIM].T
        nc_ref[0] = ubuf_ref[CONV_PAD + WINDOW - CONV_STATE:CONV_PAD + WINDOW, :]

    z_ref[...] = z_next_ref[...]


def _prompt_front(x, xs, g, w_in, w_out, batch, seq, cos4, sin4, qg2, kg2, sinks, cw, cb, lg, lb):
    nb = seq // WINDOW
    steps = batch * nb
    d = x.shape[1]
    ms = xs.shape[0]
    const = lambda t: (0, 0)
    behind = lambda t: (jnp.maximum(t - 1, 0), 0)
    ahead = lambda t: (jnp.minimum(t + 1, steps - 1), 0)
    pos = lambda t: (lax.rem(t, nb), 0)
    per_batch = lambda t: (jnp.minimum(t // nb, batch - 1), 0, 0)
    resident = dict(pipeline_mode=pl.Buffered(1))
    return pl.pallas_call(
        functools.partial(_front_kernel, nb=nb),
        grid=(steps + 1,),
        in_specs=[
            pl.BlockSpec(memory_space=pltpu.SMEM),
            pl.BlockSpec((WINDOW, d), ahead),
            pl.BlockSpec((WINDOW, d), behind),
            pl.BlockSpec((ms, d), const),
            pl.BlockSpec((1, d), const),
            pl.BlockSpec((d, IN_COLS), const, **resident),
            pl.BlockSpec((ATTN_WIDTH + CONV_CH, d), const, **resident),
            pl.BlockSpec((WINDOW, LANES), pos),
            pl.BlockSpec((WINDOW, LANES), pos),
            pl.BlockSpec((1, LANES), const),
            pl.BlockSpec((1, LANES), const),
            pl.BlockSpec((CONV_WIDTH, CONV_CH), const),
            pl.BlockSpec((1, CONV_CH), const),
            pl.BlockSpec((1, CONV_CH), const),
            pl.BlockSpec((1, CONV_CH), const),
        ],
        out_specs=[
            pl.BlockSpec((WINDOW, d), behind),
            pl.BlockSpec((ms, IN_COLS), const),
            pl.BlockSpec((1, KV_DIM, WINDOW), per_batch),
            pl.BlockSpec((1, KV_DIM, WINDOW), per_batch),
            pl.BlockSpec((1, CONV_STATE, CONV_CH), per_batch),
        ],
        out_shape=[
            jax.ShapeDtypeStruct((steps * WINDOW, d), F32),
            jax.ShapeDtypeStruct((ms, IN_COLS), F32),
            jax.ShapeDtypeStruct((batch, KV_DIM, WINDOW), F32),
            jax.ShapeDtypeStruct((batch, KV_DIM, WINDOW), F32),
            jax.ShapeDtypeStruct((batch, CONV_STATE, CONV_CH), F32),
        ],
        scratch_shapes=[
            pltpu.VMEM((WINDOW, IN_COLS), F32),
            pltpu.VMEM((WINDOW, IN_COLS), F32),
            pltpu.VMEM((WINDOW, ATTN_WIDTH + CONV_CH), BF16),
            pltpu.VMEM((WINDOW, KV_DIM), F32),
            pltpu.VMEM((N_KV_HEADS, 2 * WINDOW, LANES), BF16),
            pltpu.VMEM((N_KV_HEADS, 2 * WINDOW, LANES), BF16),
            pltpu.VMEM((N_KV_HEADS, VT_ROWS, 2 * WINDOW), BF16),
            pltpu.VMEM((CONV_PAD + WINDOW, CONV_CH), F32),
        ],
        compiler_params=pltpu.CompilerParams(
            dimension_semantics=("arbitrary",), vmem_limit_bytes=VMEM_LIMIT_BYTES),
        name="prompt_front",
    )(sinks, x, x, xs, g.reshape(1, d), w_in, w_out, cos4, sin4, qg2, kg2, cw, cb, lg, lb)


def _sample_mixer_kernel(sinks_ref, z_ref, xs_ref, ck_ref, cv_ref, st_ref, cos_ref, sin_ref,
                         qg_ref, kg_ref, cw_ref, cb_ref, lg_ref, lb_ref, wout_ref,
                         x1_ref, nk_ref, nv_ref, nc_ref, qexp_ref, knew_ref, a_ref):
    bs = z_ref.shape[0]
    cos2 = cos_ref[...]
    sin2 = sin_ref[...]

    q = jnp.concatenate([z_ref[:, h * HEAD_DIM:(h + 1) * HEAD_DIM] for h in range(N_HEADS)], axis=0)
    qn = _norm_rope(q, qg_ref[...], cos2, sin2).astype(BF16)
    tile_r = lax.broadcasted_iota(jnp.int32, (HEAD_DIM, KV_DIM), 0)
    tile_c = lax.broadcasted_iota(jnp.int32, (HEAD_DIM, KV_DIM), 1)
    tile = jnp.where((tile_c & (HEAD_DIM - 1)) == tile_r, 1.0, 0.0).astype(BF16)
    qt = jnp.dot(qn, tile, preferred_element_type=F32)
    q_row = lax.broadcasted_iota(jnp.int32, (N_HEADS * bs, KV_DIM), 0)
    q_col = lax.broadcasted_iota(jnp.int32, (N_HEADS * bs, KV_DIM), 1)
    own = (q_row // (bs * GROUP)) == (q_col // HEAD_DIM)
    qexp = jnp.where(own, qt, 0.0)
    for half in range(KV_DIM // LANES):
        qexp_ref[half] = qexp[:, half * LANES:(half + 1) * LANES]

    k = jnp.concatenate([z_ref[:, K_OFF + h * HEAD_DIM:K_OFF + (h + 1) * HEAD_DIM]
                         for h in range(N_KV_HEADS)], axis=0)
    kn = _norm_rope(k, kg_ref[...], cos2, sin2)
    for h in range(N_KV_HEADS):
        knew_ref[:, h * HEAD_DIM:(h + 1) * HEAD_DIM] = kn[h * bs:(h + 1) * bs]
    knew = knew_ref[...]
    vnew = z_ref[:, V_OFF:V_OFF + KV_DIM]
    knew_r = knew.astype(BF16).astype(F32)
    vnew_r = vnew.astype(BF16).astype(F32)
    pad = jnp.zeros((WINDOW - bs, KV_DIM), F32)
    knew_t = jnp.concatenate([knew, pad], axis=0).T
    vnew_t = jnp.concatenate([vnew, pad], axis=0).T
    newest = lax.broadcasted_iota(jnp.int32, (KV_DIM, WINDOW), 1) == WINDOW - 1

    sink_col = jnp.concatenate([jnp.full((1, 1), sinks_ref[hh], F32) for hh in range(N_HEADS)], axis=0)
    key_idx = lax.broadcasted_iota(jnp.int32, (N_HEADS, WINDOW), 1)
    key_ok = key_idx >= 1
    o_row = lax.broadcasted_iota(jnp.int32, (N_HEADS, KV_DIM), 0)
    o_col = lax.broadcasted_iota(jnp.int32, (N_HEADS, KV_DIM), 1)
    o_own = (o_row // GROUP) == (o_col // HEAD_DIM)
    fold_r = lax.broadcasted_iota(jnp.int32, (KV_DIM, LANES), 0)
    fold_c = lax.broadcasted_iota(jnp.int32, (KV_DIM, LANES), 1)
    fold = jnp.where((fold_r & (HEAD_DIM - 1)) == fold_c, 1.0, 0.0).astype(BF16)

    for b in range(bs):
        heads_of_b = pl.ds(b, N_HEADS, stride=bs)
        qb = jnp.concatenate([qexp_ref[half, heads_of_b, :] for half in range(KV_DIM // LANES)], axis=1)
        kc = ck_ref[b]
        vc = cv_ref[b]
        s = jnp.dot(qb.astype(BF16), kc.astype(BF16), preferred_element_type=F32)
        s = jnp.where(key_ok, s * (HEAD_DIM ** -0.5), NEG)
        s_new = jnp.sum(qb * knew_r[b:b + 1, :], axis=-1, keepdims=True) * (HEAD_DIM ** -0.5)
        m = jnp.maximum(jnp.maximum(jnp.max(s, axis=-1, keepdims=True), s_new), sink_col)
        p = jnp.exp(s - m)
        p_new = jnp.exp(s_new - m)
        den = jnp.sum(p, axis=-1, keepdims=True) + p_new + jnp.exp(sink_col - m)
        p = p / den
        p_new = (p_new / den).astype(BF16).astype(F32)
        o = lax.dot_general(p.astype(BF16), vc.astype(BF16), (((1,), (1,)), ((), ())),
                            preferred_element_type=F32)
        o = o + p_new * vnew_r[b:b + 1, :]
        o = jnp.where(o_own, o, 0.0).astype(BF16)
        a_ref[heads_of_b, :] = jnp.dot(o, fold, preferred_element_type=F32)

        nk_ref[b] = jnp.where(newest, pltpu.roll(knew_t, WINDOW - 1 - b, 1), pltpu.roll(kc, WINDOW - 1, 1))
        nv_ref[b] = jnp.where(newest, pltpu.roll(vnew_t, WINDOW - 1 - b, 1), pltpu.roll(vc, WINDOW - 1, 1))

    u = z_ref[:, VAL_OFF:VAL_OFF + CONV_CH] * jax.nn.sigmoid(z_ref[:, GATE_OFF:GATE_OFF + CONV_CH])
    y = cb_ref[...] + u * cw_ref[CONV_STATE:CONV_WIDTH, :]
    for j in range(CONV_STATE):
        y = y + st_ref[j] * cw_ref[j:j + 1, :]
        nc_ref[j] = st_ref[j + 1] if j + 1 < CONV_STATE else u
    c = _layernorm_swish(y, lg_ref[...], lb_ref[...])

    x1 = xs_ref[...] + jnp.dot(c.astype(BF16), wout_ref[ATTN_WIDTH:, :], preferred_element_type=F32)
    for h in range(N_HEADS):
        x1 = x1 + jnp.dot(a_ref[h * bs:(h + 1) * bs, 0:HEAD_DIM].astype(BF16),
                          wout_ref[h * HEAD_DIM:(h + 1) * HEAD_DIM, :], preferred_element_type=F32)
    x1_ref[...] = x1


def _sample_mixer(zs, xs, cache_k, cache_v, state_t, cos2, sin2, qg, kg, sinks, cw, cb, lg, lb, w_out, *, bs):
    nsamp, d = xs.shape
    const = lambda i: (0, 0)
    step = lambda i: (i, 0)
    step3 = lambda i: (i, 0, 0)
    taps = lambda i: (0, i, 0)
    return pl.pallas_call(
        _sample_mixer_kernel,
        grid=(nsamp // bs,),
        in_specs=[
            pl.BlockSpec(memory_space=pltpu.SMEM),
            pl.BlockSpec((bs, IN_COLS), step),
            pl.BlockSpec((bs, d), step),
            pl.BlockSpec((bs, KV_DIM, WINDOW), step3),
            pl.BlockSpec((bs, KV_DIM, WINDOW), step3),
            pl.BlockSpec((CONV_STATE, bs, CONV_CH), taps),
            pl.BlockSpec((1, HEAD_DIM), const),
            pl.BlockSpec((1, HEAD_DIM), const),
            pl.BlockSpec((1, HEAD_DIM), const),
            pl.BlockSpec((1, HEAD_DIM), const),
            pl.BlockSpec((CONV_WIDTH, CONV_CH), const),
            pl.BlockSpec((1, CONV_CH), const),
            pl.BlockSpec((1, CONV_CH), const),
            pl.BlockSpec((1, CONV_CH), const),
            pl.BlockSpec((ATTN_WIDTH + CONV_CH, d), const, pipeline_mode=pl.Buffered(1)),
        ],
        out_specs=[
            pl.BlockSpec((bs, d), step),
            pl.BlockSpec((bs, KV_DIM, WINDOW), step3),
            pl.BlockSpec((bs, KV_DIM, WINDOW), step3),
            pl.BlockSpec((CONV_STATE, bs, CONV_CH), taps),
        ],
        out_shape=[
            jax.ShapeDtypeStruct((nsamp, d), F32),
            jax.ShapeDtypeStruct((nsamp, KV_DIM, WINDOW), F32),
            jax.ShapeDtypeStruct((nsamp, KV_DIM, WINDOW), F32),
            jax.ShapeDtypeStruct((CONV_STATE, nsamp, CONV_CH), F32),
        ],
        scratch_shapes=[
            pltpu.VMEM((KV_DIM // LANES, N_HEADS * bs, LANES), F32),
            pltpu.VMEM((bs, KV_DIM), F32),
            pltpu.VMEM((N_HEADS * bs, LANES), F32),
        ],
        compiler_params=pltpu.CompilerParams(
            dimension_semantics=("arbitrary",), vmem_limit_bytes=VMEM_LIMIT_BYTES),
        name="sample_mixer",
    )(sinks, zs, xs, cache_k, cache_v, state_t, cos2, sin2, qg, kg, cw, cb, lg, lb, w_out)


def _rope_tables(pos, heads_per_row=1):
    reps = 2 * heads_per_row
    inv = jnp.tile(ROPE_THETA ** (-jnp.arange(HALF, dtype=F32) / HALF), reps)
    sign = jnp.tile(jnp.concatenate([-jnp.ones((HALF,), F32), jnp.ones((HALF,), F32)]), heads_per_row)
    ang = pos.astype(F32)[:, None] * inv[None, :]
    return jnp.cos(ang), jnp.sin(ang) * sign[None, :]


def _to_feature_major(cache):
    rows = cache.shape[0]
    return jnp.transpose(cache, (0, 2, 3, 1)).reshape(rows, KV_DIM, WINDOW)


def _from_feature_major(cache_t):
    rows = cache_t.shape[0]
    return jnp.transpose(cache_t.reshape(rows, N_KV_HEADS, HEAD_DIM, WINDOW), (0, 3, 1, 2))


def kernel(x_prompt, x_sample, cache_k, cache_v, state_conv, norm_mix_g, w_in, q_norm_g, k_norm_g,
           sinks, conv_w, conv_b, conv_ln_g, conv_ln_b, w_out, norm_mlp_g, w_up, w_down):
    batch, seq, d = x_prompt.shape
    nsamp, dec_seq, _ = x_sample.shape
    depth = w_in.shape[0]
    assert dec_seq == 1 and seq % WINDOW == 0 and cache_k.shape[2] == WINDOW

    cos4, sin4 = _rope_tables(jnp.arange(seq), heads_per_row=LANES // HEAD_DIM)
    cos_s, sin_s = _rope_tables(PAST_LEN + jnp.arange(dec_seq))

    xp = x_prompt.reshape(batch * seq, d)
    xs = x_sample.reshape(nsamp * dec_seq, d)
    kp, vp, cp, ksm, vsm, csm = [], [], [], [], [], []
    for l in range(depth):
        qg = q_norm_g[l].reshape(1, HEAD_DIM)
        kg = k_norm_g[l].reshape(1, HEAD_DIM)
        cb = conv_b[l].reshape(1, CONV_CH)
        lg = conv_ln_g[l].reshape(1, CONV_CH)
        lb = conv_ln_b[l].reshape(1, CONV_CH)
        qg2 = jnp.tile(qg * (HEAD_DIM ** -0.5), (1, LANES // HEAD_DIM))
        kg2 = jnp.tile(kg, (1, LANES // HEAD_DIM))
        w_in_bf, w_out_bf = _cast_pair_bf16(w_in[l], w_out[l])

        x1, zs, nk, nv, nc = _prompt_front(xp, xs, norm_mix_g[l], w_in_bf, w_out_bf, batch, seq,
                                           cos4, sin4, qg2, kg2, sinks[l], conv_w[l], cb, lg, lb)
        kp.append(_from_feature_major(nk))
        vp.append(_from_feature_major(nv))
        cp.append(nc)

        x1s, nk, nv, nc = _sample_mixer(
            zs, xs, _to_feature_major(cache_k[l]), _to_feature_major(cache_v[l]),
            jnp.transpose(state_conv[l], (1, 0, 2)), cos_s, sin_s, qg, kg, sinks[l], conv_w[l], cb, lg, lb,
            w_out_bf, bs=SAMPLE_BLOCK)
        ksm.append(_from_feature_major(nk))
        vsm.append(_from_feature_major(nv))
        csm.append(jnp.transpose(nc, (1, 0, 2)))

        xp, xs = _mlp(x1, x1s, norm_mlp_g[l], w_up[l], w_down[l])

    return (xp.reshape(batch, seq, d), xs.reshape(nsamp, dec_seq, d),
            jnp.stack(kp), jnp.stack(vp), jnp.stack(cp),
            jnp.stack(ksm), jnp.stack(vsm), jnp.stack(csm))
```

```python
import functools

import jax
import jax.numpy as jnp
from jax import lax
from jax.experimental import pallas as pl
from jax.experimental.pallas import tpu as pltpu

D_MODEL = 2048
ATTN_WIDTH = 1024
CONV_CH = 1024
HEAD_DIM = 64
HALF = HEAD_DIM // 2
N_HEADS = 16
N_KV_HEADS = 4
GROUP = N_HEADS // N_KV_HEADS
KV_DIM = N_KV_HEADS * HEAD_DIM
WINDOW = 128
CONV_WIDTH = 31
CONV_STATE = CONV_WIDTH - 1
D_FF = 4 * D_MODEL
ROPE_THETA = 10000.0
EPS = 1e-6
IN_COLS = ATTN_WIDTH + 2 * KV_DIM + 2 * CONV_CH
NEG = -1e30
PAST_LEN = 16384

K_OFF = ATTN_WIDTH
V_OFF = ATTN_WIDTH + KV_DIM
VAL_OFF = ATTN_WIDTH + 2 * KV_DIM
GATE_OFF = VAL_OFF + CONV_CH

VMEM_LIMIT_BYTES = 56 * 1024 * 1024
SUBLANES = 8
LANES = 128
BF16_ROWS = 16
VT_ROWS = HEAD_DIM + BF16_ROWS
CONV_PAD = 32

MLP_ROW_BLOCK = 1024
MLP_TILE = 512
CAST_ROWS = 256
SAMPLE_BLOCK = 16
PROJ_CHUNK = 256
SIDE_JOBS_PER_SOFTMAX = 3

F32 = jnp.float32
BF16 = jnp.bfloat16


def _rms_rows(x, g):
    ms = jnp.mean(x * x, axis=-1, keepdims=True)
    return x * lax.rsqrt(ms + EPS) * g


def _norm_dot(x, g, w):
    return jnp.dot(_rms_rows(x, g).astype(BF16), w, preferred_element_type=F32)


def _mlp_kernel(x_ref, xs_ref, g_ref, wu_ref, wd_ref, o_ref, os_ref, hm_ref):
    i, j = pl.program_id(0), pl.program_id(1)
    tm = x_ref.shape[0]

    @pl.when(j == 0)
    def _():
        x = x_ref[...]
        hm_ref[0:tm, :] = _rms_rows(x, g_ref[...]).astype(BF16)
        o_ref[...] = x

    @pl.when((i == 0) & (j == 0))
    def _():
        xs = xs_ref[...]
        hm_ref[tm:, :] = _rms_rows(xs, g_ref[...]).astype(BF16)
        os_ref[...] = xs

    def up_down(hm):
        h = jnp.dot(hm, wu_ref[...].astype(BF16), preferred_element_type=F32)
        h = jnp.square(jnp.maximum(h, 0.0)).astype(BF16)
        return jnp.dot(h, wd_ref[...].astype(BF16), preferred_element_type=F32)

    @pl.when(i == 0)
    def _():
        y = up_down(hm_ref[...])
        o_ref[...] += y[0:tm]
        os_ref[...] += y[tm:]

    @pl.when(i > 0)
    def _():
        o_ref[...] += up_down(hm_ref[0:tm, :])


def _mlp(x, xs, g, wu, wd):
    m, d = x.shape
    ms = xs.shape[0]
    f = wu.shape[1]
    tm, tf = MLP_ROW_BLOCK, MLP_TILE
    return pl.pallas_call(
        _mlp_kernel,
        grid=(m // tm, f // tf),
        in_specs=[
            pl.BlockSpec((tm, d), lambda i, j: (i, 0)),
            pl.BlockSpec((ms, d), lambda i, j: (0, 0)),
            pl.BlockSpec((1, d), lambda i, j: (0, 0)),
            pl.BlockSpec((d, tf), lambda i, j: (0, j)),
            pl.BlockSpec((tf, d), lambda i, j: (j, 0)),
        ],
        out_specs=[
            pl.BlockSpec((tm, d), lambda i, j: (i, 0)),
            pl.BlockSpec((ms, d), lambda i, j: (0, 0)),
        ],
        out_shape=[jax.ShapeDtypeStruct((m, d), F32), jax.ShapeDtypeStruct((ms, d), F32)],
        scratch_shapes=[pltpu.VMEM((tm + ms, d), BF16)],
        compiler_params=pltpu.CompilerParams(
            dimension_semantics=("arbitrary", "arbitrary"), vmem_limit_bytes=VMEM_LIMIT_BYTES),
        name="mlp",
    )(x, xs, g.reshape(1, d), wu, wd)


def _rotate_half(x):
    return jnp.concatenate([x[:, HALF:], x[:, :HALF]], axis=-1)


def _norm_rope(x, g, cos2, sin2):
    xn = _rms_rows(x, g)
    return xn * cos2 + _rotate_half(xn) * sin2


def _layernorm_swish(c, g, b):
    mu = jnp.mean(c, axis=-1, keepdims=True)
    xc = c - mu
    y = xc * lax.rsqrt(jnp.mean(xc * xc, axis=-1, keepdims=True) + EPS)
    y = y * g + b
    return y * jax.nn.sigmoid(y)


def _conv_taps_by_row_phase():
    groups = [[] for _ in range(SUBLANES)]
    for j in range(CONV_WIDTH):
        e = j + CONV_PAD - CONV_STATE
        groups[e % SUBLANES].append((j, e // SUBLANES))
    return groups


def _mixer_carry(n, klo_ref, khi_ref, vt_ref, ubuf_ref):
    @pl.when(n == 0)
    def _():
        klo_ref[:, 0:WINDOW, :] = jnp.zeros((N_KV_HEADS, WINDOW, LANES), BF16)
        khi_ref[:, 0:WINDOW, :] = jnp.zeros((N_KV_HEADS, WINDOW, LANES), BF16)
        vt_ref[:, 0:HEAD_DIM, 0:WINDOW] = jnp.zeros((N_KV_HEADS, HEAD_DIM, WINDOW), BF16)
        vt_ref[:, HEAD_DIM:, :] = jnp.ones((N_KV_HEADS, VT_ROWS - HEAD_DIM, 2 * WINDOW), BF16)
        ubuf_ref[0:CONV_PAD, :] = jnp.zeros((CONV_PAD, CONV_CH), F32)

    @pl.when(n > 0)
    def _():
        klo_ref[:, 0:WINDOW, :] = klo_ref[:, WINDOW:2 * WINDOW, :]
        khi_ref[:, 0:WINDOW, :] = khi_ref[:, WINDOW:2 * WINDOW, :]
        vt_ref[:, 0:HEAD_DIM, 0:WINDOW] = vt_ref[:, 0:HEAD_DIM, WINDOW:2 * WINDOW]
        ubuf_ref[0:CONV_PAD, :] = ubuf_ref[WINDOW:WINDOW + CONV_PAD, :]


def _mixer_block(n, z_ref, mix_ref, kr_ref, sinks_ref, cos_ref, sin_ref, qg_ref, kg_ref, cw_ref, cb_ref,
                 lg_ref, lb_ref, klo_ref, khi_ref, vt_ref, ubuf_ref, side_jobs=()):
    n_q_slabs = ATTN_WIDTH // LANES
    n_k_slabs = KV_DIM // LANES
    side = iter(side_jobs)

    def tick(k=1):
        for _ in range(k):
            job = next(side, None)
            if job is not None:
                job()

    u = z_ref[:, VAL_OFF:VAL_OFF + CONV_CH] * jax.nn.sigmoid(z_ref[:, GATE_OFF:GATE_OFF + CONV_CH])
    ubuf_ref[CONV_PAD:CONV_PAD + WINDOW, :] = u

    lane = lax.broadcasted_iota(jnp.int32, (WINDOW, LANES), 1)
    first_head = lane < HEAD_DIM
    first_half = (lane & HALF) == 0

    n_slabs = n_q_slabs + n_k_slabs
    x = jnp.concatenate([z_ref[:, m * LANES:(m + 1) * LANES] for m in range(n_slabs)], axis=0)
    sq = x * x
    sq_hi = sq.astype(BF16)
    sq_lo = (sq - sq_hi.astype(F32)).astype(BF16)
    er = lax.broadcasted_iota(jnp.int32, (LANES, LANES), 0)
    ec = lax.broadcasted_iota(jnp.int32, (LANES, LANES), 1)
    head_mean = jnp.where((er < HEAD_DIM) == (ec < HEAD_DIM), 1.0 / HEAD_DIM, 0.0).astype(BF16)
    ms = (jnp.dot(sq_hi, head_mean, preferred_element_type=F32)
          + jnp.dot(sq_lo, head_mean, preferred_element_type=F32))
    xn = x * lax.rsqrt(ms + EPS)

    cos4 = cos_ref[...]
    sin4 = sin_ref[...]

    def rope(xs):
        rot = jnp.where(first_half, pltpu.roll(xs, LANES - HALF, 1), pltpu.roll(xs, HALF, 1))
        return xs * cos4 + rot * sin4

    q_slabs = [rope(xn[m * WINDOW:(m + 1) * WINDOW] * qg_ref[...]).astype(BF16) for m in range(n_q_slabs)]

    zero = jnp.zeros((WINDOW, LANES), F32)
    for ks in range(n_k_slabs):
        r0 = (n_q_slabs + ks) * WINDOW
        kr = rope(xn[r0:r0 + WINDOW] * kg_ref[...])
        kr_ref[:, ks * LANES:(ks + 1) * LANES] = kr
        kr_swapped = pltpu.roll(kr, HEAD_DIM, 1)
        klo_ref[2 * ks, WINDOW:2 * WINDOW, :] = jnp.where(first_head, kr, zero).astype(BF16)
        khi_ref[2 * ks, WINDOW:2 * WINDOW, :] = jnp.where(first_head, zero, kr_swapped).astype(BF16)
        klo_ref[2 * ks + 1, WINDOW:2 * WINDOW, :] = jnp.where(first_head, kr_swapped, zero).astype(BF16)
        khi_ref[2 * ks + 1, WINDOW:2 * WINDOW, :] = jnp.where(first_head, zero, kr).astype(BF16)

        vs_t = z_ref[:, V_OFF + ks * LANES:V_OFF + (ks + 1) * LANES].T.astype(BF16)
        vt_ref[2 * ks, 0:HEAD_DIM, WINDOW:2 * WINDOW] = vs_t[0:HEAD_DIM]
        vt_ref[2 * ks + 1, 0:HEAD_DIM, WINDOW:2 * WINDOW] = vs_t[HEAD_DIM:2 * HEAD_DIM]

    key = lax.broadcasted_iota(jnp.int32, (2 * WINDOW, WINDOW), 0)
    qry = lax.broadcasted_iota(jnp.int32, (2 * WINDOW, WINDOW), 1)
    diff = qry + WINDOW - key
    band = (diff >= 0) & (diff <= jnp.where(n > 0, WINDOW - 1, qry))
    bias = jnp.where(band, 0.0, NEG)
    bias2 = jnp.concatenate([bias, bias], axis=1)

    def conv_chunk(c):
        cl = slice(c * LANES, (c + 1) * LANES)
        y = cb_ref[:, cl]
        for phase, taps in enumerate(_conv_taps_by_row_phase()):
            rows_needed = WINDOW + (SUBLANES if phase else 0)
            part = None
            for j, tile in taps:
                term = cw_ref[j:j + 1, cl] * ubuf_ref[tile * SUBLANES:tile * SUBLANES + rows_needed, cl]
                part = term if part is None else part + term
            if phase:
                part = pltpu.roll(part, rows_needed - phase, 0)[0:WINDOW]
            y = y + part
        return y

    conv_per_head = CONV_CH // LANES // N_KV_HEADS
    y_chunks = []
    for h in range(N_KV_HEADS):
        q2 = jnp.concatenate([q_slabs[2 * h], q_slabs[2 * h + 1]], axis=0)
        normed_t = []
        for second, k_ref in ((0, klo_ref), (1, khi_ref)):
            s_t = lax.dot_general(k_ref[h], q2, (((1,), (1,)), ((), ())), preferred_element_type=F32)
            s_t = s_t + bias2
            sink_row = jnp.concatenate(
                [jnp.full((1, WINDOW), sinks_ref[GROUP * h + 2 * r + second], F32) for r in range(2)], axis=1)
            m = jnp.maximum(jnp.max(s_t, axis=0, keepdims=True), sink_row)
            p_t = jnp.exp(s_t - m).astype(BF16)
            o_t = jnp.dot(vt_ref[h], p_t, preferred_element_type=F32)
            den = o_t[HEAD_DIM:HEAD_DIM + 1] + jnp.exp(sink_row - m)
            normed_t.append(o_t[0:HEAD_DIM] * (1.0 / den))
            tick(SIDE_JOBS_PER_SOFTMAX)
        for r in range(2):
            a_t = jnp.concatenate([normed_t[0][:, r * WINDOW:(r + 1) * WINDOW],
                                   normed_t[1][:, r * WINDOW:(r + 1) * WINDOW]], axis=0)
            m_out = 2 * h + r
            mix_ref[:, m_out * LANES:(m_out + 1) * LANES] = a_t.T.astype(BF16)
        y_chunks += [conv_chunk(conv_per_head * h + i) for i in range(conv_per_head)]

    conv = jnp.concatenate(y_chunks, axis=-1)
    mix_ref[:, ATTN_WIDTH:] = _layernorm_swish(conv, lg_ref[...], lb_ref[...]).astype(BF16)
    tick(len(side_jobs))


def _weight_chunk_copy(w_hbm, stage_ref, sem, c, slot):
    cols = w_hbm.shape[1]
    return pltpu.make_async_copy(w_hbm.at[pl.ds(c * CAST_ROWS, CAST_ROWS), :],
                                 stage_ref.at[slot, :, pl.ds(0, cols)], sem.at[slot])


def _load_weight_bf16(w_hbm, w_ref, stage_ref, sem):
    n_chunks = w_ref.shape[0] // CAST_ROWS
    cols = w_ref.shape[1]
    _weight_chunk_copy(w_hbm, stage_ref, sem, 0, 0).start()
    for c in range(n_chunks):
        slot = c % 2
        if c + 1 < n_chunks:
            _weight_chunk_copy(w_hbm, stage_ref, sem, c + 1, 1 - slot).start()
        _weight_chunk_copy(w_hbm, stage_ref, sem, c, slot).wait()
        w_ref[c * CAST_ROWS:(c + 1) * CAST_ROWS, :] = stage_ref[slot, :, 0:cols].astype(BF16)


def _front_kernel(sinks_ref, x_next_ref, x_res_ref, xs_ref, g_ref, win_hbm, wout_hbm, cos_ref, sin_ref,
                  qg_ref, kg_ref, cw_ref, cb_ref, lg_ref, lb_ref,
                  x1_ref, zs_ref, nk_ref, nv_ref, nc_ref, wout_bf_hbm,
                  win_ref, wout_ref, stage_ref, load_sem, export_sem,
                  z_next_ref, z_ref, mix_ref, kr_ref, klo_ref, khi_ref, vt_ref, ubuf_ref, *, nb):
    t = pl.program_id(0)
    drain = pl.num_programs(0) - 1
    n = lax.rem(t, nb)
    export = pltpu.make_async_copy(wout_ref, wout_bf_hbm, export_sem.at[0])

    @pl.when(t == 0)
    def _():
        _load_weight_bf16(win_hbm, win_ref, stage_ref, load_sem)
        _load_weight_bf16(wout_hbm, wout_ref, stage_ref, load_sem)
        export.start()
        z_ref[...] = _norm_dot(x_res_ref[...], g_ref[...], win_ref[...])
        zs_ref[...] = _norm_dot(xs_ref[...], g_ref[...], win_ref[...])
        mix_ref[...] = jnp.zeros(mix_ref.shape, BF16)

    _mixer_carry(n, klo_ref, khi_ref, vt_ref, ubuf_ref)

    xn_next = _rms_rows(x_next_ref[...], g_ref[...]).astype(BF16)
    mix_prev = mix_ref[...]
    jobs = []
    for c in range(IN_COLS // PROJ_CHUNK):
        def in_job(cols=slice(c * PROJ_CHUNK, (c + 1) * PROJ_CHUNK)):
            z_next_ref[:, cols] = jnp.dot(xn_next, win_ref[:, cols], preferred_element_type=F32)
        jobs.append(in_job)
    for c in range(x1_ref.shape[1] // PROJ_CHUNK):
        def out_job(cols=slice(c * PROJ_CHUNK, (c + 1) * PROJ_CHUNK)):
            x1_ref[:, cols] = x_res_ref[:, cols] + jnp.dot(mix_prev, wout_ref[:, cols],
                                                           preferred_element_type=F32)
        jobs.append(out_job)
    _mixer_block(n, z_ref, mix_ref, kr_ref, sinks_ref, cos_ref, sin_ref, qg_ref, kg_ref, cw_ref, cb_ref,
                 lg_ref, lb_ref, klo_ref, khi_ref, vt_ref, ubuf_ref, side_jobs=jobs)

    @pl.when((n == nb - 1) & (t < drain))
    def _():
        nk_ref[0] = kr_ref[...].T
        nv_ref[0] = z_ref[:, V_OFF:V_OFF + KV_DIM].T
        nc_ref[0] = ubuf_ref[CONV_PAD + WINDOW - CONV_STATE:CONV_PAD + WINDOW, :]

    z_ref[...] = z_next_ref[...]

    @pl.when(t == drain)
    def _():
        export.wait()


def _prompt_front(x, xs, g, w_in, w_out, batch, seq, cos4, sin4, qg2, kg2, sinks, cw, cb, lg, lb):
    nb = seq // WINDOW
    steps = batch * nb
    d = x.shape[1]
    ms = xs.shape[0]
    const = lambda t: (0, 0)
    behind = lambda t: (jnp.maximum(t - 1, 0), 0)
    ahead = lambda t: (jnp.minimum(t + 1, steps - 1), 0)
    pos = lambda t: (lax.rem(t, nb), 0)
    per_batch = lambda t: (jnp.minimum(t // nb, batch - 1), 0, 0)
    assert w_in.shape[0] % CAST_ROWS == 0 and w_out.shape[0] % CAST_ROWS == 0
    stage_cols = max(w_in.shape[1], w_out.shape[1])
    return pl.pallas_call(
        functools.partial(_front_kernel, nb=nb),
        grid=(steps + 1,),
        in_specs=[
            pl.BlockSpec(memory_space=pltpu.SMEM),
            pl.BlockSpec((WINDOW, d), ahead),
            pl.BlockSpec((WINDOW, d), behind),
            pl.BlockSpec((ms, d), const),
            pl.BlockSpec((1, d), const),
            pl.BlockSpec(memory_space=pl.ANY),
            pl.BlockSpec(memory_space=pl.ANY),
            pl.BlockSpec((WINDOW, LANES), pos),
            pl.BlockSpec((WINDOW, LANES), pos),
            pl.BlockSpec((1, LANES), const),
            pl.BlockSpec((1, LANES), const),
            pl.BlockSpec((CONV_WIDTH, CONV_CH), const),
            pl.BlockSpec((1, CONV_CH), const),
            pl.BlockSpec((1, CONV_CH), const),
            pl.BlockSpec((1, CONV_CH), const),
        ],
        out_specs=[
            pl.BlockSpec((WINDOW, d), behind),
            pl.BlockSpec((ms, IN_COLS), const),
            pl.BlockSpec((1, KV_DIM, WINDOW), per_batch),
            pl.BlockSpec((1, KV_DIM, WINDOW), per_batch),
            pl.BlockSpec((1, CONV_STATE, CONV_CH), per_batch),
            pl.BlockSpec(memory_space=pl.ANY),
        ],
        out_shape=[
            jax.ShapeDtypeStruct((steps * WINDOW, d), F32),
            jax.ShapeDtypeStruct((ms, IN_COLS), F32),
            jax.ShapeDtypeStruct((batch, KV_DIM, WINDOW), F32),
            jax.ShapeDtypeStruct((batch, KV_DIM, WINDOW), F32),
            jax.ShapeDtypeStruct((batch, CONV_STATE, CONV_CH), F32),
            jax.ShapeDtypeStruct(w_out.shape, BF16),
        ],
        scratch_shapes=[
            pltpu.VMEM(w_in.shape, BF16),
            pltpu.VMEM(w_out.shape, BF16),
            pltpu.VMEM((2, CAST_ROWS, stage_cols), F32),
            pltpu.SemaphoreType.DMA((2,)),
            pltpu.SemaphoreType.DMA((1,)),
            pltpu.VMEM((WINDOW, IN_COLS), F32),
            pltpu.VMEM((WINDOW, IN_COLS), F32),
            pltpu.VMEM((WINDOW, ATTN_WIDTH + CONV_CH), BF16),
            pltpu.VMEM((WINDOW, KV_DIM), F32),
            pltpu.VMEM((N_KV_HEADS, 2 * WINDOW, LANES), BF16),
            pltpu.VMEM((N_KV_HEADS, 2 * WINDOW, LANES), BF16),
            pltpu.VMEM((N_KV_HEADS, VT_ROWS, 2 * WINDOW), BF16),
            pltpu.VMEM((CONV_PAD + WINDOW, CONV_CH), F32),
        ],
        compiler_params=pltpu.CompilerParams(
            dimension_semantics=("arbitrary",), vmem_limit_bytes=VMEM_LIMIT_BYTES),
        name="prompt_front",
    )(sinks, x, x, xs, g.reshape(1, d), w_in, w_out, cos4, sin4, qg2, kg2, cw, cb, lg, lb)


def _sample_mixer_kernel(sinks_ref, z_ref, xs_ref, ck_ref, cv_ref, st_ref, cos_ref, sin_ref,
                         qg_ref, kg_ref, cw_ref, cb_ref, lg_ref, lb_ref, wout_ref,
                         x1_ref, nk_ref, nv_ref, nc_ref, qexp_ref, knew_ref, a_ref):
    bs = z_ref.shape[0]
    cos2 = cos_ref[...]
    sin2 = sin_ref[...]

    q = jnp.concatenate([z_ref[:, h * HEAD_DIM:(h + 1) * HEAD_DIM] for h in range(N_HEADS)], axis=0)
    qn = _norm_rope(q, qg_ref[...], cos2, sin2).astype(BF16)
    tile_r = lax.broadcasted_iota(jnp.int32, (HEAD_DIM, KV_DIM), 0)
    tile_c = lax.broadcasted_iota(jnp.int32, (HEAD_DIM, KV_DIM), 1)
    tile = jnp.where((tile_c & (HEAD_DIM - 1)) == tile_r, 1.0, 0.0).astype(BF16)
    qt = jnp.dot(qn, tile, preferred_element_type=F32)
    q_row = lax.broadcasted_iota(jnp.int32, (N_HEADS * bs, KV_DIM), 0)
    q_col = lax.broadcasted_iota(jnp.int32, (N_HEADS * bs, KV_DIM), 1)
    own = (q_row // (bs * GROUP)) == (q_col // HEAD_DIM)
    qexp = jnp.where(own, qt, 0.0)
    for half in range(KV_DIM // LANES):
        qexp_ref[half] = qexp[:, half * LANES:(half + 1) * LANES]

    k = jnp.concatenate([z_ref[:, K_OFF + h * HEAD_DIM:K_OFF + (h + 1) * HEAD_DIM]
                         for h in range(N_KV_HEADS)], axis=0)
    kn = _norm_rope(k, kg_ref[...], cos2, sin2)
    for h in range(N_KV_HEADS):
        knew_ref[:, h * HEAD_DIM:(h + 1) * HEAD_DIM] = kn[h * bs:(h + 1) * bs]
    knew = knew_ref[...]
    vnew = z_ref[:, V_OFF:V_OFF + KV_DIM]
    knew_r = knew.astype(BF16).astype(F32)
    vnew_r = vnew.astype(BF16).astype(F32)
    pad = jnp.zeros((WINDOW - bs, KV_DIM), F32)
    knew_t = jnp.concatenate([knew, pad], axis=0).T
    vnew_t = jnp.concatenate([vnew, pad], axis=0).T
    newest = lax.broadcasted_iota(jnp.int32, (KV_DIM, WINDOW), 1) == WINDOW - 1

    sink_col = jnp.concatenate([jnp.full((1, 1), sinks_ref[hh], F32) for hh in range(N_HEADS)], axis=0)
    key_idx = lax.broadcasted_iota(jnp.int32, (N_HEADS, WINDOW), 1)
    key_ok = key_idx >= 1
    o_row = lax.broadcasted_iota(jnp.int32, (N_HEADS, KV_DIM), 0)
    o_col = lax.broadcasted_iota(jnp.int32, (N_HEADS, KV_DIM), 1)
    o_own = (o_row // GROUP) == (o_col // HEAD_DIM)
    fold_r = lax.broadcasted_iota(jnp.int32, (KV_DIM, LANES), 0)
    fold_c = lax.broadcasted_iota(jnp.int32, (KV_DIM, LANES), 1)
    fold = jnp.where((fold_r & (HEAD_DIM - 1)) == fold_c, 1.0, 0.0).astype(BF16)

    for b in range(bs):
        heads_of_b = pl.ds(b, N_HEADS, stride=bs)
        qb = jnp.concatenate([qexp_ref[half, heads_of_b, :] for half in range(KV_DIM // LANES)], axis=1)
        kc = ck_ref[b]
        vc = cv_ref[b]
        s = jnp.dot(qb.astype(BF16), kc.astype(BF16), preferred_element_type=F32)
        s = jnp.where(key_ok, s * (HEAD_DIM ** -0.5), NEG)
        s_new = jnp.sum(qb * knew_r[b:b + 1, :], axis=-1, keepdims=True) * (HEAD_DIM ** -0.5)
        m = jnp.maximum(jnp.maximum(jnp.max(s, axis=-1, keepdims=True), s_new), sink_col)
        p = jnp.exp(s - m)
        p_new = jnp.exp(s_new - m)
        den = jnp.sum(p, axis=-1, keepdims=True) + p_new + jnp.exp(sink_col - m)
        p = p / den
        p_new = (p_new / den).astype(BF16).astype(F32)
        o = lax.dot_general(p.astype(BF16), vc.astype(BF16), (((1,), (1,)), ((), ())),
                            preferred_element_type=F32)
        o = o + p_new * vnew_r[b:b + 1, :]
        o = jnp.where(o_own, o, 0.0).astype(BF16)
        a_ref[heads_of_b, :] = jnp.dot(o, fold, preferred_element_type=F32)

        nk_ref[b] = jnp.where(newest, pltpu.roll(knew_t, WINDOW - 1 - b, 1), pltpu.roll(kc, WINDOW - 1, 1))
        nv_ref[b] = jnp.where(newest, pltpu.roll(vnew_t, WINDOW - 1 - b, 1), pltpu.roll(vc, WINDOW - 1, 1))

    u = z_ref[:, VAL_OFF:VAL_OFF + CONV_CH] * jax.nn.sigmoid(z_ref[:, GATE_OFF:GATE_OFF + CONV_CH])
    y = cb_ref[...] + u * cw_ref[CONV_STATE:CONV_WIDTH, :]
    for j in range(CONV_STATE):
        y = y + st_ref[j] * cw_ref[j:j + 1, :]
        nc_ref[j] = st_ref[j + 1] if j + 1 < CONV_STATE else u
    c = _layernorm_swish(y, lg_ref[...], lb_ref[...])

    x1 = xs_ref[...] + jnp.dot(c.astype(BF16), wout_ref[ATTN_WIDTH:, :], preferred_element_type=F32)
    for h in range(N_HEADS):
        x1 = x1 + jnp.dot(a_ref[h * bs:(h + 1) * bs, 0:HEAD_DIM].astype(BF16),
                          wout_ref[h * HEAD_DIM:(h + 1) * HEAD_DIM, :], preferred_element_type=F32)
    x1_ref[...] = x1


def _sample_mixer(zs, xs, cache_k, cache_v, state_t, cos2, sin2, qg, kg, sinks, cw, cb, lg, lb, w_out, *, bs):
    nsamp, d = xs.shape
    const = lambda i: (0, 0)
    step = lambda i: (i, 0)
    step3 = lambda i: (i, 0, 0)
    taps = lambda i: (0, i, 0)
    return pl.pallas_call(
        _sample_mixer_kernel,
        grid=(nsamp // bs,),
        in_specs=[
            pl.BlockSpec(memory_space=pltpu.SMEM),
            pl.BlockSpec((bs, IN_COLS), step),
            pl.BlockSpec((bs, d), step),
            pl.BlockSpec((bs, KV_DIM, WINDOW), step3),
            pl.BlockSpec((bs, KV_DIM, WINDOW), step3),
            pl.BlockSpec((CONV_STATE, bs, CONV_CH), taps),
            pl.BlockSpec((1, HEAD_DIM), const),
            pl.BlockSpec((1, HEAD_DIM), const),
            pl.BlockSpec((1, HEAD_DIM), const),
            pl.BlockSpec((1, HEAD_DIM), const),
            pl.BlockSpec((CONV_WIDTH, CONV_CH), const),
            pl.BlockSpec((1, CONV_CH), const),
            pl.BlockSpec((1, CONV_CH), const),
            pl.BlockSpec((1, CONV_CH), const),
            pl.BlockSpec((ATTN_WIDTH + CONV_CH, d), const, pipeline_mode=pl.Buffered(1)),
        ],
        out_specs=[
            pl.BlockSpec((bs, d), step),
            pl.BlockSpec((bs, KV_DIM, WINDOW), step3),
            pl.BlockSpec((bs, KV_DIM, WINDOW), step3),
            pl.BlockSpec((CONV_STATE, bs, CONV_CH), taps),
        ],
        out_shape=[
            jax.ShapeDtypeStruct((nsamp, d), F32),
            jax.ShapeDtypeStruct((nsamp, KV_DIM, WINDOW), F32),
            jax.ShapeDtypeStruct((nsamp, KV_DIM, WINDOW), F32),
            jax.ShapeDtypeStruct((CONV_STATE, nsamp, CONV_CH), F32),
        ],
        scratch_shapes=[
            pltpu.VMEM((KV_DIM // LANES, N_HEADS * bs, LANES), F32),
            pltpu.VMEM((bs, KV_DIM), F32),
            pltpu.VMEM((N_HEADS * bs, LANES), F32),
        ],
        compiler_params=pltpu.CompilerParams(
            dimension_semantics=("arbitrary",), vmem_limit_bytes=VMEM_LIMIT_BYTES),
        name="sample_mixer",
    )(sinks, zs, xs, cache_k, cache_v, state_t, cos2, sin2, qg, kg, cw, cb, lg, lb, w_out)


def _rope_tables(pos, heads_per_row=1):
    reps = 2 * heads_per_row
    inv = jnp.tile(ROPE_THETA ** (-jnp.arange(HALF, dtype=F32) / HALF), reps)
    sign = jnp.tile(jnp.concatenate([-jnp.ones((HALF,), F32), jnp.ones((HALF,), F32)]), heads_per_row)
    ang = pos.astype(F32)[:, None] * inv[None, :]
    return jnp.cos(ang), jnp.sin(ang) * sign[None, :]


def _to_feature_major(cache):
    rows = cache.shape[0]
    return jnp.transpose(cache, (0, 2, 3, 1)).reshape(rows, KV_DIM, WINDOW)


def _from_feature_major(cache_t):
    rows = cache_t.shape[0]
    return jnp.transpose(cache_t.reshape(rows, N_KV_HEADS, HEAD_DIM, WINDOW), (0, 3, 1, 2))


def kernel(x_prompt, x_sample, cache_k, cache_v, state_conv, norm_mix_g, w_in, q_norm_g, k_norm_g,
           sinks, conv_w, conv_b, conv_ln_g, conv_ln_b, w_out, norm_mlp_g, w_up, w_down):
    batch, seq, d = x_prompt.shape
    nsamp, dec_seq, _ = x_sample.shape
    depth = w_in.shape[0]
    assert dec_seq == 1 and seq % WINDOW == 0 and cache_k.shape[2] == WINDOW

    cos4, sin4 = _rope_tables(jnp.arange(seq), heads_per_row=LANES // HEAD_DIM)
    cos_s, sin_s = _rope_tables(PAST_LEN + jnp.arange(dec_seq))

    xp = x_prompt.reshape(batch * seq, d)
    xs = x_sample.reshape(nsamp * dec_seq, d)
    kp, vp, cp, ksm, vsm, csm = [], [], [], [], [], []
    for l in range(depth):
        qg = q_norm_g[l].reshape(1, HEAD_DIM)
        kg = k_norm_g[l].reshape(1, HEAD_DIM)
        cb = conv_b[l].reshape(1, CONV_CH)
        lg = conv_ln_g[l].reshape(1, CONV_CH)
        lb = conv_ln_b[l].reshape(1, CONV_CH)
        qg2 = jnp.tile(qg * (HEAD_DIM ** -0.5), (1, LANES // HEAD_DIM))
        kg2 = jnp.tile(kg, (1, LANES // HEAD_DIM))
        x1, zs, nk, nv, nc, w_out_bf = _prompt_front(xp, xs, norm_mix_g[l], w_in[l], w_out[l], batch, seq,
                                                     cos4, sin4, qg2, kg2, sinks[l], conv_w[l], cb, lg, lb)
        kp.append(_from_feature_major(nk))
        vp.append(_from_feature_major(nv))
        cp.append(nc)

        x1s, nk, nv, nc = _sample_mixer(
            zs, xs, _to_feature_major(cache_k[l]), _to_feature_major(cache_v[l]),
            jnp.transpose(state_conv[l], (1, 0, 2)), cos_s, sin_s, qg, kg, sinks[l], conv_w[l], cb, lg, lb,
            w_out_bf, bs=SAMPLE_BLOCK)
        ksm.append(_from_feature_major(nk))
        vsm.append(_from_feature_major(nv))
        csm.append(jnp.transpose(nc, (1, 0, 2)))

        xp, xs = _mlp(x1, x1s, norm_mlp_g[l], w_up[l], w_down[l])

    return (xp.reshape(batch, seq, d), xs.reshape(nsamp, dec_seq, d),
            jnp.stack(kp), jnp.stack(vp), jnp.stack(cp),
            jnp.stack(ksm), jnp.stack(vsm), jnp.stack(csm))
```

```python
import functools

import jax
import jax.numpy as jnp
from jax import lax
from jax.experimental import pallas as pl
from jax.experimental.pallas import tpu as pltpu

D_MODEL = 2048
ATTN_WIDTH = 1024
CONV_CH = 1024
HEAD_DIM = 64
HALF = HEAD_DIM // 2
N_HEADS = 16
N_KV_HEADS = 4
GROUP = N_HEADS // N_KV_HEADS
KV_DIM = N_KV_HEADS * HEAD_DIM
WINDOW = 128
CONV_WIDTH = 31
CONV_STATE = CONV_WIDTH - 1
D_FF = 4 * D_MODEL
ROPE_THETA = 10000.0
EPS = 1e-6
IN_COLS = ATTN_WIDTH + 2 * KV_DIM + 2 * CONV_CH
NEG = -1e30
PAST_LEN = 16384

K_OFF = ATTN_WIDTH
V_OFF = ATTN_WIDTH + KV_DIM
VAL_OFF = ATTN_WIDTH + 2 * KV_DIM
GATE_OFF = VAL_OFF + CONV_CH

VMEM_LIMIT_BYTES = 56 * 1024 * 1024
SUBLANES = 8
LANES = 128
BF16_ROWS = 16
VT_ROWS = HEAD_DIM + BF16_ROWS
CONV_PAD = 32

MLP_ROW_BLOCK = 1024
MLP_TILE = 512
CAST_ROWS = 256
SAMPLE_BLOCK = 16
PROJ_CHUNK = 256
SIDE_JOBS_PER_SOFTMAX = 3

F32 = jnp.float32
BF16 = jnp.bfloat16


def _rms_rows(x, g):
    ms = jnp.mean(x * x, axis=-1, keepdims=True)
    return x * lax.rsqrt(ms + EPS) * g


def _norm_dot(x, g, w):
    return jnp.dot(_rms_rows(x, g).astype(BF16), w, preferred_element_type=F32)


def _mlp_kernel(x_ref, xs_ref, g_ref, wu_ref, wd_ref, o_ref, os_ref, hm_ref):
    i, j = pl.program_id(0), pl.program_id(1)
    tm = x_ref.shape[0]

    @pl.when(j == 0)
    def _():
        x = x_ref[...]
        hm_ref[0:tm, :] = _rms_rows(x, g_ref[...]).astype(BF16)
        o_ref[...] = x

    @pl.when((i == 0) & (j == 0))
    def _():
        xs = xs_ref[...]
        hm_ref[tm:, :] = _rms_rows(xs, g_ref[...]).astype(BF16)
        os_ref[...] = xs

    def up_down(hm):
        h = jnp.dot(hm, wu_ref[...].astype(BF16), preferred_element_type=F32)
        h = jnp.square(jnp.maximum(h, 0.0)).astype(BF16)
        return jnp.dot(h, wd_ref[...].astype(BF16), preferred_element_type=F32)

    @pl.when(i == 0)
    def _():
        y = up_down(hm_ref[...])
        o_ref[...] += y[0:tm]
        os_ref[...] += y[tm:]

    @pl.when(i > 0)
    def _():
        o_ref[...] += up_down(hm_ref[0:tm, :])


def _mlp(x, xs, g, wu, wd):
    m, d = x.shape
    ms = xs.shape[0]
    f = wu.shape[1]
    tm, tf = MLP_ROW_BLOCK, MLP_TILE
    return pl.pallas_call(
        _mlp_kernel,
        grid=(m // tm, f // tf),
        in_specs=[
            pl.BlockSpec((tm, d), lambda i, j: (i, 0)),
            pl.BlockSpec((ms, d), lambda i, j: (0, 0)),
            pl.BlockSpec((1, d), lambda i, j: (0, 0)),
            pl.BlockSpec((d, tf), lambda i, j: (0, j)),
            pl.BlockSpec((tf, d), lambda i, j: (j, 0)),
        ],
        out_specs=[
            pl.BlockSpec((tm, d), lambda i, j: (i, 0)),
            pl.BlockSpec((ms, d), lambda i, j: (0, 0)),
        ],
        out_shape=[jax.ShapeDtypeStruct((m, d), F32), jax.ShapeDtypeStruct((ms, d), F32)],
        scratch_shapes=[pltpu.VMEM((tm + ms, d), BF16)],
        compiler_params=pltpu.CompilerParams(
            dimension_semantics=("arbitrary", "arbitrary"), vmem_limit_bytes=VMEM_LIMIT_BYTES),
        name="mlp",
    )(x, xs, g.reshape(1, d), wu, wd)


def _rotate_half(x):
    return jnp.concatenate([x[:, HALF:], x[:, :HALF]], axis=-1)


def _norm_rope(x, g, cos2, sin2):
    xn = _rms_rows(x, g)
    return xn * cos2 + _rotate_half(xn) * sin2


def _layernorm_swish(c, g, b):
    mu = jnp.mean(c, axis=-1, keepdims=True)
    xc = c - mu
    y = xc * lax.rsqrt(jnp.mean(xc * xc, axis=-1, keepdims=True) + EPS)
    y = y * g + b
    return y * jax.nn.sigmoid(y)


def _conv_taps_by_row_phase():
    groups = [[] for _ in range(SUBLANES)]
    for j in range(CONV_WIDTH):
        e = j + CONV_PAD - CONV_STATE
        groups[e % SUBLANES].append((j, e // SUBLANES))
    return groups


def _mixer_carry(n, klo_ref, khi_ref, vt_ref, ubuf_ref):
    @pl.when(n == 0)
    def _():
        klo_ref[:, 0:WINDOW, :] = jnp.zeros((N_KV_HEADS, WINDOW, LANES), BF16)
        khi_ref[:, 0:WINDOW, :] = jnp.zeros((N_KV_HEADS, WINDOW, LANES), BF16)
        vt_ref[:, 0:HEAD_DIM, 0:WINDOW] = jnp.zeros((N_KV_HEADS, HEAD_DIM, WINDOW), BF16)
        vt_ref[:, HEAD_DIM:, :] = jnp.ones((N_KV_HEADS, VT_ROWS - HEAD_DIM, 2 * WINDOW), BF16)
        ubuf_ref[0:CONV_PAD, :] = jnp.zeros((CONV_PAD, CONV_CH), F32)

    @pl.when(n > 0)
    def _():
        klo_ref[:, 0:WINDOW, :] = klo_ref[:, WINDOW:2 * WINDOW, :]
        khi_ref[:, 0:WINDOW, :] = khi_ref[:, WINDOW:2 * WINDOW, :]
        vt_ref[:, 0:HEAD_DIM, 0:WINDOW] = vt_ref[:, 0:HEAD_DIM, WINDOW:2 * WINDOW]
        ubuf_ref[0:CONV_PAD, :] = ubuf_ref[WINDOW:WINDOW + CONV_PAD, :]


def _mixer_block(n, z_ref, mix_ref, kr_ref, sinks_ref, cos_ref, sin_ref, qg_ref, kg_ref, cw_ref, cb_ref,
                 lg_ref, lb_ref, klo_ref, khi_ref, vt_ref, ubuf_ref, side_jobs=()):
    n_q_slabs = ATTN_WIDTH // LANES
    n_k_slabs = KV_DIM // LANES
    side = iter(side_jobs)

    def tick(k=1):
        for _ in range(k):
            job = next(side, None)
            if job is not None:
                job()

    u = z_ref[:, VAL_OFF:VAL_OFF + CONV_CH] * jax.nn.sigmoid(z_ref[:, GATE_OFF:GATE_OFF + CONV_CH])
    ubuf_ref[CONV_PAD:CONV_PAD + WINDOW, :] = u

    lane = lax.broadcasted_iota(jnp.int32, (WINDOW, LANES), 1)
    first_head = lane < HEAD_DIM
    first_half = (lane & HALF) == 0

    n_slabs = n_q_slabs + n_k_slabs
    x = jnp.concatenate([z_ref[:, m * LANES:(m + 1) * LANES] for m in range(n_slabs)], axis=0)
    sq = x * x
    sq_hi = sq.astype(BF16)
    sq_lo = (sq - sq_hi.astype(F32)).astype(BF16)
    er = lax.broadcasted_iota(jnp.int32, (LANES, LANES), 0)
    ec = lax.broadcasted_iota(jnp.int32, (LANES, LANES), 1)
    head_mean = jnp.where((er < HEAD_DIM) == (ec < HEAD_DIM), 1.0 / HEAD_DIM, 0.0).astype(BF16)
    ms = (jnp.dot(sq_hi, head_mean, preferred_element_type=F32)
          + jnp.dot(sq_lo, head_mean, preferred_element_type=F32))
    xn = x * lax.rsqrt(ms + EPS)

    cos4 = cos_ref[...]
    sin4 = sin_ref[...]

    def rope(xs):
        rot = jnp.where(first_half, pltpu.roll(xs, LANES - HALF, 1), pltpu.roll(xs, HALF, 1))
        return xs * cos4 + rot * sin4

    q_slabs = [rope(xn[m * WINDOW:(m + 1) * WINDOW] * qg_ref[...]).astype(BF16) for m in range(n_q_slabs)]

    zero = jnp.zeros((WINDOW, LANES), F32)
    for ks in range(n_k_slabs):
        r0 = (n_q_slabs + ks) * WINDOW
        kr = rope(xn[r0:r0 + WINDOW] * kg_ref[...])
        kr_ref[:, ks * LANES:(ks + 1) * LANES] = kr
        kr_swapped = pltpu.roll(kr, HEAD_DIM, 1)
        klo_ref[2 * ks, WINDOW:2 * WINDOW, :] = jnp.where(first_head, kr, zero).astype(BF16)
        khi_ref[2 * ks, WINDOW:2 * WINDOW, :] = jnp.where(first_head, zero, kr_swapped).astype(BF16)
        klo_ref[2 * ks + 1, WINDOW:2 * WINDOW, :] = jnp.where(first_head, kr_swapped, zero).astype(BF16)
        khi_ref[2 * ks + 1, WINDOW:2 * WINDOW, :] = jnp.where(first_head, zero, kr).astype(BF16)

        vs_t = z_ref[:, V_OFF + ks * LANES:V_OFF + (ks + 1) * LANES].T.astype(BF16)
        vt_ref[2 * ks, 0:HEAD_DIM, WINDOW:2 * WINDOW] = vs_t[0:HEAD_DIM]
        vt_ref[2 * ks + 1, 0:HEAD_DIM, WINDOW:2 * WINDOW] = vs_t[HEAD_DIM:2 * HEAD_DIM]

    key = lax.broadcasted_iota(jnp.int32, (2 * WINDOW, WINDOW), 0)
    qry = lax.broadcasted_iota(jnp.int32, (2 * WINDOW, WINDOW), 1)
    diff = qry + WINDOW - key
    band = (diff >= 0) & (diff <= jnp.where(n > 0, WINDOW - 1, qry))
    bias = jnp.where(band, 0.0, NEG)
    bias2 = jnp.concatenate([bias, bias], axis=1)

    def conv_chunk(c):
        cl = slice(c * LANES, (c + 1) * LANES)
        y = cb_ref[:, cl]
        for phase, taps in enumerate(_conv_taps_by_row_phase()):
            rows_needed = WINDOW + (SUBLANES if phase else 0)
            part = None
            for j, tile in taps:
                term = cw_ref[j:j + 1, cl] * ubuf_ref[tile * SUBLANES:tile * SUBLANES + rows_needed, cl]
                part = term if part is None else part + term
            if phase:
                part = pltpu.roll(part, rows_needed - phase, 0)[0:WINDOW]
            y = y + part
        return y

    conv_per_head = CONV_CH // LANES // N_KV_HEADS
    y_chunks = []
    for h in range(N_KV_HEADS):
        q2 = jnp.concatenate([q_slabs[2 * h], q_slabs[2 * h + 1]], axis=0)
        normed_t = []
        for second, k_ref in ((0, klo_ref), (1, khi_ref)):
            s_t = lax.dot_general(k_ref[h], q2, (((1,), (1,)), ((), ())), preferred_element_type=F32)
            s_t = s_t + bias2
            sink_row = jnp.concatenate(
                [jnp.full((1, WINDOW), sinks_ref[GROUP * h + 2 * r + second], F32) for r in range(2)], axis=1)
            m = jnp.maximum(jnp.max(s_t, axis=0, keepdims=True), sink_row)
            p_t = jnp.exp(s_t - m).astype(BF16)
            o_t = jnp.dot(vt_ref[h], p_t, preferred_element_type=F32)
            den = o_t[HEAD_DIM:HEAD_DIM + 1] + jnp.exp(sink_row - m)
            normed_t.append(o_t[0:HEAD_DIM] * (1.0 / den))
            tick(SIDE_JOBS_PER_SOFTMAX)
        for r in range(2):
            a_t = jnp.concatenate([normed_t[0][:, r * WINDOW:(r + 1) * WINDOW],
                                   normed_t[1][:, r * WINDOW:(r + 1) * WINDOW]], axis=0)
            m_out = 2 * h + r
            mix_ref[:, m_out * LANES:(m_out + 1) * LANES] = a_t.T.astype(BF16)
        y_chunks += [conv_chunk(conv_per_head * h + i) for i in range(conv_per_head)]

    conv = jnp.concatenate(y_chunks, axis=-1)
    mix_ref[:, ATTN_WIDTH:] = _layernorm_swish(conv, lg_ref[...], lb_ref[...]).astype(BF16)
    tick(len(side_jobs))


def _weight_chunk_copy(w_hbm, stage_ref, sem, c, slot):
    cols = w_hbm.shape[1]
    return pltpu.make_async_copy(w_hbm.at[pl.ds(c * CAST_ROWS, CAST_ROWS), :],
                                 stage_ref.at[slot, :, pl.ds(0, cols)], sem.at[slot])


def _load_weight_bf16(w_hbm, w_ref, stage_ref, sem):
    n_chunks = w_ref.shape[0] // CAST_ROWS
    cols = w_ref.shape[1]
    _weight_chunk_copy(w_hbm, stage_ref, sem, 0, 0).start()
    for c in range(n_chunks):
        slot = c % 2
        if c + 1 < n_chunks:
            _weight_chunk_copy(w_hbm, stage_ref, sem, c + 1, 1 - slot).start()
        _weight_chunk_copy(w_hbm, stage_ref, sem, c, slot).wait()
        w_ref[c * CAST_ROWS:(c + 1) * CAST_ROWS, :] = stage_ref[slot, :, 0:cols].astype(BF16)


def _front_kernel(sinks_ref, x_next_ref, x_res_ref, xs_ref, g_ref, win_hbm, wout_hbm, cos_ref, sin_ref,
                  qg_ref, kg_ref, cw_ref, cb_ref, lg_ref, lb_ref,
                  x1_ref, zs_ref, nk_ref, nv_ref, nc_ref, wout_bf_hbm,
                  win_ref, wout_ref, stage_ref, load_sem, export_sem,
                  z_next_ref, z_ref, mix_ref, kr_ref, klo_ref, khi_ref, vt_ref, ubuf_ref, *, nb):
    t = pl.program_id(0)
    drain = pl.num_programs(0) - 1
    n = lax.rem(t, nb)
    export = pltpu.make_async_copy(wout_ref, wout_bf_hbm, export_sem.at[0])

    @pl.when(t == 0)
    def _():
        _load_weight_bf16(win_hbm, win_ref, stage_ref, load_sem)
        _load_weight_bf16(wout_hbm, wout_ref, stage_ref, load_sem)
        export.start()
        z_ref[...] = _norm_dot(x_res_ref[...], g_ref[...], win_ref[...])
        zs_ref[...] = _norm_dot(xs_ref[...], g_ref[...], win_ref[...])
        mix_ref[...] = jnp.zeros(mix_ref.shape, BF16)

    @pl.when(t < drain)
    def _():
        _mixer_carry(n, klo_ref, khi_ref, vt_ref, ubuf_ref)

    @pl.when(t < drain)
    def _():
        xn_next = _rms_rows(x_next_ref[...], g_ref[...]).astype(BF16)
        mix_prev = mix_ref[...]
        jobs = []
        for c in range(IN_COLS // PROJ_CHUNK):
            def in_job(cols=slice(c * PROJ_CHUNK, (c + 1) * PROJ_CHUNK)):
                z_next_ref[:, cols] = jnp.dot(xn_next, win_ref[:, cols], preferred_element_type=F32)
            jobs.append(in_job)
        for c in range(x1_ref.shape[1] // PROJ_CHUNK):
            def out_job(cols=slice(c * PROJ_CHUNK, (c + 1) * PROJ_CHUNK)):
                x1_ref[:, cols] = x_res_ref[:, cols] + jnp.dot(mix_prev, wout_ref[:, cols],
                                                               preferred_element_type=F32)
            jobs.append(out_job)
        _mixer_block(n, z_ref, mix_ref, kr_ref, sinks_ref, cos_ref, sin_ref, qg_ref, kg_ref, cw_ref,
                     cb_ref, lg_ref, lb_ref, klo_ref, khi_ref, vt_ref, ubuf_ref, side_jobs=jobs)

    @pl.when((n == nb - 1) & (t < drain))
    def _():
        nk_ref[0] = kr_ref[...].T
        nv_ref[0] = z_ref[:, V_OFF:V_OFF + KV_DIM].T
        nc_ref[0] = ubuf_ref[CONV_PAD + WINDOW - CONV_STATE:CONV_PAD + WINDOW, :]

    @pl.when(t < drain)
    def _():
        z_ref[...] = z_next_ref[...]

    @pl.when(t == drain)
    def _():
        x1_ref[...] = x_res_ref[...] + jnp.dot(mix_ref[...], wout_ref[...], preferred_element_type=F32)
        export.wait()


def _prompt_front(x, xs, g, w_in, w_out, batch, seq, cos4, sin4, qg2, kg2, sinks, cw, cb, lg, lb):
    nb = seq // WINDOW
    steps = batch * nb
    d = x.shape[1]
    ms = xs.shape[0]
    const = lambda t: (0, 0)
    behind = lambda t: (jnp.maximum(t - 1, 0), 0)
    ahead = lambda t: (jnp.minimum(t + 1, steps - 1), 0)
    pos = lambda t: (lax.rem(t, nb), 0)
    per_batch = lambda t: (jnp.minimum(t // nb, batch - 1), 0, 0)
    assert w_in.shape[0] % CAST_ROWS == 0 and w_out.shape[0] % CAST_ROWS == 0
    stage_cols = max(w_in.shape[1], w_out.shape[1])
    return pl.pallas_call(
        functools.partial(_front_kernel, nb=nb),
        grid=(steps + 1,),
        in_specs=[
            pl.BlockSpec(memory_space=pltpu.SMEM),
            pl.BlockSpec((WINDOW, d), ahead),
            pl.BlockSpec((WINDOW, d), behind),
            pl.BlockSpec((ms, d), const),
            pl.BlockSpec((1, d), const),
            pl.BlockSpec(memory_space=pl.ANY),
            pl.BlockSpec(memory_space=pl.ANY),
            pl.BlockSpec((WINDOW, LANES), pos),
            pl.BlockSpec((WINDOW, LANES), pos),
            pl.BlockSpec((1, LANES), const),
            pl.BlockSpec((1, LANES), const),
            pl.BlockSpec((CONV_WIDTH, CONV_CH), const),
            pl.BlockSpec((1, CONV_CH), const),
            pl.BlockSpec((1, CONV_CH), const),
            pl.BlockSpec((1, CONV_CH), const),
        ],
        out_specs=[
            pl.BlockSpec((WINDOW, d), behind),
            pl.BlockSpec((ms, IN_COLS), const),
            pl.BlockSpec((1, KV_DIM, WINDOW), per_batch),
            pl.BlockSpec((1, KV_DIM, WINDOW), per_batch),
            pl.BlockSpec((1, CONV_STATE, CONV_CH), per_batch),
            pl.BlockSpec(memory_space=pl.ANY),
        ],
        out_shape=[
            jax.ShapeDtypeStruct((steps * WINDOW, d), F32),
            jax.ShapeDtypeStruct((ms, IN_COLS), F32),
            jax.ShapeDtypeStruct((batch, KV_DIM, WINDOW), F32),
            jax.ShapeDtypeStruct((batch, KV_DIM, WINDOW), F32),
            jax.ShapeDtypeStruct((batch, CONV_STATE, CONV_CH), F32),
            jax.ShapeDtypeStruct(w_out.shape, BF16),
        ],
        scratch_shapes=[
            pltpu.VMEM(w_in.shape, BF16),
            pltpu.VMEM(w_out.shape, BF16),
            pltpu.VMEM((2, CAST_ROWS, stage_cols), F32),
            pltpu.SemaphoreType.DMA((2,)),
            pltpu.SemaphoreType.DMA((1,)),
            pltpu.VMEM((WINDOW, IN_COLS), F32),
            pltpu.VMEM((WINDOW, IN_COLS), F32),
            pltpu.VMEM((WINDOW, ATTN_WIDTH + CONV_CH), BF16),
            pltpu.VMEM((WINDOW, KV_DIM), F32),
            pltpu.VMEM((N_KV_HEADS, 2 * WINDOW, LANES), BF16),
            pltpu.VMEM((N_KV_HEADS, 2 * WINDOW, LANES), BF16),
            pltpu.VMEM((N_KV_HEADS, VT_ROWS, 2 * WINDOW), BF16),
            pltpu.VMEM((CONV_PAD + WINDOW, CONV_CH), F32),
        ],
        compiler_params=pltpu.CompilerParams(
            dimension_semantics=("arbitrary",), vmem_limit_bytes=VMEM_LIMIT_BYTES),
        name="prompt_front",
    )(sinks, x, x, xs, g.reshape(1, d), w_in, w_out, cos4, sin4, qg2, kg2, cw, cb, lg, lb)


def _sample_mixer_kernel(sinks_ref, z_ref, xs_ref, ck_ref, cv_ref, st_ref, cos_ref, sin_ref,
                         qg_ref, kg_ref, cw_ref, cb_ref, lg_ref, lb_ref, wout_ref,
                         x1_ref, nk_ref, nv_ref, nc_ref, qexp_ref, knew_ref, a_ref):
    bs = z_ref.shape[0]
    cos2 = cos_ref[...]
    sin2 = sin_ref[...]

    q = jnp.concatenate([z_ref[:, h * HEAD_DIM:(h + 1) * HEAD_DIM] for h in range(N_HEADS)], axis=0)
    qn = _norm_rope(q, qg_ref[...], cos2, sin2).astype(BF16)
    tile_r = lax.broadcasted_iota(jnp.int32, (HEAD_DIM, KV_DIM), 0)
    tile_c = lax.broadcasted_iota(jnp.int32, (HEAD_DIM, KV_DIM), 1)
    tile = jnp.where((tile_c & (HEAD_DIM - 1)) == tile_r, 1.0, 0.0).astype(BF16)
    qt = jnp.dot(qn, tile, preferred_element_type=F32)
    q_row = lax.broadcasted_iota(jnp.int32, (N_HEADS * bs, KV_DIM), 0)
    q_col = lax.broadcasted_iota(jnp.int32, (N_HEADS * bs, KV_DIM), 1)
    own = (q_row // (bs * GROUP)) == (q_col // HEAD_DIM)
    qexp = jnp.where(own, qt, 0.0)
    for half in range(KV_DIM // LANES):
        qexp_ref[half] = qexp[:, half * LANES:(half + 1) * LANES]

    k = jnp.concatenate([z_ref[:, K_OFF + h * HEAD_DIM:K_OFF + (h + 1) * HEAD_DIM]
                         for h in range(N_KV_HEADS)], axis=0)
    kn = _norm_rope(k, kg_ref[...], cos2, sin2)
    for h in range(N_KV_HEADS):
        knew_ref[:, h * HEAD_DIM:(h + 1) * HEAD_DIM] = kn[h * bs:(h + 1) * bs]
    knew = knew_ref[...]
    vnew = z_ref[:, V_OFF:V_OFF + KV_DIM]
    knew_r = knew.astype(BF16).astype(F32)
    vnew_r = vnew.astype(BF16).astype(F32)
    pad = jnp.zeros((WINDOW - bs, KV_DIM), F32)
    knew_t = jnp.concatenate([knew, pad], axis=0).T
    vnew_t = jnp.concatenate([vnew, pad], axis=0).T
    newest = lax.broadcasted_iota(jnp.int32, (KV_DIM, WINDOW), 1) == WINDOW - 1

    n_rows = bs * N_HEADS
    q_rows, s_rows = [], []
    for b in range(bs):
        heads_of_b = pl.ds(b, N_HEADS, stride=bs)
        qb = jnp.concatenate([qexp_ref[half, heads_of_b, :] for half in range(KV_DIM // LANES)], axis=1)
        q_rows.append(qb)
        s_rows.append(jnp.dot(qb.astype(BF16), ck_ref[b].astype(BF16), preferred_element_type=F32))
    q_all = jnp.concatenate(q_rows, axis=0)
    s = jnp.concatenate(s_rows, axis=0) * (HEAD_DIM ** -0.5)
    knew_rows = jnp.concatenate([jnp.broadcast_to(knew_r[b:b + 1, :], (N_HEADS, KV_DIM)) for b in range(bs)], axis=0)
    s_new = jnp.sum(q_all * knew_rows, axis=-1, keepdims=True) * (HEAD_DIM ** -0.5)
    sink_col = jnp.concatenate([jnp.full((1, 1), sinks_ref[hh], F32) for hh in range(N_HEADS)] * bs, axis=0)
    key_ok = lax.broadcasted_iota(jnp.int32, (n_rows, WINDOW), 1) >= 1
    s = jnp.where(key_ok, s, NEG)
    m = jnp.maximum(jnp.maximum(jnp.max(s, axis=-1, keepdims=True), s_new), sink_col)
    p = jnp.exp(s - m)
    p_new = jnp.exp(s_new - m)
    den = jnp.sum(p, axis=-1, keepdims=True) + p_new + jnp.exp(sink_col - m)
    p = (p / den).astype(BF16)
    p_new = (p_new / den).astype(BF16).astype(F32)

    o_row = lax.broadcasted_iota(jnp.int32, (N_HEADS, KV_DIM), 0)
    o_col = lax.broadcasted_iota(jnp.int32, (N_HEADS, KV_DIM), 1)
    o_own = (o_row // GROUP) == (o_col // HEAD_DIM)
    fold_r = lax.broadcasted_iota(jnp.int32, (KV_DIM, LANES), 0)
    fold_c = lax.broadcasted_iota(jnp.int32, (KV_DIM, LANES), 1)
    fold = jnp.where((fold_r & (HEAD_DIM - 1)) == fold_c, 1.0, 0.0).astype(BF16)
    for b in range(bs):
        rows_b = slice(b * N_HEADS, (b + 1) * N_HEADS)
        kc = ck_ref[b]
        vc = cv_ref[b]
        o = lax.dot_general(p[rows_b], vc.astype(BF16), (((1,), (1,)), ((), ())), preferred_element_type=F32)
        o = o + p_new[rows_b] * vnew_r[b:b + 1, :]
        o = jnp.where(o_own, o, 0.0).astype(BF16)
        a_ref[pl.ds(b, N_HEADS, stride=bs), :] = jnp.dot(o, fold, preferred_element_type=F32)

        nk_ref[b] = jnp.where(newest, pltpu.roll(knew_t, WINDOW - 1 - b, 1), pltpu.roll(kc, WINDOW - 1, 1))
        nv_ref[b] = jnp.where(newest, pltpu.roll(vnew_t, WINDOW - 1 - b, 1), pltpu.roll(vc, WINDOW - 1, 1))

    u = z_ref[:, VAL_OFF:VAL_OFF + CONV_CH] * jax.nn.sigmoid(z_ref[:, GATE_OFF:GATE_OFF + CONV_CH])
    y = cb_ref[...] + u * cw_ref[CONV_STATE:CONV_WIDTH, :]
    for j in range(CONV_STATE):
        y = y + st_ref[j] * cw_ref[j:j + 1, :]
        nc_ref[j] = st_ref[j + 1] if j + 1 < CONV_STATE else u
    c = _layernorm_swish(y, lg_ref[...], lb_ref[...])

    x1 = xs_ref[...] + jnp.dot(c.astype(BF16), wout_ref[ATTN_WIDTH:, :], preferred_element_type=F32)
    for h in range(N_HEADS):
        x1 = x1 + jnp.dot(a_ref[h * bs:(h + 1) * bs, 0:HEAD_DIM].astype(BF16),
                          wout_ref[h * HEAD_DIM:(h + 1) * HEAD_DIM, :], preferred_element_type=F32)
    x1_ref[...] = x1


def _sample_mixer(zs, xs, cache_k, cache_v, state_t, cos2, sin2, qg, kg, sinks, cw, cb, lg, lb, w_out, *, bs):
    nsamp, d = xs.shape
    const = lambda i: (0, 0)
    step = lambda i: (i, 0)
    step3 = lambda i: (i, 0, 0)
    taps = lambda i: (0, i, 0)
    return pl.pallas_call(
        _sample_mixer_kernel,
        grid=(nsamp // bs,),
        in_specs=[
            pl.BlockSpec(memory_space=pltpu.SMEM),
            pl.BlockSpec((bs, IN_COLS), step),
            pl.BlockSpec((bs, d), step),
            pl.BlockSpec((bs, KV_DIM, WINDOW), step3),
            pl.BlockSpec((bs, KV_DIM, WINDOW), step3),
            pl.BlockSpec((CONV_STATE, bs, CONV_CH), taps),
            pl.BlockSpec((1, HEAD_DIM), const),
            pl.BlockSpec((1, HEAD_DIM), const),
            pl.BlockSpec((1, HEAD_DIM), const),
            pl.BlockSpec((1, HEAD_DIM), const),
            pl.BlockSpec((CONV_WIDTH, CONV_CH), const),
            pl.BlockSpec((1, CONV_CH), const),
            pl.BlockSpec((1, CONV_CH), const),
            pl.BlockSpec((1, CONV_CH), const),
            pl.BlockSpec((ATTN_WIDTH + CONV_CH, d), const, pipeline_mode=pl.Buffered(1)),
        ],
        out_specs=[
            pl.BlockSpec((bs, d), step),
            pl.BlockSpec((bs, KV_DIM, WINDOW), step3),
            pl.BlockSpec((bs, KV_DIM, WINDOW), step3),
            pl.BlockSpec((CONV_STATE, bs, CONV_CH), taps),
        ],
        out_shape=[
            jax.ShapeDtypeStruct((nsamp, d), F32),
            jax.ShapeDtypeStruct((nsamp, KV_DIM, WINDOW), F32),
            jax.ShapeDtypeStruct((nsamp, KV_DIM, WINDOW), F32),
            jax.ShapeDtypeStruct((CONV_STATE, nsamp, CONV_CH), F32),
        ],
        scratch_shapes=[
            pltpu.VMEM((KV_DIM // LANES, N_HEADS * bs, LANES), F32),
            pltpu.VMEM((bs, KV_DIM), F32),
            pltpu.VMEM((N_HEADS * bs, LANES), F32),
        ],
        compiler_params=pltpu.CompilerParams(
            dimension_semantics=("arbitrary",), vmem_limit_bytes=VMEM_LIMIT_BYTES),
        name="sample_mixer",
    )(sinks, zs, xs, cache_k, cache_v, state_t, cos2, sin2, qg, kg, cw, cb, lg, lb, w_out)


def _rope_tables(pos, heads_per_row=1):
    reps = 2 * heads_per_row
    inv = jnp.tile(ROPE_THETA ** (-jnp.arange(HALF, dtype=F32) / HALF), reps)
    sign = jnp.tile(jnp.concatenate([-jnp.ones((HALF,), F32), jnp.ones((HALF,), F32)]), heads_per_row)
    ang = pos.astype(F32)[:, None] * inv[None, :]
    return jnp.cos(ang), jnp.sin(ang) * sign[None, :]


def _to_feature_major(cache):
    rows = cache.shape[0]
    return jnp.transpose(cache, (0, 2, 3, 1)).reshape(rows, KV_DIM, WINDOW)


def _from_feature_major(cache_t):
    rows = cache_t.shape[0]
    return jnp.transpose(cache_t.reshape(rows, N_KV_HEADS, HEAD_DIM, WINDOW), (0, 3, 1, 2))


def kernel(x_prompt, x_sample, cache_k, cache_v, state_conv, norm_mix_g, w_in, q_norm_g, k_norm_g,
           sinks, conv_w, conv_b, conv_ln_g, conv_ln_b, w_out, norm_mlp_g, w_up, w_down):
    batch, seq, d = x_prompt.shape
    nsamp, dec_seq, _ = x_sample.shape
    depth = w_in.shape[0]
    assert dec_seq == 1 and seq % WINDOW == 0 and cache_k.shape[2] == WINDOW

    cos4, sin4 = _rope_tables(jnp.arange(seq), heads_per_row=LANES // HEAD_DIM)
    cos_s, sin_s = _rope_tables(PAST_LEN + jnp.arange(dec_seq))

    xp = x_prompt.reshape(batch * seq, d)
    xs = x_sample.reshape(nsamp * dec_seq, d)
    kp, vp, cp, ksm, vsm, csm = [], [], [], [], [], []
    for l in range(depth):
        qg = q_norm_g[l].reshape(1, HEAD_DIM)
        kg = k_norm_g[l].reshape(1, HEAD_DIM)
        cb = conv_b[l].reshape(1, CONV_CH)
        lg = conv_ln_g[l].reshape(1, CONV_CH)
        lb = conv_ln_b[l].reshape(1, CONV_CH)
        qg2 = jnp.tile(qg * (HEAD_DIM ** -0.5), (1, LANES // HEAD_DIM))
        kg2 = jnp.tile(kg, (1, LANES // HEAD_DIM))
        x1, zs, nk, nv, nc, w_out_bf = _prompt_front(xp, xs, norm_mix_g[l], w_in[l], w_out[l], batch, seq,
                                                     cos4, sin4, qg2, kg2, sinks[l], conv_w[l], cb, lg, lb)
        kp.append(_from_feature_major(nk))
        vp.append(_from_feature_major(nv))
        cp.append(nc)

        x1s, nk, nv, nc = _sample_mixer(
            zs, xs, _to_feature_major(cache_k[l]), _to_feature_major(cache_v[l]),
            jnp.transpose(state_conv[l], (1, 0, 2)), cos_s, sin_s, qg, kg, sinks[l], conv_w[l], cb, lg, lb,
            w_out_bf, bs=SAMPLE_BLOCK)
        ksm.append(_from_feature_major(nk))
        vsm.append(_from_feature_major(nv))
        csm.append(jnp.transpose(nc, (1, 0, 2)))

        xp, xs = _mlp(x1, x1s, norm_mlp_g[l], w_up[l], w_down[l])

    return (xp.reshape(batch, seq, d), xs.reshape(nsamp, dec_seq, d),
            jnp.stack(kp), jnp.stack(vp), jnp.stack(cp),
            jnp.stack(ksm), jnp.stack(vsm), jnp.stack(csm))
```

```python
import functools

import jax
import jax.numpy as jnp
from jax import lax
from jax.experimental import pallas as pl
from jax.experimental.pallas import tpu as pltpu

D_MODEL = 2048
ATTN_WIDTH = 1024
CONV_CH = 1024
HEAD_DIM = 64
HALF = HEAD_DIM // 2
N_HEADS = 16
N_KV_HEADS = 4
GROUP = N_HEADS // N_KV_HEADS
KV_DIM = N_KV_HEADS * HEAD_DIM
WINDOW = 128
CONV_WIDTH = 31
CONV_STATE = CONV_WIDTH - 1
D_FF = 4 * D_MODEL
ROPE_THETA = 10000.0
EPS = 1e-6
IN_COLS = ATTN_WIDTH + 2 * KV_DIM + 2 * CONV_CH
NEG = -1e30
PAST_LEN = 16384

K_OFF = ATTN_WIDTH
V_OFF = ATTN_WIDTH + KV_DIM
VAL_OFF = ATTN_WIDTH + 2 * KV_DIM
GATE_OFF = VAL_OFF + CONV_CH

VMEM_LIMIT_BYTES = 56 * 1024 * 1024
SUBLANES = 8
LANES = 128
BF16_ROWS = 16
VT_ROWS = HEAD_DIM + BF16_ROWS
CONV_PAD = 32

MLP_ROW_BLOCK = 1024
MLP_TILE = 512
CAST_ROWS = 256
SAMPLE_BLOCK = 16
PROJ_CHUNK = 256
SIDE_JOBS_PER_SOFTMAX = 3

F32 = jnp.float32
BF16 = jnp.bfloat16


def _rms_rows(x, g):
    ms = jnp.mean(x * x, axis=-1, keepdims=True)
    return x * lax.rsqrt(ms + EPS) * g


def _norm_dot(x, g, w):
    return jnp.dot(_rms_rows(x, g).astype(BF16), w, preferred_element_type=F32)


def _mlp_kernel(x_ref, xs_ref, g_ref, wu_ref, wd_ref, o_ref, os_ref, hm_ref):
    i, j = pl.program_id(0), pl.program_id(1)
    tm = x_ref.shape[0]

    @pl.when(j == 0)
    def _():
        x = x_ref[...]
        hm_ref[0:tm, :] = _rms_rows(x, g_ref[...]).astype(BF16)
        o_ref[...] = x

    @pl.when((i == 0) & (j == 0))
    def _():
        xs = xs_ref[...]
        hm_ref[tm:, :] = _rms_rows(xs, g_ref[...]).astype(BF16)
        os_ref[...] = xs

    def up_down(hm):
        h = jnp.dot(hm, wu_ref[...].astype(BF16), preferred_element_type=F32)
        h = jnp.square(jnp.maximum(h, 0.0)).astype(BF16)
        return jnp.dot(h, wd_ref[...].astype(BF16), preferred_element_type=F32)

    @pl.when(i == 0)
    def _():
        y = up_down(hm_ref[...])
        o_ref[...] += y[0:tm]
        os_ref[...] += y[tm:]

    @pl.when(i > 0)
    def _():
        o_ref[...] += up_down(hm_ref[0:tm, :])


def _mlp(x, xs, g, wu, wd):
    m, d = x.shape
    ms = xs.shape[0]
    f = wu.shape[1]
    tm, tf = MLP_ROW_BLOCK, MLP_TILE
    return pl.pallas_call(
        _mlp_kernel,
        grid=(m // tm, f // tf),
        in_specs=[
            pl.BlockSpec((tm, d), lambda i, j: (i, 0)),
            pl.BlockSpec((ms, d), lambda i, j: (0, 0)),
            pl.BlockSpec((1, d), lambda i, j: (0, 0)),
            pl.BlockSpec((d, tf), lambda i, j: (0, j)),
            pl.BlockSpec((tf, d), lambda i, j: (j, 0)),
        ],
        out_specs=[
            pl.BlockSpec((tm, d), lambda i, j: (i, 0)),
            pl.BlockSpec((ms, d), lambda i, j: (0, 0)),
        ],
        out_shape=[jax.ShapeDtypeStruct((m, d), F32), jax.ShapeDtypeStruct((ms, d), F32)],
        scratch_shapes=[pltpu.VMEM((tm + ms, d), BF16)],
        compiler_params=pltpu.CompilerParams(
            dimension_semantics=("arbitrary", "arbitrary"), vmem_limit_bytes=VMEM_LIMIT_BYTES),
        name="mlp",
    )(x, xs, g, wu, wd)


def _rotate_half(x):
    return jnp.concatenate([x[:, HALF:], x[:, :HALF]], axis=-1)


def _norm_rope(x, g, cos2, sin2):
    xn = _rms_rows(x, g)
    return xn * cos2 + _rotate_half(xn) * sin2


def _layernorm_swish(c, g, b):
    mu = jnp.mean(c, axis=-1, keepdims=True)
    xc = c - mu
    y = xc * lax.rsqrt(jnp.mean(xc * xc, axis=-1, keepdims=True) + EPS)
    y = y * g + b
    return y * jax.nn.sigmoid(y)


def _conv_taps_by_row_phase():
    groups = [[] for _ in range(SUBLANES)]
    for j in range(CONV_WIDTH):
        e = j + CONV_PAD - CONV_STATE
        groups[e % SUBLANES].append((j, e // SUBLANES))
    return groups


def _mixer_carry(n, klo_ref, khi_ref, vt_ref, ubuf_ref):
    @pl.when(n == 0)
    def _():
        klo_ref[:, 0:WINDOW, :] = jnp.zeros((N_KV_HEADS, WINDOW, LANES), BF16)
        khi_ref[:, 0:WINDOW, :] = jnp.zeros((N_KV_HEADS, WINDOW, LANES), BF16)
        vt_ref[:, 0:HEAD_DIM, 0:WINDOW] = jnp.zeros((N_KV_HEADS, HEAD_DIM, WINDOW), BF16)
        vt_ref[:, HEAD_DIM:, :] = jnp.ones((N_KV_HEADS, VT_ROWS - HEAD_DIM, 2 * WINDOW), BF16)
        ubuf_ref[0:CONV_PAD, :] = jnp.zeros((CONV_PAD, CONV_CH), F32)

    @pl.when(n > 0)
    def _():
        klo_ref[:, 0:WINDOW, :] = klo_ref[:, WINDOW:2 * WINDOW, :]
        khi_ref[:, 0:WINDOW, :] = khi_ref[:, WINDOW:2 * WINDOW, :]
        vt_ref[:, 0:HEAD_DIM, 0:WINDOW] = vt_ref[:, 0:HEAD_DIM, WINDOW:2 * WINDOW]
        ubuf_ref[0:CONV_PAD, :] = ubuf_ref[WINDOW:WINDOW + CONV_PAD, :]


def _mixer_block(n, z_ref, mix_ref, kr_ref, sinks_ref, cos_ref, sin_ref, qg_ref, kg_ref, cw_ref, cb_ref,
                 lg_ref, lb_ref, klo_ref, khi_ref, vt_ref, ubuf_ref, side_jobs=()):
    n_q_slabs = ATTN_WIDTH // LANES
    n_k_slabs = KV_DIM // LANES
    side = iter(side_jobs)

    def tick(k=1):
        for _ in range(k):
            job = next(side, None)
            if job is not None:
                job()

    u = z_ref[:, VAL_OFF:VAL_OFF + CONV_CH] * jax.nn.sigmoid(z_ref[:, GATE_OFF:GATE_OFF + CONV_CH])
    ubuf_ref[CONV_PAD:CONV_PAD + WINDOW, :] = u

    lane = lax.broadcasted_iota(jnp.int32, (WINDOW, LANES), 1)
    first_head = lane < HEAD_DIM
    first_half = (lane & HALF) == 0

    n_slabs = n_q_slabs + n_k_slabs
    x = jnp.concatenate([z_ref[:, m * LANES:(m + 1) * LANES] for m in range(n_slabs)], axis=0)
    sq = x * x
    sq_hi = sq.astype(BF16)
    sq_lo = (sq - sq_hi.astype(F32)).astype(BF16)
    er = lax.broadcasted_iota(jnp.int32, (LANES, LANES), 0)
    ec = lax.broadcasted_iota(jnp.int32, (LANES, LANES), 1)
    head_mean = jnp.where((er < HEAD_DIM) == (ec < HEAD_DIM), 1.0 / HEAD_DIM, 0.0).astype(BF16)
    ms = (jnp.dot(sq_hi, head_mean, preferred_element_type=F32)
          + jnp.dot(sq_lo, head_mean, preferred_element_type=F32))
    xn = x * lax.rsqrt(ms + EPS)

    cos4 = cos_ref[...]
    sin4 = sin_ref[...]

    def rope(xs):
        rot = jnp.where(first_half, pltpu.roll(xs, LANES - HALF, 1), pltpu.roll(xs, HALF, 1))
        return xs * cos4 + rot * sin4

    qg2 = jnp.concatenate([qg_ref[...]] * (LANES // HEAD_DIM), axis=1) * (HEAD_DIM ** -0.5)
    kg2 = jnp.concatenate([kg_ref[...]] * (LANES // HEAD_DIM), axis=1)
    q_slabs = [rope(xn[m * WINDOW:(m + 1) * WINDOW] * qg2).astype(BF16) for m in range(n_q_slabs)]

    zero = jnp.zeros((WINDOW, LANES), F32)
    for ks in range(n_k_slabs):
        r0 = (n_q_slabs + ks) * WINDOW
        kr = rope(xn[r0:r0 + WINDOW] * kg2)
        kr_ref[:, ks * LANES:(ks + 1) * LANES] = kr
        kr_swapped = pltpu.roll(kr, HEAD_DIM, 1)
        klo_ref[2 * ks, WINDOW:2 * WINDOW, :] = jnp.where(first_head, kr, zero).astype(BF16)
        khi_ref[2 * ks, WINDOW:2 * WINDOW, :] = jnp.where(first_head, zero, kr_swapped).astype(BF16)
        klo_ref[2 * ks + 1, WINDOW:2 * WINDOW, :] = jnp.where(first_head, kr_swapped, zero).astype(BF16)
        khi_ref[2 * ks + 1, WINDOW:2 * WINDOW, :] = jnp.where(first_head, zero, kr).astype(BF16)

        vs_t = z_ref[:, V_OFF + ks * LANES:V_OFF + (ks + 1) * LANES].T.astype(BF16)
        vt_ref[2 * ks, 0:HEAD_DIM, WINDOW:2 * WINDOW] = vs_t[0:HEAD_DIM]
        vt_ref[2 * ks + 1, 0:HEAD_DIM, WINDOW:2 * WINDOW] = vs_t[HEAD_DIM:2 * HEAD_DIM]

    key = lax.broadcasted_iota(jnp.int32, (2 * WINDOW, WINDOW), 0)
    qry = lax.broadcasted_iota(jnp.int32, (2 * WINDOW, WINDOW), 1)
    diff = qry + WINDOW - key
    band = (diff >= 0) & (diff <= jnp.where(n > 0, WINDOW - 1, qry))
    bias = jnp.where(band, 0.0, NEG)
    bias2 = jnp.concatenate([bias, bias], axis=1)

    def conv_chunk(c):
        cl = slice(c * LANES, (c + 1) * LANES)
        y = cb_ref[:, cl]
        for phase, taps in enumerate(_conv_taps_by_row_phase()):
            rows_needed = WINDOW + (SUBLANES if phase else 0)
            part = None
            for j, tile in taps:
                term = cw_ref[j:j + 1, cl] * ubuf_ref[tile * SUBLANES:tile * SUBLANES + rows_needed, cl]
                part = term if part is None else part + term
            if phase:
                part = pltpu.roll(part, rows_needed - phase, 0)[0:WINDOW]
            y = y + part
        return y

    conv_per_head = CONV_CH // LANES // N_KV_HEADS
    y_chunks = []
    for h in range(N_KV_HEADS):
        q2 = jnp.concatenate([q_slabs[2 * h], q_slabs[2 * h + 1]], axis=0)
        normed_t = []
        for second, k_ref in ((0, klo_ref), (1, khi_ref)):
            s_t = lax.dot_general(k_ref[h], q2, (((1,), (1,)), ((), ())), preferred_element_type=F32)
            s_t = s_t + bias2
            sink_row = jnp.concatenate(
                [jnp.full((1, WINDOW), sinks_ref[GROUP * h + 2 * r + second], F32) for r in range(2)], axis=1)
            m = jnp.maximum(jnp.max(s_t, axis=0, keepdims=True), sink_row)
            p_t = jnp.exp(s_t - m).astype(BF16)
            o_t = jnp.dot(vt_ref[h], p_t, preferred_element_type=F32)
            den = o_t[HEAD_DIM:HEAD_DIM + 1] + jnp.exp(sink_row - m)
            normed_t.append(o_t[0:HEAD_DIM] * (1.0 / den))
            tick(SIDE_JOBS_PER_SOFTMAX)
        for r in range(2):
            a_t = jnp.concatenate([normed_t[0][:, r * WINDOW:(r + 1) * WINDOW],
                                   normed_t[1][:, r * WINDOW:(r + 1) * WINDOW]], axis=0)
            m_out = 2 * h + r
            mix_ref[:, m_out * LANES:(m_out + 1) * LANES] = a_t.T.astype(BF16)
        y_chunks += [conv_chunk(conv_per_head * h + i) for i in range(conv_per_head)]

    conv = jnp.concatenate(y_chunks, axis=-1)
    mix_ref[:, ATTN_WIDTH:] = _layernorm_swish(conv, lg_ref[...], lb_ref[...]).astype(BF16)
    tick(len(side_jobs))


def _weight_chunk_copy(w_hbm, stage_ref, sem, c, slot):
    cols = w_hbm.shape[1]
    return pltpu.make_async_copy(w_hbm.at[pl.ds(c * CAST_ROWS, CAST_ROWS), :],
                                 stage_ref.at[slot, :, pl.ds(0, cols)], sem.at[slot])


def _load_weight_bf16(w_hbm, w_ref, stage_ref, sem):
    n_chunks = w_ref.shape[0] // CAST_ROWS
    cols = w_ref.shape[1]
    _weight_chunk_copy(w_hbm, stage_ref, sem, 0, 0).start()
    for c in range(n_chunks):
        slot = c % 2
        if c + 1 < n_chunks:
            _weight_chunk_copy(w_hbm, stage_ref, sem, c + 1, 1 - slot).start()
        _weight_chunk_copy(w_hbm, stage_ref, sem, c, slot).wait()
        w_ref[c * CAST_ROWS:(c + 1) * CAST_ROWS, :] = stage_ref[slot, :, 0:cols].astype(BF16)


def _front_kernel(sinks_ref, x_next_ref, x_res_ref, xs_ref, g_ref, win_hbm, wout_hbm, cos_ref, sin_ref,
                  qg_ref, kg_ref, cw_ref, cb_ref, lg_ref, lb_ref,
                  x1_ref, zs_ref, nk_ref, nv_ref, nc_ref, wout_bf_hbm,
                  win_ref, wout_ref, stage_ref, load_sem, export_sem,
                  z_next_ref, z_ref, mix_ref, kr_ref, klo_ref, khi_ref, vt_ref, ubuf_ref, *, nb):
    t = pl.program_id(0)
    drain = pl.num_programs(0) - 1
    n = lax.rem(t, nb)
    export = pltpu.make_async_copy(wout_ref, wout_bf_hbm, export_sem.at[0])

    @pl.when(t == 0)
    def _():
        _load_weight_bf16(win_hbm, win_ref, stage_ref, load_sem)
        _load_weight_bf16(wout_hbm, wout_ref, stage_ref, load_sem)
        export.start()
        rows0 = x_res_ref.shape[0]
        z_both = _norm_dot(jnp.concatenate([x_res_ref[...], xs_ref[...]], axis=0), g_ref[...], win_ref[...])
        z_ref[...] = z_both[0:rows0]
        zs_ref[...] = z_both[rows0:]
        mix_ref[...] = jnp.zeros(mix_ref.shape, BF16)

    @pl.when(t < drain)
    def _():
        _mixer_carry(n, klo_ref, khi_ref, vt_ref, ubuf_ref)

    @pl.when(t < drain)
    def _():
        xn_next = _rms_rows(x_next_ref[...], g_ref[...]).astype(BF16)
        mix_prev = mix_ref[...]
        jobs = []
        for c in range(IN_COLS // PROJ_CHUNK):
            def in_job(cols=slice(c * PROJ_CHUNK, (c + 1) * PROJ_CHUNK)):
                z_next_ref[:, cols] = jnp.dot(xn_next, win_ref[:, cols], preferred_element_type=F32)
            jobs.append(in_job)
        for c in range(x1_ref.shape[1] // PROJ_CHUNK):
            def out_job(cols=slice(c * PROJ_CHUNK, (c + 1) * PROJ_CHUNK)):
                x1_ref[:, cols] = x_res_ref[:, cols] + jnp.dot(mix_prev, wout_ref[:, cols],
                                                               preferred_element_type=F32)
            jobs.append(out_job)
        _mixer_block(n, z_ref, mix_ref, kr_ref, sinks_ref, cos_ref, sin_ref, qg_ref, kg_ref, cw_ref,
                     cb_ref, lg_ref, lb_ref, klo_ref, khi_ref, vt_ref, ubuf_ref, side_jobs=jobs)

    @pl.when((n == nb - 1) & (t < drain))
    def _():
        nk_ref[0] = kr_ref[...].T
        nv_ref[0] = z_ref[:, V_OFF:V_OFF + KV_DIM].T
        nc_ref[0] = ubuf_ref[CONV_PAD + WINDOW - CONV_STATE:CONV_PAD + WINDOW, :]

    @pl.when(t < drain)
    def _():
        z_ref[...] = z_next_ref[...]

    @pl.when(t == drain)
    def _():
        x1_ref[...] = x_res_ref[...] + jnp.dot(mix_ref[...], wout_ref[...], preferred_element_type=F32)
        export.wait()


def _prompt_front(x, xs, g, w_in, w_out, batch, seq, cos4, sin4, qg, kg, sinks, cw, cb, lg, lb):
    nb = seq // WINDOW
    steps = batch * nb
    d = x.shape[1]
    ms = xs.shape[0]
    const = lambda t: (0, 0)
    behind = lambda t: (jnp.maximum(t - 1, 0), 0)
    ahead = lambda t: (jnp.minimum(t + 1, steps - 1), 0)
    pos = lambda t: (lax.rem(t, nb), 0)
    per_batch = lambda t: (jnp.minimum(t // nb, batch - 1), 0, 0)
    assert w_in.shape[0] % CAST_ROWS == 0 and w_out.shape[0] % CAST_ROWS == 0
    stage_cols = max(w_in.shape[1], w_out.shape[1])
    return pl.pallas_call(
        functools.partial(_front_kernel, nb=nb),
        grid=(steps + 1,),
        in_specs=[
            pl.BlockSpec(memory_space=pltpu.SMEM),
            pl.BlockSpec((WINDOW, d), ahead),
            pl.BlockSpec((WINDOW, d), behind),
            pl.BlockSpec((ms, d), const),
            pl.BlockSpec((1, d), const),
            pl.BlockSpec(memory_space=pl.ANY),
            pl.BlockSpec(memory_space=pl.ANY),
            pl.BlockSpec((WINDOW, LANES), pos),
            pl.BlockSpec((WINDOW, LANES), pos),
            pl.BlockSpec((1, HEAD_DIM), const),
            pl.BlockSpec((1, HEAD_DIM), const),
            pl.BlockSpec((CONV_WIDTH, CONV_CH), const),
            pl.BlockSpec((1, CONV_CH), const),
            pl.BlockSpec((1, CONV_CH), const),
            pl.BlockSpec((1, CONV_CH), const),
        ],
        out_specs=[
            pl.BlockSpec((WINDOW, d), behind),
            pl.BlockSpec((ms, IN_COLS), const),
            pl.BlockSpec((1, KV_DIM, WINDOW), per_batch),
            pl.BlockSpec((1, KV_DIM, WINDOW), per_batch),
            pl.BlockSpec((1, CONV_STATE, CONV_CH), per_batch),
            pl.BlockSpec(memory_space=pl.ANY),
        ],
        out_shape=[
            jax.ShapeDtypeStruct((steps * WINDOW, d), F32),
            jax.ShapeDtypeStruct((ms, IN_COLS), F32),
            jax.ShapeDtypeStruct((batch, KV_DIM, WINDOW), F32),
            jax.ShapeDtypeStruct((batch, KV_DIM, WINDOW), F32),
            jax.ShapeDtypeStruct((batch, CONV_STATE, CONV_CH), F32),
            jax.ShapeDtypeStruct(w_out.shape, BF16),
        ],
        scratch_shapes=[
            pltpu.VMEM(w_in.shape, BF16),
            pltpu.VMEM(w_out.shape, BF16),
            pltpu.VMEM((2, CAST_ROWS, stage_cols), F32),
            pltpu.SemaphoreType.DMA((2,)),
            pltpu.SemaphoreType.DMA((1,)),
            pltpu.VMEM((WINDOW, IN_COLS), F32),
            pltpu.VMEM((WINDOW, IN_COLS), F32),
            pltpu.VMEM((WINDOW, ATTN_WIDTH + CONV_CH), BF16),
            pltpu.VMEM((WINDOW, KV_DIM), F32),
            pltpu.VMEM((N_KV_HEADS, 2 * WINDOW, LANES), BF16),
            pltpu.VMEM((N_KV_HEADS, 2 * WINDOW, LANES), BF16),
            pltpu.VMEM((N_KV_HEADS, VT_ROWS, 2 * WINDOW), BF16),
            pltpu.VMEM((CONV_PAD + WINDOW, CONV_CH), F32),
        ],
        compiler_params=pltpu.CompilerParams(
            dimension_semantics=("arbitrary",), vmem_limit_bytes=VMEM_LIMIT_BYTES),
        name="prompt_front",
    )(sinks, x, x, xs, g, w_in, w_out, cos4, sin4, qg, kg, cw, cb, lg, lb)


def _sample_mixer_kernel(sinks_ref, z_ref, xs_ref, ck_ref, cv_ref, st_ref, cos_ref, sin_ref,
                         qg_ref, kg_ref, cw_ref, cb_ref, lg_ref, lb_ref, wout_ref,
                         x1_ref, nk_ref, nv_ref, nc_ref, qexp_ref, knew_ref, a_ref):
    bs = z_ref.shape[0]
    cos2 = cos_ref[0:1, 0:HEAD_DIM]
    sin2 = sin_ref[0:1, 0:HEAD_DIM]

    q = jnp.concatenate([z_ref[:, h * HEAD_DIM:(h + 1) * HEAD_DIM] for h in range(N_HEADS)], axis=0)
    qn = _norm_rope(q, qg_ref[...], cos2, sin2).astype(BF16)
    tile_r = lax.broadcasted_iota(jnp.int32, (HEAD_DIM, KV_DIM), 0)
    tile_c = lax.broadcasted_iota(jnp.int32, (HEAD_DIM, KV_DIM), 1)
    tile = jnp.where((tile_c & (HEAD_DIM - 1)) == tile_r, 1.0, 0.0).astype(BF16)
    qt = jnp.dot(qn, tile, preferred_element_type=F32)
    q_row = lax.broadcasted_iota(jnp.int32, (N_HEADS * bs, KV_DIM), 0)
    q_col = lax.broadcasted_iota(jnp.int32, (N_HEADS * bs, KV_DIM), 1)
    own = (q_row // (bs * GROUP)) == (q_col // HEAD_DIM)
    qexp = jnp.where(own, qt, 0.0)
    for half in range(KV_DIM // LANES):
        qexp_ref[half] = qexp[:, half * LANES:(half + 1) * LANES]

    k = jnp.concatenate([z_ref[:, K_OFF + h * HEAD_DIM:K_OFF + (h + 1) * HEAD_DIM]
                         for h in range(N_KV_HEADS)], axis=0)
    kn = _norm_rope(k, kg_ref[...], cos2, sin2)
    for h in range(N_KV_HEADS):
        knew_ref[:, h * HEAD_DIM:(h + 1) * HEAD_DIM] = kn[h * bs:(h + 1) * bs]
    knew = knew_ref[...]
    vnew = z_ref[:, V_OFF:V_OFF + KV_DIM]
    knew_r = knew.astype(BF16).astype(F32)
    vnew_r = vnew.astype(BF16).astype(F32)
    pad = jnp.zeros((WINDOW - bs, KV_DIM), F32)
    knew_t = jnp.concatenate([knew, pad], axis=0).T
    vnew_t = jnp.concatenate([vnew, pad], axis=0).T
    newest = lax.broadcasted_iota(jnp.int32, (KV_DIM, WINDOW), 1) == WINDOW - 1

    n_rows = bs * N_HEADS
    q_rows, s_rows = [], []
    for b in range(bs):
        heads_of_b = pl.ds(b, N_HEADS, stride=bs)
        qb = jnp.concatenate([qexp_ref[half, heads_of_b, :] for half in range(KV_DIM // LANES)], axis=1)
        q_rows.append(qb)
        s_rows.append(jnp.dot(qb.astype(BF16), ck_ref[b].astype(BF16), preferred_element_type=F32))
    q_all = jnp.concatenate(q_rows, axis=0)
    s = jnp.concatenate(s_rows, axis=0) * (HEAD_DIM ** -0.5)
    knew_rows = jnp.concatenate([jnp.broadcast_to(knew_r[b:b + 1, :], (N_HEADS, KV_DIM)) for b in range(bs)], axis=0)
    s_new = jnp.sum(q_all * knew_rows, axis=-1, keepdims=True) * (HEAD_DIM ** -0.5)
    sink_col = jnp.concatenate([jnp.full((1, 1), sinks_ref[hh], F32) for hh in range(N_HEADS)] * bs, axis=0)
    key_ok = lax.broadcasted_iota(jnp.int32, (n_rows, WINDOW), 1) >= 1
    s = jnp.where(key_ok, s, NEG)
    m = jnp.maximum(jnp.maximum(jnp.max(s, axis=-1, keepdims=True), s_new), sink_col)
    p = jnp.exp(s - m)
    p_new = jnp.exp(s_new - m)
    den = jnp.sum(p, axis=-1, keepdims=True) + p_new + jnp.exp(sink_col - m)
    p = (p / den).astype(BF16)
    p_new = (p_new / den).astype(BF16).astype(F32)

    o_row = lax.broadcasted_iota(jnp.int32, (N_HEADS, KV_DIM), 0)
    o_col = lax.broadcasted_iota(jnp.int32, (N_HEADS, KV_DIM), 1)
    o_own = (o_row // GROUP) == (o_col // HEAD_DIM)
    fold_r = lax.broadcasted_iota(jnp.int32, (KV_DIM, LANES), 0)
    fold_c = lax.broadcasted_iota(jnp.int32, (KV_DIM, LANES), 1)
    fold = jnp.where((fold_r & (HEAD_DIM - 1)) == fold_c, 1.0, 0.0).astype(BF16)
    for b in range(bs):
        rows_b = slice(b * N_HEADS, (b + 1) * N_HEADS)
        kc = ck_ref[b]
        vc = cv_ref[b]
        o = lax.dot_general(p[rows_b], vc.astype(BF16), (((1,), (1,)), ((), ())), preferred_element_type=F32)
        o = o + p_new[rows_b] * vnew_r[b:b + 1, :]
        o = jnp.where(o_own, o, 0.0).astype(BF16)
        a_ref[pl.ds(b, N_HEADS, stride=bs), :] = jnp.dot(o, fold, preferred_element_type=F32)

        nk_ref[b] = jnp.where(newest, pltpu.roll(knew_t, WINDOW - 1 - b, 1), pltpu.roll(kc, WINDOW - 1, 1))
        nv_ref[b] = jnp.where(newest, pltpu.roll(vnew_t, WINDOW - 1 - b, 1), pltpu.roll(vc, WINDOW - 1, 1))

    u = z_ref[:, VAL_OFF:VAL_OFF + CONV_CH] * jax.nn.sigmoid(z_ref[:, GATE_OFF:GATE_OFF + CONV_CH])
    y = cb_ref[...] + u * cw_ref[CONV_STATE:CONV_WIDTH, :]
    for j in range(CONV_STATE):
        y = y + st_ref[j] * cw_ref[j:j + 1, :]
        nc_ref[j] = st_ref[j + 1] if j + 1 < CONV_STATE else u
    c = _layernorm_swish(y, lg_ref[...], lb_ref[...])

    x1 = xs_ref[...] + jnp.dot(c.astype(BF16), wout_ref[ATTN_WIDTH:, :], preferred_element_type=F32)
    for h in range(N_HEADS):
        x1 = x1 + jnp.dot(a_ref[h * bs:(h + 1) * bs, 0:HEAD_DIM].astype(BF16),
                          wout_ref[h * HEAD_DIM:(h + 1) * HEAD_DIM, :], preferred_element_type=F32)
    x1_ref[...] = x1


def _sample_mixer(zs, xs, cache_k, cache_v, state_t, cos_t, sin_t, pos_row, qg, kg, sinks, cw, cb, lg, lb, w_out,
                  *, bs):
    nsamp, d = xs.shape
    const = lambda i: (0, 0)
    step = lambda i: (i, 0)
    step3 = lambda i: (i, 0, 0)
    taps = lambda i: (0, i, 0)
    assert pos_row % SUBLANES == 0
    decode_pos = lambda i: (pos_row // SUBLANES, 0)
    return pl.pallas_call(
        _sample_mixer_kernel,
        grid=(nsamp // bs,),
        in_specs=[
            pl.BlockSpec(memory_space=pltpu.SMEM),
            pl.BlockSpec((bs, IN_COLS), step),
            pl.BlockSpec((bs, d), step),
            pl.BlockSpec((bs, KV_DIM, WINDOW), step3),
            pl.BlockSpec((bs, KV_DIM, WINDOW), step3),
            pl.BlockSpec((CONV_STATE, bs, CONV_CH), taps),
            pl.BlockSpec((SUBLANES, LANES), decode_pos),
            pl.BlockSpec((SUBLANES, LANES), decode_pos),
            pl.BlockSpec((1, HEAD_DIM), const),
            pl.BlockSpec((1, HEAD_DIM), const),
            pl.BlockSpec((CONV_WIDTH, CONV_CH), const),
            pl.BlockSpec((1, CONV_CH), const),
            pl.BlockSpec((1, CONV_CH), const),
            pl.BlockSpec((1, CONV_CH), const),
            pl.BlockSpec((ATTN_WIDTH + CONV_CH, d), const, pipeline_mode=pl.Buffered(1)),
        ],
        out_specs=[
            pl.BlockSpec((bs, d), step),
            pl.BlockSpec((bs, KV_DIM, WINDOW), step3),
            pl.BlockSpec((bs, KV_DIM, WINDOW), step3),
            pl.BlockSpec((CONV_STATE, bs, CONV_CH), taps),
        ],
        out_shape=[
            jax.ShapeDtypeStruct((nsamp, d), F32),
            jax.ShapeDtypeStruct((nsamp, KV_DIM, WINDOW), F32),
            jax.ShapeDtypeStruct((nsamp, KV_DIM, WINDOW), F32),
            jax.ShapeDtypeStruct((CONV_STATE, nsamp, CONV_CH), F32),
        ],
        scratch_shapes=[
            pltpu.VMEM((KV_DIM // LANES, N_HEADS * bs, LANES), F32),
            pltpu.VMEM((bs, KV_DIM), F32),
            pltpu.VMEM((N_HEADS * bs, LANES), F32),
        ],
        compiler_params=pltpu.CompilerParams(
            dimension_semantics=("arbitrary",), vmem_limit_bytes=VMEM_LIMIT_BYTES),
        name="sample_mixer",
    )(sinks, zs, xs, cache_k, cache_v, state_t, cos_t, sin_t, qg, kg, cw, cb, lg, lb, w_out)


def _rope_tables(pos, heads_per_row=1):
    reps = 2 * heads_per_row
    inv = jnp.tile(ROPE_THETA ** (-jnp.arange(HALF, dtype=F32) / HALF), reps)
    sign = jnp.tile(jnp.concatenate([-jnp.ones((HALF,), F32), jnp.ones((HALF,), F32)]), heads_per_row)
    ang = pos.astype(F32)[:, None] * inv[None, :]
    return jnp.cos(ang), jnp.sin(ang) * sign[None, :]


def _to_feature_major(cache):
    rows = cache.shape[0]
    return jnp.transpose(cache, (0, 2, 3, 1)).reshape(rows, KV_DIM, WINDOW)


def _from_feature_major(cache_t):
    rows = cache_t.shape[0]
    return jnp.transpose(cache_t.reshape(rows, N_KV_HEADS, HEAD_DIM, WINDOW), (0, 3, 1, 2))


def kernel(x_prompt, x_sample, cache_k, cache_v, state_conv, norm_mix_g, w_in, q_norm_g, k_norm_g,
           sinks, conv_w, conv_b, conv_ln_g, conv_ln_b, w_out, norm_mlp_g, w_up, w_down):
    batch, seq, d = x_prompt.shape
    nsamp, dec_seq, _ = x_sample.shape
    depth = w_in.shape[0]
    assert dec_seq == 1 and seq % WINDOW == 0 and cache_k.shape[2] == WINDOW

    row = jnp.arange(seq + WINDOW)
    cos4, sin4 = _rope_tables(jnp.where(row < seq, row, PAST_LEN + row - seq), heads_per_row=LANES // HEAD_DIM)

    xp = x_prompt.reshape(batch * seq, d)
    xs = x_sample.reshape(nsamp * dec_seq, d)
    kp, vp, cp, ksm, vsm, csm = [], [], [], [], [], []
    for l in range(depth):
        qg, kg = q_norm_g[l:l + 1], k_norm_g[l:l + 1]
        cb, lg, lb = conv_b[l:l + 1], conv_ln_g[l:l + 1], conv_ln_b[l:l + 1]
        x1, zs, nk, nv, nc, w_out_bf = _prompt_front(xp, xs, norm_mix_g[l:l + 1], w_in[l], w_out[l], batch, seq,
                                                     cos4, sin4, qg, kg, sinks[l], conv_w[l], cb, lg, lb)
        kp.append(_from_feature_major(nk))
        vp.append(_from_feature_major(nv))
        cp.append(nc)

        x1s, nk, nv, nc = _sample_mixer(
            zs, xs, _to_feature_major(cache_k[l]), _to_feature_major(cache_v[l]),
            jnp.transpose(state_conv[l], (1, 0, 2)), cos4, sin4, seq, qg, kg, sinks[l], conv_w[l], cb, lg, lb,
            w_out_bf, bs=SAMPLE_BLOCK)
        ksm.append(_from_feature_major(nk))
        vsm.append(_from_feature_major(nv))
        csm.append(jnp.transpose(nc, (1, 0, 2)))

        xp, xs = _mlp(x1, x1s, norm_mlp_g[l:l + 1], w_up[l], w_down[l])

    return (xp.reshape(batch, seq, d), xs.reshape(nsamp, dec_seq, d),
            jnp.stack(kp), jnp.stack(vp), jnp.stack(cp),
            jnp.stack(ksm), jnp.stack(vsm), jnp.stack(csm))
```

```python
import functools

import jax
import jax.numpy as jnp
from jax import lax
from jax.experimental import pallas as pl
from jax.experimental.pallas import tpu as pltpu

D_MODEL = 2048
ATTN_WIDTH = 1024
CONV_CH = 1024
HEAD_DIM = 64
HALF = HEAD_DIM // 2
N_HEADS = 16
N_KV_HEADS = 4
GROUP = N_HEADS // N_KV_HEADS
KV_DIM = N_KV_HEADS * HEAD_DIM
WINDOW = 128
CONV_WIDTH = 31
CONV_STATE = CONV_WIDTH - 1
D_FF = 4 * D_MODEL
ROPE_THETA = 10000.0
EPS = 1e-6
IN_COLS = ATTN_WIDTH + 2 * KV_DIM + 2 * CONV_CH
NEG = -1e30
PAST_LEN = 16384

K_OFF = ATTN_WIDTH
V_OFF = ATTN_WIDTH + KV_DIM
VAL_OFF = ATTN_WIDTH + 2 * KV_DIM
GATE_OFF = VAL_OFF + CONV_CH

VMEM_LIMIT_BYTES = 56 * 1024 * 1024
SUBLANES = 8
LANES = 128
BF16_ROWS = 16
VT_ROWS = HEAD_DIM + BF16_ROWS
CONV_PAD = 32

MLP_ROW_BLOCK = 1024
MLP_TILE = 512
CAST_ROWS = 256
SAMPLE_BLOCK = 16
PROJ_CHUNK = 256
SIDE_JOBS_PER_SOFTMAX = 3

F32 = jnp.float32
BF16 = jnp.bfloat16


def _rms_rows(x, g):
    ms = jnp.mean(x * x, axis=-1, keepdims=True)
    return x * lax.rsqrt(ms + EPS) * g


def _norm_dot(x, g, w):
    return jnp.dot(_rms_rows(x, g).astype(BF16), w, preferred_element_type=F32)


def _mlp_kernel(x_ref, xs_ref, g_ref, wu_ref, wd_ref, o_ref, os_ref, hm_ref):
    i, j = pl.program_id(0), pl.program_id(1)
    tm = x_ref.shape[0]

    @pl.when(j == 0)
    def _():
        x = x_ref[...]
        hm_ref[0:tm, :] = _rms_rows(x, g_ref[...]).astype(BF16)
        o_ref[...] = x

    @pl.when((i == 0) & (j == 0))
    def _():
        xs = xs_ref[...]
        hm_ref[tm:, :] = _rms_rows(xs, g_ref[...]).astype(BF16)
        os_ref[...] = xs

    def up_down(hm):
        h = jnp.dot(hm, wu_ref[...].astype(BF16), preferred_element_type=F32)
        h = jnp.square(jnp.maximum(h, 0.0)).astype(BF16)
        return jnp.dot(h, wd_ref[...].astype(BF16), preferred_element_type=F32)

    @pl.when(i == 0)
    def _():
        y = up_down(hm_ref[...])
        o_ref[...] += y[0:tm]
        os_ref[...] += y[tm:]

    @pl.when(i > 0)
    def _():
        o_ref[...] += up_down(hm_ref[0:tm, :])


def _mlp(x, xs, g, wu, wd):
    m, d = x.shape
    ms = xs.shape[0]
    f = wu.shape[1]
    tm, tf = MLP_ROW_BLOCK, MLP_TILE
    return pl.pallas_call(
        _mlp_kernel,
        grid=(m // tm, f // tf),
        in_specs=[
            pl.BlockSpec((tm, d), lambda i, j: (i, 0)),
            pl.BlockSpec((ms, d), lambda i, j: (0, 0)),
            pl.BlockSpec((1, d), lambda i, j: (0, 0)),
            pl.BlockSpec((d, tf), lambda i, j: (0, j)),
            pl.BlockSpec((tf, d), lambda i, j: (j, 0)),
        ],
        out_specs=[
            pl.BlockSpec((tm, d), lambda i, j: (i, 0)),
            pl.BlockSpec((ms, d), lambda i, j: (0, 0)),
        ],
        out_shape=[jax.ShapeDtypeStruct((m, d), F32), jax.ShapeDtypeStruct((ms, d), F32)],
        scratch_shapes=[pltpu.VMEM((tm + ms, d), BF16)],
        compiler_params=pltpu.CompilerParams(
            dimension_semantics=("arbitrary", "arbitrary"), vmem_limit_bytes=VMEM_LIMIT_BYTES),
        name="mlp",
    )(x, xs, g, wu, wd)


def _rotate_half(x):
    return jnp.concatenate([x[:, HALF:], x[:, :HALF]], axis=-1)


def _norm_rope(x, g, cos2, sin2):
    xn = _rms_rows(x, g)
    return xn * cos2 + _rotate_half(xn) * sin2


def _layernorm_swish(c, g, b):
    mu = jnp.mean(c, axis=-1, keepdims=True)
    xc = c - mu
    y = xc * lax.rsqrt(jnp.mean(xc * xc, axis=-1, keepdims=True) + EPS)
    y = y * g + b
    return y * jax.nn.sigmoid(y)


def _conv_taps_by_row_phase():
    groups = [[] for _ in range(SUBLANES)]
    for j in range(CONV_WIDTH):
        e = j + CONV_PAD - CONV_STATE
        groups[e % SUBLANES].append((j, e // SUBLANES))
    return groups


def _mixer_carry(n, klo_ref, khi_ref, vt_ref, ubuf_ref):
    @pl.when(n == 0)
    def _():
        klo_ref[:, 0:WINDOW, :] = jnp.zeros((N_KV_HEADS, WINDOW, LANES), BF16)
        khi_ref[:, 0:WINDOW, :] = jnp.zeros((N_KV_HEADS, WINDOW, LANES), BF16)
        vt_ref[:, 0:HEAD_DIM, 0:WINDOW] = jnp.zeros((N_KV_HEADS, HEAD_DIM, WINDOW), BF16)
        vt_ref[:, HEAD_DIM:, :] = jnp.ones((N_KV_HEADS, VT_ROWS - HEAD_DIM, 2 * WINDOW), BF16)
        ubuf_ref[0:CONV_PAD, :] = jnp.zeros((CONV_PAD, CONV_CH), F32)

    @pl.when(n > 0)
    def _():
        klo_ref[:, 0:WINDOW, :] = klo_ref[:, WINDOW:2 * WINDOW, :]
        khi_ref[:, 0:WINDOW, :] = khi_ref[:, WINDOW:2 * WINDOW, :]
        vt_ref[:, 0:HEAD_DIM, 0:WINDOW] = vt_ref[:, 0:HEAD_DIM, WINDOW:2 * WINDOW]
        ubuf_ref[0:CONV_PAD, :] = ubuf_ref[WINDOW:WINDOW + CONV_PAD, :]


def _mixer_block(n, z_ref, mix_ref, kr_ref, sinks_ref, cos_ref, sin_ref, qg_ref, kg_ref, cw_ref, cb_ref,
                 lg_ref, lb_ref, klo_ref, khi_ref, vt_ref, ubuf_ref, side_jobs=()):
    n_q_slabs = ATTN_WIDTH // LANES
    n_k_slabs = KV_DIM // LANES
    side = iter(side_jobs)

    def tick(k=1):
        for _ in range(k):
            job = next(side, None)
            if job is not None:
                job()

    u = z_ref[:, VAL_OFF:VAL_OFF + CONV_CH] * jax.nn.sigmoid(z_ref[:, GATE_OFF:GATE_OFF + CONV_CH])
    ubuf_ref[CONV_PAD:CONV_PAD + WINDOW, :] = u

    lane = lax.broadcasted_iota(jnp.int32, (WINDOW, LANES), 1)
    first_head = lane < HEAD_DIM
    first_half = (lane & HALF) == 0

    n_slabs = n_q_slabs + n_k_slabs
    x = jnp.concatenate([z_ref[:, m * LANES:(m + 1) * LANES] for m in range(n_slabs)], axis=0)
    sq = x * x
    sq_hi = sq.astype(BF16)
    sq_lo = (sq - sq_hi.astype(F32)).astype(BF16)
    er = lax.broadcasted_iota(jnp.int32, (LANES, LANES), 0)
    ec = lax.broadcasted_iota(jnp.int32, (LANES, LANES), 1)
    head_mean = jnp.where((er < HEAD_DIM) == (ec < HEAD_DIM), 1.0 / HEAD_DIM, 0.0).astype(BF16)
    ms = (jnp.dot(sq_hi, head_mean, preferred_element_type=F32)
          + jnp.dot(sq_lo, head_mean, preferred_element_type=F32))
    xn = x * lax.rsqrt(ms + EPS)

    cos4 = cos_ref[...]
    sin4 = sin_ref[...]

    def rope(xs):
        rot = jnp.where(first_half, pltpu.roll(xs, LANES - HALF, 1), pltpu.roll(xs, HALF, 1))
        return xs * cos4 + rot * sin4

    qg2 = jnp.concatenate([qg_ref[...]] * (LANES // HEAD_DIM), axis=1) * (HEAD_DIM ** -0.5)
    kg2 = jnp.concatenate([kg_ref[...]] * (LANES // HEAD_DIM), axis=1)
    q_slabs = [rope(xn[m * WINDOW:(m + 1) * WINDOW] * qg2).astype(BF16) for m in range(n_q_slabs)]

    zero = jnp.zeros((WINDOW, LANES), F32)
    for ks in range(n_k_slabs):
        r0 = (n_q_slabs + ks) * WINDOW
        kr = rope(xn[r0:r0 + WINDOW] * kg2)
        kr_ref[:, ks * LANES:(ks + 1) * LANES] = kr
        kr_swapped = pltpu.roll(kr, HEAD_DIM, 1)
        klo_ref[2 * ks, WINDOW:2 * WINDOW, :] = jnp.where(first_head, kr, zero).astype(BF16)
        khi_ref[2 * ks, WINDOW:2 * WINDOW, :] = jnp.where(first_head, zero, kr_swapped).astype(BF16)
        klo_ref[2 * ks + 1, WINDOW:2 * WINDOW, :] = jnp.where(first_head, kr_swapped, zero).astype(BF16)
        khi_ref[2 * ks + 1, WINDOW:2 * WINDOW, :] = jnp.where(first_head, zero, kr).astype(BF16)

        vs_t = z_ref[:, V_OFF + ks * LANES:V_OFF + (ks + 1) * LANES].T.astype(BF16)
        vt_ref[2 * ks, 0:HEAD_DIM, WINDOW:2 * WINDOW] = vs_t[0:HEAD_DIM]
        vt_ref[2 * ks + 1, 0:HEAD_DIM, WINDOW:2 * WINDOW] = vs_t[HEAD_DIM:2 * HEAD_DIM]

    key = lax.broadcasted_iota(jnp.int32, (2 * WINDOW, WINDOW), 0)
    qry = lax.broadcasted_iota(jnp.int32, (2 * WINDOW, WINDOW), 1)
    diff = qry + WINDOW - key
    band = (diff >= 0) & (diff <= jnp.where(n > 0, WINDOW - 1, qry))
    bias = jnp.where(band, 0.0, NEG)
    bias2 = jnp.concatenate([bias, bias], axis=1)

    def conv_chunk(c, anchor):
        cl = slice(c * LANES, (c + 1) * LANES)
        bits = pltpu.bitcast(anchor, jnp.uint32)
        zero = pltpu.bitcast(lax.shift_right_logical(lax.shift_right_logical(bits, jnp.uint32(16)), jnp.uint32(16)),
                             F32)[0:1, :]
        y = cb_ref[:, cl]
        for phase, taps in enumerate(_conv_taps_by_row_phase()):
            rows_needed = WINDOW + (SUBLANES if phase else 0)
            part = None
            for j, tile in taps:
                term = (cw_ref[j:j + 1, cl] + zero) * ubuf_ref[tile * SUBLANES:tile * SUBLANES + rows_needed, cl]
                part = term if part is None else part + term
            if phase:
                part = pltpu.roll(part, rows_needed - phase, 0)[0:WINDOW]
            y = y + part
        return y

    assert CONV_CH // LANES == 2 * N_KV_HEADS
    y_chunks = []
    anchors = []
    for h in range(N_KV_HEADS):
        q2 = jnp.concatenate([q_slabs[2 * h], q_slabs[2 * h + 1]], axis=0)
        normed_t = []
        for second, k_ref in ((0, klo_ref), (1, khi_ref)):
            s_t = lax.dot_general(k_ref[h], q2, (((1,), (1,)), ((), ())), preferred_element_type=F32)
            s_t = s_t + bias2
            sink_row = jnp.concatenate(
                [jnp.full((1, WINDOW), sinks_ref[GROUP * h + 2 * r + second], F32) for r in range(2)], axis=1)
            m = jnp.maximum(jnp.max(s_t, axis=0, keepdims=True), sink_row)
            p_t = jnp.exp(s_t - m).astype(BF16)
            o_t = jnp.dot(vt_ref[h], p_t, preferred_element_type=F32)
            den = o_t[HEAD_DIM:HEAD_DIM + 1] + jnp.exp(sink_row - m)
            normed_t.append(o_t[0:HEAD_DIM] * (1.0 / den))
            tick(SIDE_JOBS_PER_SOFTMAX)
            anchors.append(normed_t[-1][0:SUBLANES, 0:LANES])
            anchor = anchors[-2] if len(anchors) > 1 else s_t[0:SUBLANES, 0:LANES]
            y_chunks.append(conv_chunk(2 * h + second, anchor))
        for r in range(2):
            a_t = jnp.concatenate([normed_t[0][:, r * WINDOW:(r + 1) * WINDOW],
                                   normed_t[1][:, r * WINDOW:(r + 1) * WINDOW]], axis=0)
            m_out = 2 * h + r
            mix_ref[:, m_out * LANES:(m_out + 1) * LANES] = a_t.T.astype(BF16)

    conv = jnp.concatenate(y_chunks, axis=-1)
    mix_ref[:, ATTN_WIDTH:] = _layernorm_swish(conv, lg_ref[...], lb_ref[...]).astype(BF16)
    tick(len(side_jobs))


def _weight_chunk_copy(w_hbm, stage_ref, sem, c, slot):
    cols = w_hbm.shape[1]
    return pltpu.make_async_copy(w_hbm.at[pl.ds(c * CAST_ROWS, CAST_ROWS), :],
                                 stage_ref.at[slot, :, pl.ds(0, cols)], sem.at[slot])


def _load_weight_bf16(w_hbm, w_ref, stage_ref, sem):
    n_chunks = w_ref.shape[0] // CAST_ROWS
    cols = w_ref.shape[1]
    _weight_chunk_copy(w_hbm, stage_ref, sem, 0, 0).start()
    for c in range(n_chunks):
        slot = c % 2
        if c + 1 < n_chunks:
            _weight_chunk_copy(w_hbm, stage_ref, sem, c + 1, 1 - slot).start()
        _weight_chunk_copy(w_hbm, stage_ref, sem, c, slot).wait()
        w_ref[c * CAST_ROWS:(c + 1) * CAST_ROWS, :] = stage_ref[slot, :, 0:cols].astype(BF16)


def _front_kernel(sinks_ref, x_next_ref, x_res_ref, xs_ref, g_ref, win_hbm, wout_hbm, cos_ref, sin_ref,
                  qg_ref, kg_ref, cw_ref, cb_ref, lg_ref, lb_ref,
                  x1_ref, zs_ref, nk_ref, nv_ref, nc_ref, wout_bf_hbm,
                  win_ref, wout_ref, stage_ref, load_sem, export_sem,
                  z_next_ref, z_ref, mix_ref, kr_ref, klo_ref, khi_ref, vt_ref, ubuf_ref, *, nb):
    t = pl.program_id(0)
    drain = pl.num_programs(0) - 1
    n = lax.rem(t, nb)
    export = pltpu.make_async_copy(wout_ref, wout_bf_hbm, export_sem.at[0])

    @pl.when(t == 0)
    def _():
        _load_weight_bf16(win_hbm, win_ref, stage_ref, load_sem)
        _load_weight_bf16(wout_hbm, wout_ref, stage_ref, load_sem)
        export.start()
        rows0 = x_res_ref.shape[0]
        z_both = _norm_dot(jnp.concatenate([x_res_ref[...], xs_ref[...]], axis=0), g_ref[...], win_ref[...])
        z_ref[...] = z_both[0:rows0]
        zs_ref[...] = z_both[rows0:]
        mix_ref[...] = jnp.zeros(mix_ref.shape, BF16)

    @pl.when(t < drain)
    def _():
        _mixer_carry(n, klo_ref, khi_ref, vt_ref, ubuf_ref)

    @pl.when(t < drain)
    def _():
        xn_next = _rms_rows(x_next_ref[...], g_ref[...]).astype(BF16)
        mix_prev = mix_ref[...]
        jobs = []
        for c in range(IN_COLS // PROJ_CHUNK):
            def in_job(cols=slice(c * PROJ_CHUNK, (c + 1) * PROJ_CHUNK)):
                z_next_ref[:, cols] = jnp.dot(xn_next, win_ref[:, cols], preferred_element_type=F32)
            jobs.append(in_job)
        for c in range(x1_ref.shape[1] // PROJ_CHUNK):
            def out_job(cols=slice(c * PROJ_CHUNK, (c + 1) * PROJ_CHUNK)):
                x1_ref[:, cols] = x_res_ref[:, cols] + jnp.dot(mix_prev, wout_ref[:, cols],
                                                               preferred_element_type=F32)
            jobs.append(out_job)
        _mixer_block(n, z_ref, mix_ref, kr_ref, sinks_ref, cos_ref, sin_ref, qg_ref, kg_ref, cw_ref,
                     cb_ref, lg_ref, lb_ref, klo_ref, khi_ref, vt_ref, ubuf_ref, side_jobs=jobs)

    @pl.when((n == nb - 1) & (t < drain))
    def _():
        nk_ref[0] = kr_ref[...].T
        nv_ref[0] = z_ref[:, V_OFF:V_OFF + KV_DIM].T
        nc_ref[0] = ubuf_ref[CONV_PAD + WINDOW - CONV_STATE:CONV_PAD + WINDOW, :]

    @pl.when(t < drain)
    def _():
        z_ref[...] = z_next_ref[...]

    @pl.when(t == drain)
    def _():
        x1_ref[...] = x_res_ref[...] + jnp.dot(mix_ref[...], wout_ref[...], preferred_element_type=F32)
        export.wait()


def _prompt_front(x, xs, g, w_in, w_out, batch, seq, cos4, sin4, qg, kg, sinks, cw, cb, lg, lb):
    nb = seq // WINDOW
    steps = batch * nb
    d = x.shape[1]
    ms = xs.shape[0]
    const = lambda t: (0, 0)
    behind = lambda t: (jnp.maximum(t - 1, 0), 0)
    ahead = lambda t: (jnp.minimum(t + 1, steps - 1), 0)
    pos = lambda t: (lax.rem(t, nb), 0)
    per_batch = lambda t: (jnp.minimum(t // nb, batch - 1), 0, 0)
    assert w_in.shape[0] % CAST_ROWS == 0 and w_out.shape[0] % CAST_ROWS == 0
    stage_cols = max(w_in.shape[1], w_out.shape[1])
    return pl.pallas_call(
        functools.partial(_front_kernel, nb=nb),
        grid=(steps + 1,),
        in_specs=[
            pl.BlockSpec(memory_space=pltpu.SMEM),
            pl.BlockSpec((WINDOW, d), ahead),
            pl.BlockSpec((WINDOW, d), behind),
            pl.BlockSpec((ms, d), const),
            pl.BlockSpec((1, d), const),
            pl.BlockSpec(memory_space=pl.ANY),
            pl.BlockSpec(memory_space=pl.ANY),
            pl.BlockSpec((WINDOW, LANES), pos),
            pl.BlockSpec((WINDOW, LANES), pos),
            pl.BlockSpec((1, HEAD_DIM), const),
            pl.BlockSpec((1, HEAD_DIM), const),
            pl.BlockSpec((CONV_WIDTH, CONV_CH), const),
            pl.BlockSpec((1, CONV_CH), const),
            pl.BlockSpec((1, CONV_CH), const),
            pl.BlockSpec((1, CONV_CH), const),
        ],
        out_specs=[
            pl.BlockSpec((WINDOW, d), behind),
            pl.BlockSpec((ms, IN_COLS), const),
            pl.BlockSpec((1, KV_DIM, WINDOW), per_batch),
            pl.BlockSpec((1, KV_DIM, WINDOW), per_batch),
            pl.BlockSpec((1, CONV_STATE, CONV_CH), per_batch),
            pl.BlockSpec(memory_space=pl.ANY),
        ],
        out_shape=[
            jax.ShapeDtypeStruct((steps * WINDOW, d), F32),
            jax.ShapeDtypeStruct((ms, IN_COLS), F32),
            jax.ShapeDtypeStruct((batch, KV_DIM, WINDOW), F32),
            jax.ShapeDtypeStruct((batch, KV_DIM, WINDOW), F32),
            jax.ShapeDtypeStruct((batch, CONV_STATE, CONV_CH), F32),
            jax.ShapeDtypeStruct(w_out.shape, BF16),
        ],
        scratch_shapes=[
            pltpu.VMEM(w_in.shape, BF16),
            pltpu.VMEM(w_out.shape, BF16),
            pltpu.VMEM((2, CAST_ROWS, stage_cols), F32),
            pltpu.SemaphoreType.DMA((2,)),
            pltpu.SemaphoreType.DMA((1,)),
            pltpu.VMEM((WINDOW, IN_COLS), F32),
            pltpu.VMEM((WINDOW, IN_COLS), F32),
            pltpu.VMEM((WINDOW, ATTN_WIDTH + CONV_CH), BF16),
            pltpu.VMEM((WINDOW, KV_DIM), F32),
            pltpu.VMEM((N_KV_HEADS, 2 * WINDOW, LANES), BF16),
            pltpu.VMEM((N_KV_HEADS, 2 * WINDOW, LANES), BF16),
            pltpu.VMEM((N_KV_HEADS, VT_ROWS, 2 * WINDOW), BF16),
            pltpu.VMEM((CONV_PAD + WINDOW, CONV_CH), F32),
        ],
        compiler_params=pltpu.CompilerParams(
            dimension_semantics=("arbitrary",), vmem_limit_bytes=VMEM_LIMIT_BYTES),
        name="prompt_front",
    )(sinks, x, x, xs, g, w_in, w_out, cos4, sin4, qg, kg, cw, cb, lg, lb)


def _sample_mixer_kernel(sinks_ref, z_ref, xs_ref, ck_ref, cv_ref, st_ref, cos_ref, sin_ref,
                         qg_ref, kg_ref, cw_ref, cb_ref, lg_ref, lb_ref, wout_ref,
                         x1_ref, nk_ref, nv_ref, nc_ref, qexp_ref, knew_ref, a_ref):
    bs = z_ref.shape[0]
    cos2 = cos_ref[0:1, 0:HEAD_DIM]
    sin2 = sin_ref[0:1, 0:HEAD_DIM]

    q = jnp.concatenate([z_ref[:, h * HEAD_DIM:(h + 1) * HEAD_DIM] for h in range(N_HEADS)], axis=0)
    qn = _norm_rope(q, qg_ref[...], cos2, sin2).astype(BF16)
    tile_r = lax.broadcasted_iota(jnp.int32, (HEAD_DIM, KV_DIM), 0)
    tile_c = lax.broadcasted_iota(jnp.int32, (HEAD_DIM, KV_DIM), 1)
    tile = jnp.where((tile_c & (HEAD_DIM - 1)) == tile_r, 1.0, 0.0).astype(BF16)
    qt = jnp.dot(qn, tile, preferred_element_type=F32)
    q_row = lax.broadcasted_iota(jnp.int32, (N_HEADS * bs, KV_DIM), 0)
    q_col = lax.broadcasted_iota(jnp.int32, (N_HEADS * bs, KV_DIM), 1)
    own = (q_row // (bs * GROUP)) == (q_col // HEAD_DIM)
    qexp = jnp.where(own, qt, 0.0)
    for half in range(KV_DIM // LANES):
        qexp_ref[half] = qexp[:, half * LANES:(half + 1) * LANES]

    k = jnp.concatenate([z_ref[:, K_OFF + h * HEAD_DIM:K_OFF + (h + 1) * HEAD_DIM]
                         for h in range(N_KV_HEADS)], axis=0)
    kn = _norm_rope(k, kg_ref[...], cos2, sin2)
    for h in range(N_KV_HEADS):
        knew_ref[:, h * HEAD_DIM:(h + 1) * HEAD_DIM] = kn[h * bs:(h + 1) * bs]
    knew = knew_ref[...]
    vnew = z_ref[:, V_OFF:V_OFF + KV_DIM]
    knew_r = knew.astype(BF16).astype(F32)
    vnew_r = vnew.astype(BF16).astype(F32)
    pad = jnp.zeros((WINDOW - bs, KV_DIM), F32)
    knew_t = jnp.concatenate([knew, pad], axis=0).T
    vnew_t = jnp.concatenate([vnew, pad], axis=0).T
    newest = lax.broadcasted_iota(jnp.int32, (KV_DIM, WINDOW), 1) == WINDOW - 1

    n_rows = bs * N_HEADS
    q_rows, s_rows = [], []
    for b in range(bs):
        heads_of_b = pl.ds(b, N_HEADS, stride=bs)
        qb = jnp.concatenate([qexp_ref[half, heads_of_b, :] for half in range(KV_DIM // LANES)], axis=1)
        q_rows.append(qb)
        s_rows.append(jnp.dot(qb.astype(BF16), ck_ref[b].astype(BF16), preferred_element_type=F32))
    q_all = jnp.concatenate(q_rows, axis=0)
    s = jnp.concatenate(s_rows, axis=0) * (HEAD_DIM ** -0.5)
    knew_rows = jnp.concatenate([jnp.broadcast_to(knew_r[b:b + 1, :], (N_HEADS, KV_DIM)) for b in range(bs)], axis=0)
    s_new = jnp.sum(q_all * knew_rows, axis=-1, keepdims=True) * (HEAD_DIM ** -0.5)
    sink_col = jnp.concatenate([jnp.full((1, 1), sinks_ref[hh], F32) for hh in range(N_HEADS)] * bs, axis=0)
    key_ok = lax.broadcasted_iota(jnp.int32, (n_rows, WINDOW), 1) >= 1
    s = jnp.where(key_ok, s, NEG)
    m = jnp.maximum(jnp.maximum(jnp.max(s, axis=-1, keepdims=True), s_new), sink_col)
    p = jnp.exp(s - m)
    p_new = jnp.exp(s_new - m)
    den = jnp.sum(p, axis=-1, keepdims=True) + p_new + jnp.exp(sink_col - m)
    p = (p / den).astype(BF16)
    p_new = (p_new / den).astype(BF16).astype(F32)

    o_row = lax.broadcasted_iota(jnp.int32, (N_HEADS, KV_DIM), 0)
    o_col = lax.broadcasted_iota(jnp.int32, (N_HEADS, KV_DIM), 1)
    o_own = (o_row // GROUP) == (o_col // HEAD_DIM)
    fold_r = lax.broadcasted_iota(jnp.int32, (KV_DIM, LANES), 0)
    fold_c = lax.broadcasted_iota(jnp.int32, (KV_DIM, LANES), 1)
    fold = jnp.where((fold_r & (HEAD_DIM - 1)) == fold_c, 1.0, 0.0).astype(BF16)
    for b in range(bs):
        rows_b = slice(b * N_HEADS, (b + 1) * N_HEADS)
        kc = ck_ref[b]
        vc = cv_ref[b]
        o = lax.dot_general(p[rows_b], vc.astype(BF16), (((1,), (1,)), ((), ())), preferred_element_type=F32)
        o = o + p_new[rows_b] * vnew_r[b:b + 1, :]
        o = jnp.where(o_own, o, 0.0).astype(BF16)
        a_ref[pl.ds(b, N_HEADS, stride=bs), :] = jnp.dot(o, fold, preferred_element_type=F32)

        nk_ref[b] = jnp.where(newest, pltpu.roll(knew_t, WINDOW - 1 - b, 1), pltpu.roll(kc, WINDOW - 1, 1))
        nv_ref[b] = jnp.where(newest, pltpu.roll(vnew_t, WINDOW - 1 - b, 1), pltpu.roll(vc, WINDOW - 1, 1))

    u = z_ref[:, VAL_OFF:VAL_OFF + CONV_CH] * jax.nn.sigmoid(z_ref[:, GATE_OFF:GATE_OFF + CONV_CH])
    y = cb_ref[...] + u * cw_ref[CONV_STATE:CONV_WIDTH, :]
    for j in range(CONV_STATE):
        y = y + st_ref[j] * cw_ref[j:j + 1, :]
        nc_ref[j] = st_ref[j + 1] if j + 1 < CONV_STATE else u
    c = _layernorm_swish(y, lg_ref[...], lb_ref[...])

    x1 = xs_ref[...] + jnp.dot(c.astype(BF16), wout_ref[ATTN_WIDTH:, :], preferred_element_type=F32)
    for h in range(N_HEADS):
        x1 = x1 + jnp.dot(a_ref[h * bs:(h + 1) * bs, 0:HEAD_DIM].astype(BF16),
                          wout_ref[h * HEAD_DIM:(h + 1) * HEAD_DIM, :], preferred_element_type=F32)
    x1_ref[...] = x1


def _sample_mixer(zs, xs, cache_k, cache_v, state_t, cos_t, sin_t, pos_row, qg, kg, sinks, cw, cb, lg, lb, w_out,
                  *, bs):
    nsamp, d = xs.shape
    const = lambda i: (0, 0)
    step = lambda i: (i, 0)
    step3 = lambda i: (i, 0, 0)
    taps = lambda i: (0, i, 0)
    assert pos_row % SUBLANES == 0
    decode_pos = lambda i: (pos_row // SUBLANES, 0)
    return pl.pallas_call(
        _sample_mixer_kernel,
        grid=(nsamp // bs,),
        in_specs=[
            pl.BlockSpec(memory_space=pltpu.SMEM),
            pl.BlockSpec((bs, IN_COLS), step),
            pl.BlockSpec((bs, d), step),
            pl.BlockSpec((bs, KV_DIM, WINDOW), step3),
            pl.BlockSpec((bs, KV_DIM, WINDOW), step3),
            pl.BlockSpec((CONV_STATE, bs, CONV_CH), taps),
            pl.BlockSpec((SUBLANES, LANES), decode_pos),
            pl.BlockSpec((SUBLANES, LANES), decode_pos),
            pl.BlockSpec((1, HEAD_DIM), const),
            pl.BlockSpec((1, HEAD_DIM), const),
            pl.BlockSpec((CONV_WIDTH, CONV_CH), const),
            pl.BlockSpec((1, CONV_CH), const),
            pl.BlockSpec((1, CONV_CH), const),
            pl.BlockSpec((1, CONV_CH), const),
            pl.BlockSpec((ATTN_WIDTH + CONV_CH, d), const, pipeline_mode=pl.Buffered(1)),
        ],
        out_specs=[
            pl.BlockSpec((bs, d), step),
            pl.BlockSpec((bs, KV_DIM, WINDOW), step3),
            pl.BlockSpec((bs, KV_DIM, WINDOW), step3),
            pl.BlockSpec((CONV_STATE, bs, CONV_CH), taps),
        ],
        out_shape=[
            jax.ShapeDtypeStruct((nsamp, d), F32),
            jax.ShapeDtypeStruct((nsamp, KV_DIM, WINDOW), F32),
            jax.ShapeDtypeStruct((nsamp, KV_DIM, WINDOW), F32),
            jax.ShapeDtypeStruct((CONV_STATE, nsamp, CONV_CH), F32),
        ],
        scratch_shapes=[
            pltpu.VMEM((KV_DIM // LANES, N_HEADS * bs, LANES), F32),
            pltpu.VMEM((bs, KV_DIM), F32),
            pltpu.VMEM((N_HEADS * bs, LANES), F32),
        ],
        compiler_params=pltpu.CompilerParams(
            dimension_semantics=("arbitrary",), vmem_limit_bytes=VMEM_LIMIT_BYTES),
        name="sample_mixer",
    )(sinks, zs, xs, cache_k, cache_v, state_t, cos_t, sin_t, qg, kg, cw, cb, lg, lb, w_out)


def _rope_tables(pos, heads_per_row=1):
    reps = 2 * heads_per_row
    inv = jnp.tile(ROPE_THETA ** (-jnp.arange(HALF, dtype=F32) / HALF), reps)
    sign = jnp.tile(jnp.concatenate([-jnp.ones((HALF,), F32), jnp.ones((HALF,), F32)]), heads_per_row)
    ang = pos.astype(F32)[:, None] * inv[None, :]
    return jnp.cos(ang), jnp.sin(ang) * sign[None, :]


def _to_feature_major(cache):
    rows = cache.shape[0]
    return jnp.transpose(cache, (0, 2, 3, 1)).reshape(rows, KV_DIM, WINDOW)


def _from_feature_major(cache_t):
    rows = cache_t.shape[0]
    return jnp.transpose(cache_t.reshape(rows, N_KV_HEADS, HEAD_DIM, WINDOW), (0, 3, 1, 2))


def kernel(x_prompt, x_sample, cache_k, cache_v, state_conv, norm_mix_g, w_in, q_norm_g, k_norm_g,
           sinks, conv_w, conv_b, conv_ln_g, conv_ln_b, w_out, norm_mlp_g, w_up, w_down):
    batch, seq, d = x_prompt.shape
    nsamp, dec_seq, _ = x_sample.shape
    depth = w_in.shape[0]
    assert dec_seq == 1 and seq % WINDOW == 0 and cache_k.shape[2] == WINDOW

    row = jnp.arange(seq + WINDOW)
    cos4, sin4 = _rope_tables(jnp.where(row < seq, row, PAST_LEN + row - seq), heads_per_row=LANES // HEAD_DIM)

    xp = x_prompt.reshape(batch * seq, d)
    xs = x_sample.reshape(nsamp * dec_seq, d)
    kp, vp, cp, ksm, vsm, csm = [], [], [], [], [], []
    for l in range(depth):
        qg, kg = q_norm_g[l:l + 1], k_norm_g[l:l + 1]
        cb, lg, lb = conv_b[l:l + 1], conv_ln_g[l:l + 1], conv_ln_b[l:l + 1]
        x1, zs, nk, nv, nc, w_out_bf = _prompt_front(xp, xs, norm_mix_g[l:l + 1], w_in[l], w_out[l], batch, seq,
                                                     cos4, sin4, qg, kg, sinks[l], conv_w[l], cb, lg, lb)
        kp.append(_from_feature_major(nk))
        vp.append(_from_feature_major(nv))
        cp.append(nc)

        x1s, nk, nv, nc = _sample_mixer(
            zs, xs, _to_feature_major(cache_k[l]), _to_feature_major(cache_v[l]),
            jnp.transpose(state_conv[l], (1, 0, 2)), cos4, sin4, seq, qg, kg, sinks[l], conv_w[l], cb, lg, lb,
            w_out_bf, bs=SAMPLE_BLOCK)
        ksm.append(_from_feature_major(nk))
        vsm.append(_from_feature_major(nv))
        csm.append(jnp.transpose(nc, (1, 0, 2)))

        xp, xs = _mlp(x1, x1s, norm_mlp_g[l:l + 1], w_up[l], w_down[l])

    return (xp.reshape(batch, seq, d), xs.reshape(nsamp, dec_seq, d),
            jnp.stack(kp), jnp.stack(vp), jnp.stack(cp),
            jnp.stack(ksm), jnp.stack(vsm), jnp.stack(csm))
```
